```python
import math
import jax, jax.numpy as jnp
from jax import lax
import numpy as np

D_MODEL = 1024
BATCH = 8
SEQ = 2048
DEPTH = 2
DEC_BATCH = 128
DEC_SEQ = 8
PAST_LEN = 2048
PAGE_SIZE = 128

HEAD_DIM = 64
FOX_HEADS = D_MODEL // HEAD_DIM
NSA_HEADS = D_MODEL // HEAD_DIM
NSA_KV_HEADS = 4
NSA_GROUP = NSA_HEADS // NSA_KV_HEADS
CMP_LEN = 32
CMP_STRIDE = 16
SEL_BLOCK = 64
SEL_TOPK = 16
WINDOW = 512
N_BUCKETS = 32
MAX_DISTANCE = 128
PEER_HEADS = 8
PEER_TOPK = 16
N_KEYS = 128
N_EXPERTS = N_KEYS * N_KEYS
PEER_DKEY = 256
PLE_DIM = 256
FOX_Q_BLOCK = 128
NSA_Q_BLOCK = 64
PEER_CHUNK = 256
FORGET_BIAS = 3.0
FORCE_SCORE = 1e4
RMS_EPS = 1e-6
NEG_INF = -1e30
TINY = 1e-30
N_FOX_LAYERS = (DEPTH + 1) // 2
N_NSA_LAYERS = DEPTH // 2
FOX_IN = 3 * FOX_HEADS * HEAD_DIM + FOX_HEADS
NSA_IN = NSA_HEADS * HEAD_DIM + 6 * NSA_KV_HEADS * HEAD_DIM + 3 * NSA_HEADS

kernel_name = 'hybrid_fox_nsa_peer_step'


def rmsnorm(x, g):
    xf = x.astype(jnp.float32)
    y = xf * lax.rsqrt(jnp.mean(xf * xf, axis=-1, keepdims=True) + RMS_EPS)
    return (y * g.astype(jnp.float32)).astype(x.dtype)


def masked_softmax(s, mask):
    s = jnp.where(mask, s, NEG_INF)
    m = jnp.max(s, axis=-1, keepdims=True)
    e = jnp.where(mask, jnp.exp(s - m), 0.0)
    return e / jnp.maximum(jnp.sum(e, axis=-1, keepdims=True), TINY)


def t5_bucket(dist):
    n = jnp.maximum(dist, 0)
    max_exact = N_BUCKETS // 2
    nf = jnp.maximum(n, max_exact).astype(jnp.float32)
    large = max_exact + (jnp.log(nf / max_exact) / math.log(MAX_DISTANCE / max_exact)
                         * (N_BUCKETS - max_exact)).astype(jnp.int32)
    large = jnp.minimum(large, N_BUCKETS - 1)
    return jnp.where(n < max_exact, n, large)


def fox_project(h, w_in, b_f):
    B, T, _ = h.shape
    n = FOX_HEADS * HEAD_DIM
    proj = h @ w_in
    q = proj[..., :n].reshape(B, T, FOX_HEADS, HEAD_DIM)
    k = proj[..., n:2 * n].reshape(B, T, FOX_HEADS, HEAD_DIM)
    v = proj[..., 2 * n:3 * n].reshape(B, T, FOX_HEADS, HEAD_DIM)
    logf = jax.nn.log_sigmoid((proj[..., 3 * n:] + b_f).astype(jnp.float32))
    return q, k, v, logf


def fox_attend(q, k, v, c_q, c_k, q_pos, k_pos):
    s = jnp.einsum('bqhd,bkhd->bhqk', q, k).astype(jnp.float32) * (HEAD_DIM ** -0.5)
    s = s + (jnp.swapaxes(c_q, 1, 2)[..., :, None] - jnp.swapaxes(c_k, 1, 2)[..., None, :])
    mask = k_pos[None, :] <= q_pos[:, None]
    p = masked_softmax(s, mask)
    return jnp.einsum('bhqk,bkhd->bqhd', p.astype(v.dtype), v)


def fox_prompt(h, w_in, b_f, w_out):
    B, S, _ = h.shape
    q, k, v, logf = fox_project(h, w_in, b_f)
    c = jnp.cumsum(logf, axis=1)
    k_pos = jnp.arange(S)

    def block(i):
        qs = i * FOX_Q_BLOCK
        qb = lax.dynamic_slice_in_dim(q, qs, FOX_Q_BLOCK, axis=1)
        cb = lax.dynamic_slice_in_dim(c, qs, FOX_Q_BLOCK, axis=1)
        return fox_attend(qb, k, v, cb, c, qs + jnp.arange(FOX_Q_BLOCK), k_pos)

    o = lax.map(block, jnp.arange(S // FOX_Q_BLOCK))
    o = jnp.swapaxes(o, 0, 1).reshape(B, S, FOX_HEADS * HEAD_DIM)
    return o @ w_out, k, v, logf


def fox_sample(h, w_in, b_f, w_out, k_pool, v_pool, f_pool, page_table):
    DB, T, _ = h.shape
    q, k, v, logf = fox_project(h, w_in, b_f)
    past = page_table.shape[1] * PAGE_SIZE
    k_all = jnp.concatenate([k_pool[page_table].reshape(DB, past, FOX_HEADS, HEAD_DIM).astype(k.dtype), k], axis=1)
    v_all = jnp.concatenate([v_pool[page_table].reshape(DB, past, FOX_HEADS, HEAD_DIM).astype(v.dtype), v], axis=1)
    f_all = jnp.concatenate([f_pool[page_table].reshape(DB, past, FOX_HEADS).astype(jnp.float32), logf], axis=1)
    c = jnp.cumsum(f_all, axis=1)
    o = fox_attend(q, k_all, v_all, c[:, past:], c, past + jnp.arange(T), jnp.arange(past + T))
    return o.reshape(DB, T, FOX_HEADS * HEAD_DIM) @ w_out, k, v, logf


def nsa_project(h, w_in):
    B, T, _ = h.shape
    nq = NSA_HEADS * HEAD_DIM
    nkv = NSA_KV_HEADS * HEAD_DIM
    proj = h @ w_in
    q = proj[..., :nq].reshape(B, T, NSA_KV_HEADS, NSA_GROUP, HEAD_DIM)
    kv = proj[..., nq:nq + 6 * nkv].reshape(B, T, 6, NSA_KV_HEADS, HEAD_DIM)
    gates = jax.nn.sigmoid(proj[..., nq + 6 * nkv:].astype(jnp.float32)).astype(h.dtype)
    gates = gates.reshape(B, T, NSA_KV_HEADS, NSA_GROUP, 3)
    return q, gates, kv[:, :, 0], kv[:, :, 1], kv[:, :, 2], kv[:, :, 3], kv[:, :, 4], kv[:, :, 5]


def compress(x, pe, w1, w2):
    B, T, G, dh = x.shape
    r = CMP_LEN // CMP_STRIDE
    n_chunks = T // CMP_STRIDE
    nc = n_chunks - r + 1
    ch = x.reshape(B, n_chunks, CMP_STRIDE, G, dh)
    blk = jnp.concatenate([ch[:, m:m + nc] for m in range(r)], axis=2)
    blk = blk + pe[:, None, :]
    flat = blk.transpose(0, 1, 3, 2, 4).reshape(B, nc, G, CMP_LEN * dh)
    return jax.nn.gelu(flat @ w1, approximate=False) @ w2


def to_blocks(x):
    B, T, G, dh = x.shape
    return x.reshape(B, T // SEL_BLOCK, SEL_BLOCK, G, dh).transpose(0, 3, 1, 2, 4)


def nsa_attend(q, gates, q_pos, kcmp, vcmp, cmp_end, ksb, vsb, kw, vw, kw_pos, rel_bias):
    B, Q, G, R, dh = q.shape
    NC = kcmp.shape[1]
    NS = ksb.shape[2]
    scale = HEAD_DIM ** -0.5
    tbl = rel_bias.reshape(N_BUCKETS, G, R)
    g_idx = jnp.arange(G)
    dist_c = q_pos[:, None] - cmp_end[None, :]
    s = jnp.einsum('bqgrd,bcgd->bgrqc', q, kcmp).astype(jnp.float32) * scale
    s = s + tbl[t5_bucket(dist_c)].transpose(2, 3, 0, 1)[None].astype(jnp.float32)
    p_cmp = masked_softmax(s, (dist_c >= 0)[None, None, None])
    o_cmp = jnp.einsum('bgrqc,bcgd->bqgrd', p_cmp.astype(vcmp.dtype), vcmp)
    c_start = jnp.arange(NC) * CMP_STRIDE
    b_start = jnp.arange(NS) * SEL_BLOCK
    overlap = ((c_start[:, None] < b_start[None, :] + SEL_BLOCK)
               & (c_start[:, None] + CMP_LEN > b_start[None, :])).astype(jnp.float32)
    imp = jnp.einsum('bgrqc,cn->bgqn', p_cmp, overlap)
    cur = q_pos // SEL_BLOCK
    blk = jnp.arange(NS)[None, :]
    valid = blk <= cur[:, None]
    forced = (blk == 0) | (blk == cur[:, None]) | (blk == cur[:, None] - 1)
    score = jnp.where(valid, jnp.where(forced, FORCE_SCORE, imp), -1.0)
    n_sel = min(SEL_TOPK, NS)
    _, idx = lax.top_k(score, n_sel)
    b_idx = jnp.arange(B)[:, None, None, None]
    gi = g_idx[None, :, None, None]
    ksel = ksb[b_idx, gi, idx].reshape(B, G, Q, n_sel * SEL_BLOCK, dh)
    vsel = vsb[b_idx, gi, idx].reshape(B, G, Q, n_sel * SEL_BLOCK, dh)
    kpos = (idx[..., None] * SEL_BLOCK + jnp.arange(SEL_BLOCK)).reshape(B, G, Q, n_sel * SEL_BLOCK)
    dist_s = q_pos[None, None, :, None] - kpos
    s = jnp.einsum('bqgrd,bgqkd->bgrqk', q, ksel).astype(jnp.float32) * scale
    s = s + jnp.moveaxis(tbl[t5_bucket(dist_s), gi], -1, 2).astype(jnp.float32)
    p = masked_softmax(s, (dist_s >= 0)[:, :, None])
    o_sel = jnp.einsum('bgrqk,bgqkd->bqgrd', p.astype(vsel.dtype), vsel)
    dist_w = q_pos[:, None] - kw_pos[None, :]
    mask_w = (dist_w >= 0) & (dist_w < WINDOW) & (kw_pos >= 0)[None, :]
    s = jnp.einsum('bqgrd,bkgd->bgrqk', q, kw).astype(jnp.float32) * scale
    s = s + tbl[t5_bucket(dist_w)].transpose(2, 3, 0, 1)[None].astype(jnp.float32)
    p = masked_softmax(s, mask_w[None, None, None])
    o_win = jnp.einsum('bgrqk,bkgd->bqgrd', p.astype(vw.dtype), vw)
    o = gates[..., 0:1] * o_cmp + gates[..., 1:2] * o_sel + gates[..., 2:3] * o_win
    return o.reshape(B, Q, G * R * dh)


def nsa_prompt(h, w_in, w_out, pe_k, w1_k, w2_k, pe_v, w1_v, w2_v, rel_bias):
    B, S, _ = h.shape
    q, gates, kc, vc, ks, vs, kw, vw = nsa_project(h, w_in)
    kcmp = compress(kc, pe_k, w1_k, w2_k)
    vcmp = compress(vc, pe_v, w1_v, w2_v)
    cmp_end = jnp.arange(kcmp.shape[1]) * CMP_STRIDE + (CMP_LEN - 1)
    ksb = to_blocks(ks)
    vsb = to_blocks(vs)
    kw_pad = jnp.pad(kw, ((0, 0), (WINDOW, 0), (0, 0), (0, 0)))
    vw_pad = jnp.pad(vw, ((0, 0), (WINDOW, 0), (0, 0), (0, 0)))

    def block(i):
        qs = i * NSA_Q_BLOCK
        qb = lax.dynamic_slice_in_dim(q, qs, NSA_Q_BLOCK, axis=1)
        gb = lax.dynamic_slice_in_dim(gates, qs, NSA_Q_BLOCK, axis=1)
        kwb = lax.dynamic_slice_in_dim(kw_pad, qs, WINDOW + NSA_Q_BLOCK, axis=1)
        vwb = lax.dynamic_slice_in_dim(vw_pad, qs, WINDOW + NSA_Q_BLOCK, axis=1)
        kw_pos = qs - WINDOW + jnp.arange(WINDOW + NSA_Q_BLOCK)
        return nsa_attend(qb, gb, qs + jnp.arange(NSA_Q_BLOCK), kcmp, vcmp, cmp_end,
                          ksb, vsb, kwb, vwb, kw_pos, rel_bias)

    o = lax.map(block, jnp.arange(S // NSA_Q_BLOCK))
    o = jnp.swapaxes(o, 0, 1).reshape(B, S, NSA_HEADS * HEAD_DIM)
    wb = min(WINDOW, S)
    return o @ w_out, kc, vc, ks, vs, kw[:, S - wb:], vw[:, S - wb:]


def nsa_sample(h, w_in, w_out, pe_k, w1_k, w2_k, pe_v, w1_v, w2_v, rel_bias,
               ck_pool, cv_pool, sk_pool, sv_pool, wk_buf, wv_buf, page_table):
    DB, T, _ = h.shape
    q, gates, kc, vc, ks, vs, kw, vw = nsa_project(h, w_in)
    past = page_table.shape[1] * PAGE_SIZE
    L = past + T
    Lp = -(-L // SEL_BLOCK) * SEL_BLOCK

    def full(pool, new):
        old = pool[page_table].reshape(DB, past, NSA_KV_HEADS, HEAD_DIM).astype(new.dtype)
        return jnp.pad(jnp.concatenate([old, new], axis=1), ((0, 0), (0, Lp - L), (0, 0), (0, 0)))

    kcmp = compress(full(ck_pool, kc), pe_k, w1_k, w2_k)
    vcmp = compress(full(cv_pool, vc), pe_v, w1_v, w2_v)
    cmp_end = jnp.arange(kcmp.shape[1]) * CMP_STRIDE + (CMP_LEN - 1)
    ksb = to_blocks(full(sk_pool, ks))
    vsb = to_blocks(full(sv_pool, vs))
    wb = wk_buf.shape[1]
    kw_all = jnp.concatenate([wk_buf.astype(kw.dtype), kw], axis=1)
    vw_all = jnp.concatenate([wv_buf.astype(vw.dtype), vw], axis=1)
    kw_pos = past - wb + jnp.arange(wb + T)
    o = nsa_attend(q, gates, past + jnp.arange(T), kcmp, vcmp, cmp_end, ksb, vsb,
                   kw_all, vw_all, kw_pos, rel_bias)
    return o @ w_out, kc, vc, ks, vs, kw_all[:, T:], vw_all[:, T:]


def peer(x, wq, sub_k1, sub_k2, u, v):
    shp = x.shape
    xt = x.reshape(-1, shp[-1])
    n = xt.shape[0]
    n_chunks = -(-n // PEER_CHUNK)
    xt = jnp.pad(xt, ((0, n_chunks * PEER_CHUNK - n), (0, 0)))
    xc = xt.reshape(n_chunks, PEER_CHUNK, shp[-1])
    half = PEER_DKEY // 2

    def chunk(xb):
        qv = (xb @ wq).reshape(PEER_CHUNK, PEER_HEADS, PEER_DKEY)
        s1 = jnp.einsum('chd,hkd->chk', qv[..., :half], sub_k1).astype(jnp.float32)
        s2 = jnp.einsum('chd,hkd->chk', qv[..., half:], sub_k2).astype(jnp.float32)
        v1, i1 = lax.top_k(s1, PEER_TOPK)
        v2, i2 = lax.top_k(s2, PEER_TOPK)
        cand = (v1[..., :, None] + v2[..., None, :]).reshape(PEER_CHUNK, PEER_HEADS, PEER_TOPK * PEER_TOPK)
        sv, si = lax.top_k(cand, PEER_TOPK)
        e = (jnp.take_along_axis(i1, si // PEER_TOPK, axis=-1) * N_KEYS
             + jnp.take_along_axis(i2, si % PEER_TOPK, axis=-1))
        w = jax.nn.softmax(sv, axis=-1)
        act = jax.nn.gelu(jnp.einsum('chkd,cd->chk', u[e], xb).astype(jnp.float32), approximate=False)
        return jnp.einsum('chk,chkd->cd', (w * act).astype(xb.dtype), v[e])

    y = lax.map(chunk, xc).reshape(n_chunks * PEER_CHUNK, shp[-1])[:n]
    return y.reshape(shp)


def ple_add(x, p, g, w_proj, w_gate):
    gate = jax.nn.sigmoid((rmsnorm(x, g) @ w_gate).astype(jnp.float32)).astype(x.dtype)
    return x + gate * (p @ w_proj)


def setup_inputs(seed: int = 0) -> dict:
    key = jax.random.key(seed)
    ks = iter(jax.random.split(key, 64))

    def nrm(shape, scale=1.0):
        return jax.random.normal(next(ks), shape, jnp.float32) * scale

    d = D_MODEL
    n_pages = PAST_LEN // PAGE_SIZE
    n_pool = (DEC_BATCH * n_pages * 5) // 4
    wb = min(WINDOW, PAST_LEN)
    nf, nn_ = N_FOX_LAYERS, N_NSA_LAYERS
    page_table = jax.random.permutation(next(ks), n_pool)[:DEC_BATCH * n_pages].reshape(DEC_BATCH, n_pages).astype(jnp.int32)
    return {
        'x_prompt': nrm((BATCH, SEQ, d)),
        'x_sample': nrm((DEC_BATCH, DEC_SEQ, d)),
        'p_prompt': nrm((DEPTH, BATCH, SEQ, PLE_DIM)),
        'p_sample': nrm((DEPTH, DEC_BATCH, DEC_SEQ, PLE_DIM)),
        'cache_fox_k': nrm((nf, n_pool, PAGE_SIZE, FOX_HEADS, HEAD_DIM)),
        'cache_fox_v': nrm((nf, n_pool, PAGE_SIZE, FOX_HEADS, HEAD_DIM)),
        'cache_fox_logf': jax.nn.log_sigmoid(FORGET_BIAS + nrm((nf, n_pool, PAGE_SIZE, FOX_HEADS))),
        'cache_nsa_cmp_k': nrm((nn_, n_pool, PAGE_SIZE, NSA_KV_HEADS, HEAD_DIM)),
        'cache_nsa_cmp_v': nrm((nn_, n_pool, PAGE_SIZE, NSA_KV_HEADS, HEAD_DIM)),
        'cache_nsa_sel_k': nrm((nn_, n_pool, PAGE_SIZE, NSA_KV_HEADS, HEAD_DIM)),
        'cache_nsa_sel_v': nrm((nn_, n_pool, PAGE_SIZE, NSA_KV_HEADS, HEAD_DIM)),
        'cache_nsa_win_k': nrm((nn_, DEC_BATCH, wb, NSA_KV_HEADS, HEAD_DIM)),
        'cache_nsa_win_v': nrm((nn_, DEC_BATCH, wb, NSA_KV_HEADS, HEAD_DIM)),
        'page_table': page_table,
        'norm_mix': 1.0 + nrm((DEPTH, d), 0.1),
        'norm_ffn': 1.0 + nrm((DEPTH, d), 0.1),
        'norm_ple': 1.0 + nrm((DEPTH, d), 0.1),
        'norm_final': 1.0 + nrm((d,), 0.1),
        'fox_w_in': nrm((nf, d, FOX_IN), d ** -0.5),
        'fox_b_f': FORGET_BIAS + nrm((nf, FOX_HEADS), 0.5),
        'fox_w_out': nrm((nf, FOX_HEADS * HEAD_DIM, d), (FOX_HEADS * HEAD_DIM) ** -0.5),
        'nsa_w_in': nrm((nn_, d, NSA_IN), d ** -0.5),
        'nsa_w_out': nrm((nn_, NSA_HEADS * HEAD_DIM, d), (NSA_HEADS * HEAD_DIM) ** -0.5),
        'cmp_pe_k': nrm((nn_, CMP_LEN, HEAD_DIM), 0.1),
        'cmp_w1_k': nrm((nn_, CMP_LEN * HEAD_DIM, HEAD_DIM), (CMP_LEN * HEAD_DIM) ** -0.5),
        'cmp_w2_k': nrm((nn_, HEAD_DIM, HEAD_DIM), HEAD_DIM ** -0.5),
        'cmp_pe_v': nrm((nn_, CMP_LEN, HEAD_DIM), 0.1),
        'cmp_w1_v': nrm((nn_, CMP_LEN * HEAD_DIM, HEAD_DIM), (CMP_LEN * HEAD_DIM) ** -0.5),
        'cmp_w2_v': nrm((nn_, HEAD_DIM, HEAD_DIM), HEAD_DIM ** -0.5),
        'rel_bias': nrm((N_BUCKETS, NSA_HEADS), 0.5),
        'peer_wq': nrm((DEPTH, d, PEER_HEADS * PEER_DKEY), d ** -0.5),
        'peer_sub_k1': nrm((DEPTH, PEER_HEADS, N_KEYS, PEER_DKEY // 2), (PEER_DKEY // 2) ** -0.5),
        'peer_sub_k2': nrm((DEPTH, PEER_HEADS, N_KEYS, PEER_DKEY // 2), (PEER_DKEY // 2) ** -0.5),
        'peer_u': nrm((DEPTH, N_EXPERTS, d), d ** -0.5),
        'peer_v': nrm((DEPTH, N_EXPERTS, d), PEER_HEADS ** -0.5),
        'ple_w_proj': nrm((DEPTH, PLE_DIM, d), PLE_DIM ** -0.5),
        'ple_w_gate': nrm((DEPTH, d, d), d ** -0.5),
    }


def reference(x_prompt, x_sample, p_prompt, p_sample, cache_fox_k, cache_fox_v, cache_fox_logf,
              cache_nsa_cmp_k, cache_nsa_cmp_v, cache_nsa_sel_k, cache_nsa_sel_v,
              cache_nsa_win_k, cache_nsa_win_v, page_table,
              norm_mix, norm_ffn, norm_ple, norm_final,
              fox_w_in, fox_b_f, fox_w_out, nsa_w_in, nsa_w_out,
              cmp_pe_k, cmp_w1_k, cmp_w2_k, cmp_pe_v, cmp_w1_v, cmp_w2_v, rel_bias,
              peer_wq, peer_sub_k1, peer_sub_k2, peer_u, peer_v, ple_w_proj, ple_w_gate):
    xp, xs = x_prompt, x_sample
    fox_p, fox_s, nsa_p, nsa_s = [], [], [], []
    for i in range(DEPTH):
        hp = rmsnorm(xp, norm_mix[i])
        hs = rmsnorm(xs, norm_mix[i])
        j = i // 2
        if i % 2 == 0:
            yp, *stp = fox_prompt(hp, fox_w_in[j], fox_b_f[j], fox_w_out[j])
            ys, *sts = fox_sample(hs, fox_w_in[j], fox_b_f[j], fox_w_out[j],
                                  cache_fox_k[j], cache_fox_v[j], cache_fox_logf[j], page_table)
            fox_p.append(stp)
            fox_s.append(sts)
        else:
            cw = (cmp_pe_k[j], cmp_w1_k[j], cmp_w2_k[j], cmp_pe_v[j], cmp_w1_v[j], cmp_w2_v[j])
            yp, *stp = nsa_prompt(hp, nsa_w_in[j], nsa_w_out[j], *cw, rel_bias)
            ys, *sts = nsa_sample(hs, nsa_w_in[j], nsa_w_out[j], *cw, rel_bias,
                                  cache_nsa_cmp_k[j], cache_nsa_cmp_v[j], cache_nsa_sel_k[j],
                                  cache_nsa_sel_v[j], cache_nsa_win_k[j], cache_nsa_win_v[j], page_table)
            nsa_p.append(stp)
            nsa_s.append(sts)
        xp = xp + yp
        xs = xs + ys
        xp = xp + peer(rmsnorm(xp, norm_ffn[i]), peer_wq[i], peer_sub_k1[i], peer_sub_k2[i], peer_u[i], peer_v[i])
        xs = xs + peer(rmsnorm(xs, norm_ffn[i]), peer_wq[i], peer_sub_k1[i], peer_sub_k2[i], peer_u[i], peer_v[i])
        xp = ple_add(xp, p_prompt[i], norm_ple[i], ple_w_proj[i], ple_w_gate[i])
        xs = ple_add(xs, p_sample[i], norm_ple[i], ple_w_proj[i], ple_w_gate[i])
    y_prompt = rmsnorm(xp, norm_final)
    y_sample = rmsnorm(xs, norm_final)
    fox_k_prompt = jnp.stack([s[0] for s in fox_p])
    fox_v_prompt = jnp.stack([s[1] for s in fox_p])
    fox_logf_prompt = jnp.stack([s[2] for s in fox_p])
    fox_k_sample = jnp.stack([s[0] for s in fox_s])
    fox_v_sample = jnp.stack([s[1] for s in fox_s])
    fox_logf_sample = jnp.stack([s[2] for s in fox_s])
    nsa_cmp_k_prompt = jnp.stack([s[0] for s in nsa_p])
    nsa_cmp_v_prompt = jnp.stack([s[1] for s in nsa_p])
    nsa_sel_k_prompt = jnp.stack([s[2] for s in nsa_p])
    nsa_sel_v_prompt = jnp.stack([s[3] for s in nsa_p])
    nsa_win_k_prompt = jnp.stack([s[4] for s in nsa_p])
    nsa_win_v_prompt = jnp.stack([s[5] for s in nsa_p])
    nsa_cmp_k_sample = jnp.stack([s[0] for s in nsa_s])
    nsa_cmp_v_sample = jnp.stack([s[1] for s in nsa_s])
    nsa_sel_k_sample = jnp.stack([s[2] for s in nsa_s])
    nsa_sel_v_sample = jnp.stack([s[3] for s in nsa_s])
    nsa_win_k_sample = jnp.stack([s[4] for s in nsa_s])
    nsa_win_v_sample = jnp.stack([s[5] for s in nsa_s])
    return (y_prompt, y_sample,
            fox_k_prompt, fox_v_prompt, fox_logf_prompt,
            fox_k_sample, fox_v_sample, fox_logf_sample,
            nsa_cmp_k_prompt, nsa_cmp_v_prompt, nsa_sel_k_prompt, nsa_sel_v_prompt,
            nsa_win_k_prompt, nsa_win_v_prompt,
            nsa_cmp_k_sample, nsa_cmp_v_sample, nsa_sel_k_sample, nsa_sel_v_sample,
            nsa_win_k_sample, nsa_win_v_sample)
```

```python
import math
import functools
import jax, jax.numpy as jnp
from jax import lax
import numpy as np
from jax.experimental import pallas as pl
from jax.experimental.pallas import tpu as pltpu

D_MODEL = 1024
BATCH = 8
SEQ = 2048
DEPTH = 2
DEC_BATCH = 128
DEC_SEQ = 8
PAST_LEN = 2048
PAGE_SIZE = 128

HEAD_DIM = 64
FOX_HEADS = D_MODEL // HEAD_DIM
NSA_HEADS = D_MODEL // HEAD_DIM
NSA_KV_HEADS = 4
NSA_GROUP = NSA_HEADS // NSA_KV_HEADS
CMP_LEN = 32
CMP_STRIDE = 16
SEL_BLOCK = 64
SEL_TOPK = 16
WINDOW = 512
N_BUCKETS = 32
MAX_DISTANCE = 128
PEER_HEADS = 8
PEER_TOPK = 16
N_KEYS = 128
N_EXPERTS = N_KEYS * N_KEYS
PEER_DKEY = 256
PLE_DIM = 256
FOX_Q_BLOCK = 128
NSA_Q_BLOCK = 64
PEER_CHUNK = 256
FORGET_BIAS = 3.0
FORCE_SCORE = 1e4
RMS_EPS = 1e-6
NEG_INF = -1e30
TINY = 1e-30
N_FOX_LAYERS = (DEPTH + 1) // 2
N_NSA_LAYERS = DEPTH // 2
FOX_IN = 3 * FOX_HEADS * HEAD_DIM + FOX_HEADS
NSA_IN = NSA_HEADS * HEAD_DIM + 6 * NSA_KV_HEADS * HEAD_DIM + 3 * NSA_HEADS


def _rmsnorm_body(x_ref, g_ref, o_ref):
    x = x_ref[...]
    ms = jnp.mean(x * x, axis=-1, keepdims=True)
    o_ref[...] = x * lax.rsqrt(ms + RMS_EPS) * g_ref[...]


def rmsnorm_pallas(x, g, tile=512):
    shp = x.shape
    xt = x.reshape(-1, shp[-1])
    n, d = xt.shape
    out = pl.pallas_call(
        _rmsnorm_body,
        grid=(n // tile,),
        in_specs=[pl.BlockSpec((tile, d), lambda i: (i, 0)),
                  pl.BlockSpec((1, d), lambda i: (0, 0))],
        out_specs=pl.BlockSpec((tile, d), lambda i: (i, 0)),
        out_shape=jax.ShapeDtypeStruct((n, d), x.dtype),
        name="rmsnorm",
    )(xt, g.reshape(1, d))
    return out.reshape(shp)


def rmsnorm(x, g):
    xf = x.astype(jnp.float32)
    y = xf * lax.rsqrt(jnp.mean(xf * xf, axis=-1, keepdims=True) + RMS_EPS)
    return (y * g.astype(jnp.float32)).astype(x.dtype)


def masked_softmax(s, mask):
    s = jnp.where(mask, s, NEG_INF)
    m = jnp.max(s, axis=-1, keepdims=True)
    e = jnp.where(mask, jnp.exp(s - m), 0.0)
    return e / jnp.maximum(jnp.sum(e, axis=-1, keepdims=True), TINY)


def t5_bucket(dist):
    n = jnp.maximum(dist, 0)
    max_exact = N_BUCKETS // 2
    nf = jnp.maximum(n, max_exact).astype(jnp.float32)
    large = max_exact + (jnp.log(nf / max_exact) / math.log(MAX_DISTANCE / max_exact)
                         * (N_BUCKETS - max_exact)).astype(jnp.int32)
    large = jnp.minimum(large, N_BUCKETS - 1)
    return jnp.where(n < max_exact, n, large)


def fox_project(h, w_in, b_f):
    B, T, _ = h.shape
    n = FOX_HEADS * HEAD_DIM
    proj = h @ w_in
    q = proj[..., :n].reshape(B, T, FOX_HEADS, HEAD_DIM)
    k = proj[..., n:2 * n].reshape(B, T, FOX_HEADS, HEAD_DIM)
    v = proj[..., 2 * n:3 * n].reshape(B, T, FOX_HEADS, HEAD_DIM)
    logf = jax.nn.log_sigmoid((proj[..., 3 * n:] + b_f).astype(jnp.float32))
    return q, k, v, logf


def fox_attend(q, k, v, c_q, c_k, q_pos, k_pos):
    s = jnp.einsum('bqhd,bkhd->bhqk', q, k).astype(jnp.float32) * (HEAD_DIM ** -0.5)
    s = s + (jnp.swapaxes(c_q, 1, 2)[..., :, None] - jnp.swapaxes(c_k, 1, 2)[..., None, :])
    mask = k_pos[None, :] <= q_pos[:, None]
    p = masked_softmax(s, mask)
    return jnp.einsum('bhqk,bkhd->bqhd', p.astype(v.dtype), v)


def fox_prompt(h, w_in, b_f, w_out):
    B, S, _ = h.shape
    q, k, v, logf = fox_project(h, w_in, b_f)
    c = jnp.cumsum(logf, axis=1)
    k_pos = jnp.arange(S)

    def block(i):
        qs = i * FOX_Q_BLOCK
        qb = lax.dynamic_slice_in_dim(q, qs, FOX_Q_BLOCK, axis=1)
        cb = lax.dynamic_slice_in_dim(c, qs, FOX_Q_BLOCK, axis=1)
        return fox_attend(qb, k, v, cb, c, qs + jnp.arange(FOX_Q_BLOCK), k_pos)

    o = lax.map(block, jnp.arange(S // FOX_Q_BLOCK))
    o = jnp.swapaxes(o, 0, 1).reshape(B, S, FOX_HEADS * HEAD_DIM)
    return o @ w_out, k, v, logf


def fox_sample(h, w_in, b_f, w_out, k_pool, v_pool, f_pool, page_table):
    DB, T, _ = h.shape
    q, k, v, logf = fox_project(h, w_in, b_f)
    past = page_table.shape[1] * PAGE_SIZE
    k_all = jnp.concatenate([k_pool[page_table].reshape(DB, past, FOX_HEADS, HEAD_DIM).astype(k.dtype), k], axis=1)
    v_all = jnp.concatenate([v_pool[page_table].reshape(DB, past, FOX_HEADS, HEAD_DIM).astype(v.dtype), v], axis=1)
    f_all = jnp.concatenate([f_pool[page_table].reshape(DB, past, FOX_HEADS).astype(jnp.float32), logf], axis=1)
    c = jnp.cumsum(f_all, axis=1)
    o = fox_attend(q, k_all, v_all, c[:, past:], c, past + jnp.arange(T), jnp.arange(past + T))
    return o.reshape(DB, T, FOX_HEADS * HEAD_DIM) @ w_out, k, v, logf


def nsa_project(h, w_in):
    B, T, _ = h.shape
    nq = NSA_HEADS * HEAD_DIM
    nkv = NSA_KV_HEADS * HEAD_DIM
    proj = h @ w_in
    q = proj[..., :nq].reshape(B, T, NSA_KV_HEADS, NSA_GROUP, HEAD_DIM)
    kv = proj[..., nq:nq + 6 * nkv].reshape(B, T, 6, NSA_KV_HEADS, HEAD_DIM)
    gates = jax.nn.sigmoid(proj[..., nq + 6 * nkv:].astype(jnp.float32)).astype(h.dtype)
    gates = gates.reshape(B, T, NSA_KV_HEADS, NSA_GROUP, 3)
    return q, gates, kv[:, :, 0], kv[:, :, 1], kv[:, :, 2], kv[:, :, 3], kv[:, :, 4], kv[:, :, 5]


def compress(x, pe, w1, w2):
    B, T, G, dh = x.shape
    r = CMP_LEN // CMP_STRIDE
    n_chunks = T // CMP_STRIDE
    nc = n_chunks - r + 1
    ch = x.reshape(B, n_chunks, CMP_STRIDE, G, dh)
    blk = jnp.concatenate([ch[:, m:m + nc] for m in range(r)], axis=2)
    blk = blk + pe[:, None, :]
    flat = blk.transpose(0, 1, 3, 2, 4).reshape(B, nc, G, CMP_LEN * dh)
    return jax.nn.gelu(flat @ w1, approximate=False) @ w2


def to_blocks(x):
    B, T, G, dh = x.shape
    return x.reshape(B, T // SEL_BLOCK, SEL_BLOCK, G, dh).transpose(0, 3, 1, 2, 4)


def nsa_attend(q, gates, q_pos, kcmp, vcmp, cmp_end, ksb, vsb, kw, vw, kw_pos, rel_bias):
    B, Q, G, R, dh = q.shape
    NC = kcmp.shape[1]
    NS = ksb.shape[2]
    scale = HEAD_DIM ** -0.5
    tbl = rel_bias.reshape(N_BUCKETS, G, R)
    g_idx = jnp.arange(G)
    dist_c = q_pos[:, None] - cmp_end[None, :]
    s = jnp.einsum('bqgrd,bcgd->bgrqc', q, kcmp).astype(jnp.float32) * scale
    s = s + tbl[t5_bucket(dist_c)].transpose(2, 3, 0, 1)[None].astype(jnp.float32)
    p_cmp = masked_softmax(s, (dist_c >= 0)[None, None, None])
    o_cmp = jnp.einsum('bgrqc,bcgd->bqgrd', p_cmp.astype(vcmp.dtype), vcmp)
    c_start = jnp.arange(NC) * CMP_STRIDE
    b_start = jnp.arange(NS) * SEL_BLOCK
    overlap = ((c_start[:, None] < b_start[None, :] + SEL_BLOCK)
               & (c_start[:, None] + CMP_LEN > b_start[None, :])).astype(jnp.float32)
    imp = jnp.einsum('bgrqc,cn->bgqn', p_cmp, overlap)
    cur = q_pos // SEL_BLOCK
    blk = jnp.arange(NS)[None, :]
    valid = blk <= cur[:, None]
    forced = (blk == 0) | (blk == cur[:, None]) | (blk == cur[:, None] - 1)
    score = jnp.where(valid, jnp.where(forced, FORCE_SCORE, imp), -1.0)
    n_sel = min(SEL_TOPK, NS)
    _, idx = lax.top_k(score, n_sel)
    b_idx = jnp.arange(B)[:, None, None, None]
    gi = g_idx[None, :, None, None]
    ksel = ksb[b_idx, gi, idx].reshape(B, G, Q, n_sel * SEL_BLOCK, dh)
    vsel = vsb[b_idx, gi, idx].reshape(B, G, Q, n_sel * SEL_BLOCK, dh)
    kpos = (idx[..., None] * SEL_BLOCK + jnp.arange(SEL_BLOCK)).reshape(B, G, Q, n_sel * SEL_BLOCK)
    dist_s = q_pos[None, None, :, None] - kpos
    s = jnp.einsum('bqgrd,bgqkd->bgrqk', q, ksel).astype(jnp.float32) * scale
    s = s + jnp.moveaxis(tbl[t5_bucket(dist_s), gi], -1, 2).astype(jnp.float32)
    p = masked_softmax(s, (dist_s >= 0)[:, :, None])
    o_sel = jnp.einsum('bgrqk,bgqkd->bqgrd', p.astype(vsel.dtype), vsel)
    dist_w = q_pos[:, None] - kw_pos[None, :]
    mask_w = (dist_w >= 0) & (dist_w < WINDOW) & (kw_pos >= 0)[None, :]
    s = jnp.einsum('bqgrd,bkgd->bgrqk', q, kw).astype(jnp.float32) * scale
    s = s + tbl[t5_bucket(dist_w)].transpose(2, 3, 0, 1)[None].astype(jnp.float32)
    p = masked_softmax(s, mask_w[None, None, None])
    o_win = jnp.einsum('bgrqk,bkgd->bqgrd', p.astype(vw.dtype), vw)
    o = gates[..., 0:1] * o_cmp + gates[..., 1:2] * o_sel + gates[..., 2:3] * o_win
    return o.reshape(B, Q, G * R * dh)


def nsa_prompt(h, w_in, w_out, pe_k, w1_k, w2_k, pe_v, w1_v, w2_v, rel_bias):
    B, S, _ = h.shape
    q, gates, kc, vc, ks, vs, kw, vw = nsa_project(h, w_in)
    kcmp = compress(kc, pe_k, w1_k, w2_k)
    vcmp = compress(vc, pe_v, w1_v, w2_v)
    cmp_end = jnp.arange(kcmp.shape[1]) * CMP_STRIDE + (CMP_LEN - 1)
    ksb = to_blocks(ks)
    vsb = to_blocks(vs)
    kw_pad = jnp.pad(kw, ((0, 0), (WINDOW, 0), (0, 0), (0, 0)))
    vw_pad = jnp.pad(vw, ((0, 0), (WINDOW, 0), (0, 0), (0, 0)))

    def block(i):
        qs = i * NSA_Q_BLOCK
        qb = lax.dynamic_slice_in_dim(q, qs, NSA_Q_BLOCK, axis=1)
        gb = lax.dynamic_slice_in_dim(gates, qs, NSA_Q_BLOCK, axis=1)
        kwb = lax.dynamic_slice_in_dim(kw_pad, qs, WINDOW + NSA_Q_BLOCK, axis=1)
        vwb = lax.dynamic_slice_in_dim(vw_pad, qs, WINDOW + NSA_Q_BLOCK, axis=1)
        kw_pos = qs - WINDOW + jnp.arange(WINDOW + NSA_Q_BLOCK)
        return nsa_attend(qb, gb, qs + jnp.arange(NSA_Q_BLOCK), kcmp, vcmp, cmp_end,
                          ksb, vsb, kwb, vwb, kw_pos, rel_bias)

    o = lax.map(block, jnp.arange(S // NSA_Q_BLOCK))
    o = jnp.swapaxes(o, 0, 1).reshape(B, S, NSA_HEADS * HEAD_DIM)
    wb = min(WINDOW, S)
    return o @ w_out, kc, vc, ks, vs, kw[:, S - wb:], vw[:, S - wb:]


def nsa_sample(h, w_in, w_out, pe_k, w1_k, w2_k, pe_v, w1_v, w2_v, rel_bias,
               ck_pool, cv_pool, sk_pool, sv_pool, wk_buf, wv_buf, page_table):
    DB, T, _ = h.shape
    q, gates, kc, vc, ks, vs, kw, vw = nsa_project(h, w_in)
    past = page_table.shape[1] * PAGE_SIZE
    L = past + T
    Lp = -(-L // SEL_BLOCK) * SEL_BLOCK

    def full(pool, new):
        old = pool[page_table].reshape(DB, past, NSA_KV_HEADS, HEAD_DIM).astype(new.dtype)
        return jnp.pad(jnp.concatenate([old, new], axis=1), ((0, 0), (0, Lp - L), (0, 0), (0, 0)))

    kcmp = compress(full(ck_pool, kc), pe_k, w1_k, w2_k)
    vcmp = compress(full(cv_pool, vc), pe_v, w1_v, w2_v)
    cmp_end = jnp.arange(kcmp.shape[1]) * CMP_STRIDE + (CMP_LEN - 1)
    ksb = to_blocks(full(sk_pool, ks))
    vsb = to_blocks(full(sv_pool, vs))
    wb = wk_buf.shape[1]
    kw_all = jnp.concatenate([wk_buf.astype(kw.dtype), kw], axis=1)
    vw_all = jnp.concatenate([wv_buf.astype(vw.dtype), vw], axis=1)
    kw_pos = past - wb + jnp.arange(wb + T)
    o = nsa_attend(q, gates, past + jnp.arange(T), kcmp, vcmp, cmp_end, ksb, vsb,
                   kw_all, vw_all, kw_pos, rel_bias)
    return o @ w_out, kc, vc, ks, vs, kw_all[:, T:], vw_all[:, T:]


def peer(x, wq, sub_k1, sub_k2, u, v):
    shp = x.shape
    xt = x.reshape(-1, shp[-1])
    n = xt.shape[0]
    n_chunks = -(-n // PEER_CHUNK)
    xt = jnp.pad(xt, ((0, n_chunks * PEER_CHUNK - n), (0, 0)))
    xc = xt.reshape(n_chunks, PEER_CHUNK, shp[-1])
    half = PEER_DKEY // 2

    def chunk(xb):
        qv = (xb @ wq).reshape(PEER_CHUNK, PEER_HEADS, PEER_DKEY)
        s1 = jnp.einsum('chd,hkd->chk', qv[..., :half], sub_k1).astype(jnp.float32)
        s2 = jnp.einsum('chd,hkd->chk', qv[..., half:], sub_k2).astype(jnp.float32)
        v1, i1 = lax.top_k(s1, PEER_TOPK)
        v2, i2 = lax.top_k(s2, PEER_TOPK)
        cand = (v1[..., :, None] + v2[..., None, :]).reshape(PEER_CHUNK, PEER_HEADS, PEER_TOPK * PEER_TOPK)
        sv, si = lax.top_k(cand, PEER_TOPK)
        e = (jnp.take_along_axis(i1, si // PEER_TOPK, axis=-1) * N_KEYS
             + jnp.take_along_axis(i2, si % PEER_TOPK, axis=-1))
        w = jax.nn.softmax(sv, axis=-1)
        act = jax.nn.gelu(jnp.einsum('chkd,cd->chk', u[e], xb).astype(jnp.float32), approximate=False)
        return jnp.einsum('chk,chkd->cd', (w * act).astype(xb.dtype), v[e])

    y = lax.map(chunk, xc).reshape(n_chunks * PEER_CHUNK, shp[-1])[:n]
    return y.reshape(shp)


def ple_add(x, p, g, w_proj, w_gate):
    gate = jax.nn.sigmoid((rmsnorm(x, g) @ w_gate).astype(jnp.float32)).astype(x.dtype)
    return x + gate * (p @ w_proj)


def kernel(x_prompt, x_sample, p_prompt, p_sample, cache_fox_k, cache_fox_v, cache_fox_logf,
           cache_nsa_cmp_k, cache_nsa_cmp_v, cache_nsa_sel_k, cache_nsa_sel_v,
           cache_nsa_win_k, cache_nsa_win_v, page_table,
           norm_mix, norm_ffn, norm_ple, norm_final,
           fox_w_in, fox_b_f, fox_w_out, nsa_w_in, nsa_w_out,
           cmp_pe_k, cmp_w1_k, cmp_w2_k, cmp_pe_v, cmp_w1_v, cmp_w2_v, rel_bias,
           peer_wq, peer_sub_k1, peer_sub_k2, peer_u, peer_v, ple_w_proj, ple_w_gate):
    xp, xs = x_prompt, x_sample
    fox_p, fox_s, nsa_p, nsa_s = [], [], [], []
    for i in range(DEPTH):
        hp = rmsnorm(xp, norm_mix[i])
        hs = rmsnorm(xs, norm_mix[i])
        j = i // 2
        if i % 2 == 0:
            yp, *stp = fox_prompt(hp, fox_w_in[j], fox_b_f[j], fox_w_out[j])
            ys, *sts = fox_sample(hs, fox_w_in[j], fox_b_f[j], fox_w_out[j],
                                  cache_fox_k[j], cache_fox_v[j], cache_fox_logf[j], page_table)
            fox_p.append(stp)
            fox_s.append(sts)
        else:
            cw = (cmp_pe_k[j], cmp_w1_k[j], cmp_w2_k[j], cmp_pe_v[j], cmp_w1_v[j], cmp_w2_v[j])
            yp, *stp = nsa_prompt(hp, nsa_w_in[j], nsa_w_out[j], *cw, rel_bias)
            ys, *sts = nsa_sample(hs, nsa_w_in[j], nsa_w_out[j], *cw, rel_bias,
                                  cache_nsa_cmp_k[j], cache_nsa_cmp_v[j], cache_nsa_sel_k[j],
                                  cache_nsa_sel_v[j], cache_nsa_win_k[j], cache_nsa_win_v[j], page_table)
            nsa_p.append(stp)
            nsa_s.append(sts)
        xp = xp + yp
        xs = xs + ys
        xp = xp + peer(rmsnorm(xp, norm_ffn[i]), peer_wq[i], peer_sub_k1[i], peer_sub_k2[i], peer_u[i], peer_v[i])
        xs = xs + peer(rmsnorm(xs, norm_ffn[i]), peer_wq[i], peer_sub_k1[i], peer_sub_k2[i], peer_u[i], peer_v[i])
        xp = ple_add(xp, p_prompt[i], norm_ple[i], ple_w_proj[i], ple_w_gate[i])
        xs = ple_add(xs, p_sample[i], norm_ple[i], ple_w_proj[i], ple_w_gate[i])
    y_prompt = rmsnorm_pallas(xp, norm_final)
    y_sample = rmsnorm_pallas(xs, norm_final)
    st = lambda lst, k: jnp.stack([s[k] for s in lst])
    return (y_prompt, y_sample,
            st(fox_p, 0), st(fox_p, 1), st(fox_p, 2),
            st(fox_s, 0), st(fox_s, 1), st(fox_s, 2),
            st(nsa_p, 0), st(nsa_p, 1), st(nsa_p, 2), st(nsa_p, 3), st(nsa_p, 4), st(nsa_p, 5),
            st(nsa_s, 0), st(nsa_s, 1), st(nsa_s, 2), st(nsa_s, 3), st(nsa_s, 4), st(nsa_s, 5))
```

```python
import functools
import math

import jax
import jax.numpy as jnp
from jax import lax
from jax.experimental import pallas as pl
from jax.experimental.pallas import tpu as pltpu

D_MODEL = 1024
DEPTH = 2
PAGE_SIZE = 128
HEAD_DIM = 64
FOX_HEADS = D_MODEL // HEAD_DIM
NSA_HEADS = D_MODEL // HEAD_DIM
NSA_KV_HEADS = 4
NSA_GROUP = NSA_HEADS // NSA_KV_HEADS
CMP_LEN = 32
CMP_STRIDE = 16
SEL_BLOCK = 64
SEL_TOPK = 16
WINDOW = 512
N_BUCKETS = 32
MAX_DISTANCE = 128
PEER_HEADS = 8
PEER_TOPK = 16
N_KEYS = 128
N_EXPERTS = N_KEYS * N_KEYS
PEER_DKEY = 256
FOX_Q_BLOCK = 128
FORCE_SCORE = 1e4
RMS_EPS = 1e-6
NEG_INF = -1e30
TINY = 1e-30

VMEM_LIMIT_BYTES = 56 * 1024 * 1024
LANES = 128
ROW_TILE = 512
MAX_COL_TILE = 1024


def _round_up(x, m):
    return -(-x // m) * m


def _rms(x, g):
    return x * lax.rsqrt(jnp.mean(x * x, axis=-1, keepdims=True) + RMS_EPS) * g


def _rmsnorm_body(x_ref, g_ref, o_ref):
    o_ref[...] = _rms(x_ref[...], g_ref[...]).astype(o_ref.dtype)


def rmsnorm_pallas(x, g, out_dtype):
    n, d = x.shape
    return pl.pallas_call(
        _rmsnorm_body,
        grid=(n // ROW_TILE,),
        in_specs=[pl.BlockSpec((ROW_TILE, d), lambda i: (i, 0)),
                  pl.BlockSpec((1, d), lambda i: (0, 0))],
        out_specs=pl.BlockSpec((ROW_TILE, d), lambda i: (i, 0)),
        out_shape=jax.ShapeDtypeStruct((n, d), out_dtype),
        compiler_params=pltpu.CompilerParams(dimension_semantics=("parallel",)),
        name="rmsnorm",
    )(x, g.reshape(1, d))


def _linear_body(x_ref, w_ref, o_ref):
    o_ref[...] = jnp.dot(x_ref[...].astype(jnp.bfloat16), w_ref[...], preferred_element_type=jnp.float32)


def linear_pallas(x, w):
    n, k = x.shape
    m = w.shape[1]
    mp = _round_up(m, LANES)
    tn = max(t for t in range(LANES, MAX_COL_TILE + 1, LANES) if mp % t == 0)
    wb = jnp.pad(w.astype(jnp.bfloat16), ((0, 0), (0, mp - m)))
    out = pl.pallas_call(
        _linear_body,
        grid=(n // ROW_TILE, mp // tn),
        in_specs=[pl.BlockSpec((ROW_TILE, k), lambda i, j: (i, 0)),
                  pl.BlockSpec((k, tn), lambda i, j: (0, j))],
        out_specs=pl.BlockSpec((ROW_TILE, tn), lambda i, j: (i, j)),
        out_shape=jax.ShapeDtypeStruct((n, mp), jnp.float32),
        compiler_params=pltpu.CompilerParams(dimension_semantics=("parallel", "parallel"),
                                             vmem_limit_bytes=VMEM_LIMIT_BYTES),
        name="linear",
    )(x, wb)
    return out[:, :m]


def _ple_body(x_ref, p_ref, g_ref, wg_ref, wp_ref, o_ref):
    x = x_ref[...]
    h = _rms(x, g_ref[...]).astype(jnp.bfloat16)
    gate = 1.0 / (1.0 + jnp.exp(-jnp.dot(h, wg_ref[...], preferred_element_type=jnp.float32)))
    proj = jnp.dot(p_ref[...].astype(jnp.bfloat16), wp_ref[...], preferred_element_type=jnp.float32)
    o_ref[...] = x + gate * proj


def ple_pallas(x, p, g, w_proj, w_gate):
    n, d = x.shape
    dp = p.shape[1]
    return pl.pallas_call(
        _ple_body,
        grid=(n // ROW_TILE,),
        in_specs=[pl.BlockSpec((ROW_TILE, d), lambda i: (i, 0)),
                  pl.BlockSpec((ROW_TILE, dp), lambda i: (i, 0)),
                  pl.BlockSpec((1, d), lambda i: (0, 0)),
                  pl.BlockSpec((d, d), lambda i: (0, 0)),
                  pl.BlockSpec((dp, d), lambda i: (0, 0))],
        out_specs=pl.BlockSpec((ROW_TILE, d), lambda i: (i, 0)),
        out_shape=jax.ShapeDtypeStruct((n, d), jnp.float32),
        compiler_params=pltpu.CompilerParams(dimension_semantics=("parallel",),
                                             vmem_limit_bytes=VMEM_LIMIT_BYTES),
        name="ple",
    )(x, p, g.reshape(1, d), w_gate.astype(jnp.bfloat16), w_proj.astype(jnp.bfloat16))


PEER_TOKEN_TILE = 512
PEER_EXPERT_TILE = 1024
SQRT_HALF = 0.7071067811865476


def _gelu_exact(x):
    return 0.5 * x * (1.0 + lax.erf(x * SQRT_HALF))


def _top_rows(x, k):
    vals = []
    for _ in range(k):
        m = jnp.max(x, axis=0, keepdims=True)
        vals.append(m)
        x = jnp.where(x == m, NEG_INF, x)
    return vals


def _peer_route_body(h_ref, wqT_ref, k1_ref, k2_ref, s1_ref, c1_ref, s2_ref, e2_ref, tau_ref):
    h = h_ref[...]
    half = PEER_DKEY // 2
    nt = (((1,), (1,)), ((), ()))
    taus = []
    for hd in range(PEER_HEADS):
        qv = lax.dot_general(wqT_ref[hd * PEER_DKEY:(hd + 1) * PEER_DKEY, :], h, nt,
                             preferred_element_type=jnp.float32)
        s1 = jnp.dot(k1_ref[hd], qv[:half].astype(jnp.bfloat16), preferred_element_type=jnp.float32)
        s2 = jnp.dot(k2_ref[hd], qv[half:].astype(jnp.bfloat16), preferred_element_type=jnp.float32)
        v1 = _top_rows(s1, PEER_TOPK)
        v2 = _top_rows(s2, PEER_TOPK)
        v2_stack = jnp.concatenate(v2, axis=0)
        blocks = []
        for p in range(PEER_TOPK):
            n_p = PEER_TOPK // (p + 1)
            rows = -(-n_p // 8) * 8
            blk = v1[p] + v2_stack[:rows]
            if n_p < rows:
                r = lax.broadcasted_iota(jnp.int32, blk.shape, 0)
                blk = jnp.where(r < n_p, blk, NEG_INF)
            blocks.append(blk)
        c = _top_rows(jnp.concatenate(blocks, axis=0), PEER_TOPK)
        z = jnp.ones_like(c[0])
        for kk in range(1, PEER_TOPK):
            z = z + jnp.exp(c[kk] - c[0])
        taus.append(c[PEER_TOPK - 1])
        s1_ref[hd] = s1
        c1_ref[hd] = jnp.exp(s1 - v1[0]) / z
        s2_ref[hd] = s2
        e2_ref[hd] = jnp.exp(s2 - v2[0])
    tau_ref[...] = jnp.concatenate(taus, axis=0)


def _peer_dense_body(h_ref, u_ref, v_ref, s1_ref, c1_ref, s2_ref, e2_ref, tau_ref, y_ref):
    j = pl.program_id(1)

    @pl.when(j == 0)
    def _():
        y_ref[...] = jnp.zeros_like(y_ref)

    h = h_ref[...]
    act = lax.dot_general(u_ref[...], h, (((1,), (1,)), ((), ())),
                          preferred_element_type=jnp.float32)
    groups = PEER_EXPERT_TILE // N_KEYS
    parts = []
    for aa in range(groups):
        w = None
        for hd in range(PEER_HEADS):
            pair_sum = s2_ref[hd] + s1_ref[hd, aa:aa + 1, :]
            sel = jnp.where(pair_sum >= tau_ref[hd:hd + 1, :], e2_ref[hd], 0.0)
            term = sel * c1_ref[hd, aa:aa + 1, :]
            w = term if w is None else w + term
        g = _gelu_exact(act[aa * N_KEYS:(aa + 1) * N_KEYS])
        parts.append((w * g).astype(jnp.bfloat16))
    p = jnp.concatenate(parts, axis=0)
    y_ref[...] += lax.dot_general(p, v_ref[...], (((0,), (0,)), ((), ())),
                                  preferred_element_type=jnp.float32)


def peer_pallas(h, wqT, k1, k2, u, v):
    n, d = h.shape
    t = PEER_TOKEN_TILE
    nt = n // t
    hk = (PEER_HEADS, N_KEYS)
    route_shape = jax.ShapeDtypeStruct(hk + (n,), jnp.float32)
    route_spec = pl.BlockSpec(hk + (t,), lambda i: (0, 0, i))
    s1, c1, s2, e2, tau = pl.pallas_call(
        _peer_route_body,
        grid=(nt,),
        in_specs=[pl.BlockSpec((t, d), lambda i: (i, 0)),
                  pl.BlockSpec(wqT.shape, lambda i: (0, 0)),
                  pl.BlockSpec(k1.shape, lambda i: (0, 0, 0)),
                  pl.BlockSpec(k2.shape, lambda i: (0, 0, 0))],
        out_specs=[route_spec, route_spec, route_spec, route_spec,
                   pl.BlockSpec((PEER_HEADS, t), lambda i: (0, i))],
        out_shape=[route_shape, route_shape, route_shape, route_shape,
                   jax.ShapeDtypeStruct((PEER_HEADS, n), jnp.float32)],
        compiler_params=pltpu.CompilerParams(dimension_semantics=("parallel",),
                                             vmem_limit_bytes=VMEM_LIMIT_BYTES),
        name="peer_route",
    )(h, wqT, k1, k2)

    e = PEER_EXPERT_TILE
    groups = e // N_KEYS
    row_spec = pl.BlockSpec((PEER_HEADS, groups, t), lambda i, j: (0, j, i))
    full_spec = pl.BlockSpec(hk + (t,), lambda i, j: (0, 0, i))
    return pl.pallas_call(
        _peer_dense_body,
        grid=(nt, N_EXPERTS // e),
        in_specs=[pl.BlockSpec((t, d), lambda i, j: (i, 0)),
                  pl.BlockSpec((e, d), lambda i, j: (j, 0)),
                  pl.BlockSpec((e, d), lambda i, j: (j, 0)),
                  row_spec, row_spec, full_spec, full_spec,
                  pl.BlockSpec((PEER_HEADS, t), lambda i, j: (0, i))],
        out_specs=pl.BlockSpec((t, d), lambda i, j: (i, 0)),
        out_shape=jax.ShapeDtypeStruct((n, d), jnp.float32),
        compiler_params=pltpu.CompilerParams(dimension_semantics=("parallel", "arbitrary"),
                                             vmem_limit_bytes=VMEM_LIMIT_BYTES),
        name="peer_dense",
    )(h, u, v, s1, c1, s2, e2, tau)


ATT_TILE = 128
MASKED_BELOW = -0.5e30


def t5_bucket(dist):
    n = jnp.maximum(dist, 0)
    max_exact = N_BUCKETS // 2
    nf = jnp.maximum(n, max_exact).astype(jnp.float32)
    large = max_exact + (jnp.log(nf / max_exact) / math.log(MAX_DISTANCE / max_exact)
                         * (N_BUCKETS - max_exact)).astype(jnp.int32)
    large = jnp.minimum(large, N_BUCKETS - 1)
    return jnp.where(n < max_exact, n, large)


def _softmax_cols(s):
    m = jnp.max(s, axis=0, keepdims=True)
    e = jnp.where(s > MASKED_BELOW, jnp.exp(s - m), 0.0)
    l = jnp.sum(e, axis=0, keepdims=True)
    return e / jnp.maximum(l, TINY)


def _dot_f32_by_01(mat01, x):
    hi = x.astype(jnp.bfloat16)
    r1 = x - hi.astype(jnp.float32)
    mid = r1.astype(jnp.bfloat16)
    lo = (r1 - mid.astype(jnp.float32)).astype(jnp.bfloat16)
    out = jnp.dot(mat01, hi, preferred_element_type=jnp.float32)
    out = out + jnp.dot(mat01, mid, preferred_element_type=jnp.float32)
    return out + jnp.dot(mat01, lo, preferred_element_type=jnp.float32)


def _x_dot_01(x, mat01):
    hi = x.astype(jnp.bfloat16)
    r1 = x - hi.astype(jnp.float32)
    mid = r1.astype(jnp.bfloat16)
    lo = (r1 - mid.astype(jnp.float32)).astype(jnp.bfloat16)
    out = jnp.dot(hi, mat01, preferred_element_type=jnp.float32)
    out = out + jnp.dot(mid, mat01, preferred_element_type=jnp.float32)
    return out + jnp.dot(lo, mat01, preferred_element_type=jnp.float32)


def _online_tiles(k_ref, v_ref, qT, lo, hi, bias_of, mask_of):
    width = qT.shape[1]
    tn = (((0,), (0,)), ((), ()))

    def step(j, carry):
        m, l, acc = carry
        rows = pl.ds(pl.multiple_of(j * ATT_TILE, ATT_TILE), ATT_TILE)
        s = jnp.dot(k_ref[rows, :], qT, preferred_element_type=jnp.float32) + bias_of(j)
        s = mask_of(j, s)
        m_new = jnp.maximum(m, jnp.max(s, axis=0, keepdims=True))
        alpha = jnp.exp(m - m_new)
        e = jnp.where(s > MASKED_BELOW, jnp.exp(s - m_new), 0.0)
        l = alpha * l + jnp.sum(e, axis=0, keepdims=True)
        pv = lax.dot_general(v_ref[rows, :], e.astype(jnp.bfloat16), tn,
                             preferred_element_type=jnp.float32)
        return m_new, l, alpha * acc + pv

    init = (jnp.full((1, width), NEG_INF, jnp.float32), jnp.zeros((1, width), jnp.float32),
            jnp.zeros((HEAD_DIM, width), jnp.float32))
    m, l, acc = lax.fori_loop(lo, hi, step, init)
    return acc / jnp.maximum(l, TINY)


def _nsa_core(qT, gate_logits, q_pos, kc_ref, vc_ref, ks_ref, vs_ref, kw_ref, vw_ref, cmp_bias, mask_ref,
              sel_range, sel_bias, win_range, win_bias, *, tq, n_sel):
    width = qT.shape[1]
    tn = (((0,), (0,)), ((), ()))
    p_cmp = _softmax_cols(jnp.dot(kc_ref[...], qT, preferred_element_type=jnp.float32) + cmp_bias)
    o_cmp = lax.dot_general(vc_ref[...], p_cmp.astype(jnp.bfloat16), tn, preferred_element_type=jnp.float32)
    ncp = p_cmp.shape[0]
    n_keys = mask_ref.shape[0]
    n_blocks = n_keys // SEL_BLOCK
    nb = lax.broadcasted_iota(jnp.int32, (n_blocks, ncp), 0) * SEL_BLOCK
    cs = lax.broadcasted_iota(jnp.int32, (n_blocks, ncp), 1) * CMP_STRIDE
    overlap = jnp.where((cs < nb + SEL_BLOCK) & (cs + CMP_LEN > nb), 1.0, 0.0).astype(jnp.bfloat16)
    la = lax.broadcasted_iota(jnp.int32, (width, width), 0)
    lb = lax.broadcasted_iota(jnp.int32, (width, width), 1)
    same_query = jnp.where((la % tq) == (lb % tq), 1.0, 0.0).astype(jnp.bfloat16)
    imp = _x_dot_01(_dot_f32_by_01(overlap, p_cmp), same_query)
    blk = lax.broadcasted_iota(jnp.int32, (n_blocks, width), 0)
    cur = q_pos // SEL_BLOCK
    forced = (blk == 0) | (blk == cur) | (blk == cur - 1)
    score = jnp.where(blk <= cur, jnp.where(forced, FORCE_SCORE, imp), -1.0)
    rank = jnp.zeros((n_blocks, width), jnp.float32)
    for mrow in range(n_blocks):
        row = score[mrow:mrow + 1, :]
        ahead = (row > score) | ((row == score) & (blk > mrow))
        rank = rank + jnp.where(ahead, 1.0, 0.0)
    sel = jnp.where(rank < n_sel, 1.0, 0.0).astype(jnp.bfloat16)
    kb = lax.broadcasted_iota(jnp.int32, (n_keys, n_blocks), 0) // SEL_BLOCK
    nn = lax.broadcasted_iota(jnp.int32, (n_keys, n_blocks), 1)
    expand = jnp.where(kb == nn, 1.0, 0.0).astype(jnp.bfloat16)
    mask_ref[...] = jnp.dot(expand, sel, preferred_element_type=jnp.float32)

    def sel_mask(j, s):
        mk = mask_ref[pl.ds(pl.multiple_of(j * ATT_TILE, ATT_TILE), ATT_TILE), :]
        return jnp.where(mk > 0.5, s, NEG_INF)

    o_sel = _online_tiles(ks_ref, vs_ref, qT, sel_range[0], sel_range[1], sel_bias, sel_mask)
    o_win = _online_tiles(kw_ref, vw_ref, qT, win_range[0], win_range[1], win_bias, lambda j, s: s)
    g = 1.0 / (1.0 + jnp.exp(-gate_logits))
    return g[0:1] * o_cmp + g[1:2] * o_sel + g[2:3] * o_win


def _nsa_prompt_body(qT_ref, gate_ref, kc_ref, vc_ref, ks_ref, vs_ref, kw_ref, vw_ref, cb_ref, tz_ref,
                     o_ref, mask_ref, *, n_sel):
    i = pl.program_id(2)
    tq = ATT_TILE
    width = qT_ref.shape[1]
    q_pos = i * tq + lax.broadcasted_iota(jnp.int32, (1, width), 1) % tq
    n_win_tiles = WINDOW // ATT_TILE
    o_ref[...] = _nsa_core(
        qT_ref[...], gate_ref[...], q_pos, kc_ref, vc_ref, ks_ref, vs_ref, kw_ref, vw_ref, cb_ref[...], mask_ref,
        (0, i + 1), lambda j: tz_ref[jnp.minimum(i - j, 2)],
        (jnp.maximum(i - n_win_tiles, 0), i + 1), lambda j: tz_ref[i - j],
        tq=tq, n_sel=n_sel)


def _nsa_sample_body(qT_ref, gate_ref, kc_ref, vc_ref, ks_ref, vs_ref, kw_ref, vw_ref, cb_ref, sb_ref, wb_ref,
                     o_ref, mask_ref, *, n_sel, tq, past):
    width = qT_ref.shape[1]
    q_pos = past + lax.broadcasted_iota(jnp.int32, (1, width), 1) % tq
    o_ref[...] = _nsa_core(
        qT_ref[...], gate_ref[...], q_pos, kc_ref, vc_ref, ks_ref, vs_ref, kw_ref, vw_ref, cb_ref[...], mask_ref,
        (0, sb_ref.shape[0]), lambda j: sb_ref[j],
        (0, wb_ref.shape[0]), lambda j: wb_ref[j],
        tq=tq, n_sel=n_sel)


def _bias_lanes(tbl, dist, ok):
    b = jnp.where(ok[..., None, None], tbl[t5_bucket(dist)], NEG_INF)
    nd = b.ndim
    b = jnp.moveaxis(b, (nd - 2, nd - 1), (0, nd - 2))
    return b.reshape(b.shape[:-2] + (b.shape[-2] * b.shape[-1],))


def nsa_prompt_attention(q, gate_logits, kcmp, vcmp, ks, vs, kw, vw, rel_bias):
    B, S, G, R, dh = q.shape
    t = ATT_TILE
    n_qt = S // t
    bf = jnp.bfloat16
    tbl = rel_bias.reshape(N_BUCKETS, G, R).astype(jnp.float32)
    qT = (q * (HEAD_DIM ** -0.5)).astype(bf).reshape(B, n_qt, t, G, R, dh)
    qT = qT.transpose(0, 3, 1, 5, 4, 2).reshape(B, G, n_qt, dh, R * t)
    gT = gate_logits.reshape(B, n_qt, t, G, R, 3).transpose(0, 3, 1, 5, 4, 2).reshape(B, G, n_qt, 3, R * t)
    n_cmp = kcmp.shape[1]
    ncp = _round_up(n_cmp, 8)
    padc = lambda x: jnp.pad(x.astype(bf).transpose(0, 2, 1, 3), ((0, 0), (0, 0), (0, ncp - n_cmp), (0, 0)))
    tr = lambda x: x.astype(bf).transpose(0, 2, 1, 3)
    qp = jnp.arange(S).reshape(n_qt, 1, t)
    c_idx = jnp.arange(ncp).reshape(1, ncp, 1)
    dist = qp - (c_idx * CMP_STRIDE + CMP_LEN - 1)
    cb = _bias_lanes(tbl, dist, (dist >= 0) & (c_idx < n_cmp))
    n_delta = WINDOW // t + 1
    d = (jnp.arange(n_delta).reshape(-1, 1, 1) * t + jnp.arange(t).reshape(1, 1, t)
         - jnp.arange(t).reshape(1, t, 1))
    tz = _bias_lanes(tbl, d, (d >= 0) & (d < WINDOW))
    body = functools.partial(_nsa_prompt_body, n_sel=min(SEL_TOPK, S // SEL_BLOCK))
    kv_spec = pl.BlockSpec((None, None, S, dh), lambda b, g, i: (b, g, 0, 0))
    cmp_spec = pl.BlockSpec((None, None, ncp, dh), lambda b, g, i: (b, g, 0, 0))
    oT = pl.pallas_call(
        body,
        grid=(B, G, n_qt),
        in_specs=[pl.BlockSpec((None, None, None, dh, R * t), lambda b, g, i: (b, g, i, 0, 0)),
                  pl.BlockSpec((None, None, None, 3, R * t), lambda b, g, i: (b, g, i, 0, 0)),
                  cmp_spec, cmp_spec, kv_spec, kv_spec, kv_spec, kv_spec,
                  pl.BlockSpec((None, None, ncp, R * t), lambda b, g, i: (g, i, 0, 0)),
                  pl.BlockSpec((None,) + tz.shape[1:], lambda b, g, i: (g, 0, 0, 0))],
        out_specs=pl.BlockSpec((None, None, None, dh, R * t), lambda b, g, i: (b, g, i, 0, 0)),
        out_shape=jax.ShapeDtypeStruct((B, G, n_qt, dh, R * t), jnp.float32),
        scratch_shapes=[pltpu.VMEM((S, R * t), jnp.float32)],
        compiler_params=pltpu.CompilerParams(dimension_semantics=("parallel", "parallel", "arbitrary"),
                                             vmem_limit_bytes=VMEM_LIMIT_BYTES),
        name="nsa_prompt_attention",
    )(qT, gT, padc(kcmp), padc(vcmp), tr(ks), tr(vs), tr(kw), tr(vw), cb, tz)
    o = oT.reshape(B, G, n_qt, dh, R, t).transpose(0, 2, 5, 1, 4, 3)
    return o.reshape(B, S, G * R * dh)


def nsa_sample_attention(q, gate_logits, kcmp, vcmp, ks_full, vs_full, kw_all, vw_all, rel_bias, past):
    DB, T, G, R, dh = q.shape
    bf = jnp.bfloat16
    width = R * T
    tbl = rel_bias.reshape(N_BUCKETS, G, R).astype(jnp.float32)
    qT = (q * (HEAD_DIM ** -0.5)).astype(bf).transpose(0, 2, 4, 3, 1).reshape(DB, G, dh, width)
    gT = gate_logits.transpose(0, 2, 4, 3, 1).reshape(DB, G, 3, width)
    q_pos = past + jnp.arange(T)

    def keys(x, n_pad):
        x = x.astype(bf).transpose(0, 2, 1, 3)
        return jnp.pad(x, ((0, 0), (0, 0), (0, n_pad - x.shape[2]), (0, 0)))

    n_cmp = kcmp.shape[1]
    ncp = _round_up(n_cmp, 8)
    c_idx = jnp.arange(ncp).reshape(ncp, 1)
    dist = q_pos.reshape(1, T) - (c_idx * CMP_STRIDE + CMP_LEN - 1)
    cb = _bias_lanes(tbl, dist, (dist >= 0) & (c_idx < n_cmp))
    lp = ks_full.shape[1]
    n_sel_keys = _round_up(lp, ATT_TILE)
    k_idx = jnp.arange(n_sel_keys).reshape(-1, ATT_TILE, 1)
    dist = q_pos.reshape(1, 1, T) - k_idx
    sb = _bias_lanes(tbl, dist, dist >= 0)
    n_win = kw_all.shape[1]
    n_win_keys = _round_up(n_win, ATT_TILE)
    w_idx = jnp.arange(n_win_keys).reshape(-1, ATT_TILE, 1)
    w_pos = past - (n_win - T) + w_idx
    dist = q_pos.reshape(1, 1, T) - w_pos
    wb = _bias_lanes(tbl, dist, (dist >= 0) & (dist < WINDOW) & (w_pos >= 0) & (w_idx < n_win))
    body = functools.partial(_nsa_sample_body, n_sel=min(SEL_TOPK, lp // SEL_BLOCK), tq=T, past=past)
    spec4 = lambda n: pl.BlockSpec((None, None, n, dh), lambda b, g: (b, g, 0, 0))
    oT = pl.pallas_call(
        body,
        grid=(DB, G),
        in_specs=[pl.BlockSpec((None, None, dh, width), lambda b, g: (b, g, 0, 0)),
                  pl.BlockSpec((None, None, 3, width), lambda b, g: (b, g, 0, 0)),
                  spec4(ncp), spec4(ncp), spec4(n_sel_keys), spec4(n_sel_keys), spec4(n_win_keys), spec4(n_win_keys),
                  pl.BlockSpec((None, ncp, width), lambda b, g: (g, 0, 0)),
                  pl.BlockSpec((None,) + sb.shape[1:], lambda b, g: (g, 0, 0, 0)),
                  pl.BlockSpec((None,) + wb.shape[1:], lambda b, g: (g, 0, 0, 0))],
        out_specs=pl.BlockSpec((None, None, dh, width), lambda b, g: (b, g, 0, 0)),
        out_shape=jax.ShapeDtypeStruct((DB, G, dh, width), jnp.float32),
        scratch_shapes=[pltpu.VMEM((n_sel_keys, width), jnp.float32)],
        compiler_params=pltpu.CompilerParams(dimension_semantics=("parallel", "arbitrary"),
                                             vmem_limit_bytes=VMEM_LIMIT_BYTES),
        name="nsa_sample_attention",
    )(qT, gT, keys(kcmp, ncp), keys(vcmp, ncp), keys(ks_full, n_sel_keys), keys(vs_full, n_sel_keys),
      keys(kw_all, n_win_keys), keys(vw_all, n_win_keys), cb, sb, wb)
    o = oT.reshape(DB, G, dh, R, T).transpose(0, 4, 1, 3, 2)
    return o.reshape(DB, T, G * R * dh)


def masked_softmax(s, mask):
    s = jnp.where(mask, s, NEG_INF)
    m = jnp.max(s, axis=-1, keepdims=True)
    e = jnp.where(mask, jnp.exp(s - m), 0.0)
    return e / jnp.maximum(jnp.sum(e, axis=-1, keepdims=True), TINY)


def fox_project(h, w_in, b_f):
    B, T, _ = h.shape
    n = FOX_HEADS * HEAD_DIM
    proj = h @ w_in
    q = proj[..., :n].reshape(B, T, FOX_HEADS, HEAD_DIM)
    k = proj[..., n:2 * n].reshape(B, T, FOX_HEADS, HEAD_DIM)
    v = proj[..., 2 * n:3 * n].reshape(B, T, FOX_HEADS, HEAD_DIM)
    logf = jax.nn.log_sigmoid((proj[..., 3 * n:] + b_f).astype(jnp.float32))
    return q, k, v, logf


def fox_attend(q, k, v, c_q, c_k, q_pos, k_pos):
    s = jnp.einsum('bqhd,bkhd->bhqk', q, k).astype(jnp.float32) * (HEAD_DIM ** -0.5)
    s = s + (jnp.swapaxes(c_q, 1, 2)[..., :, None] - jnp.swapaxes(c_k, 1, 2)[..., None, :])
    mask = k_pos[None, :] <= q_pos[:, None]
    p = masked_softmax(s, mask)
    return jnp.einsum('bhqk,bkhd->bqhd', p.astype(v.dtype), v)


def fox_prompt(h, w_in, b_f, w_out):
    B, S, _ = h.shape
    q, k, v, logf = fox_project(h, w_in, b_f)
    c = jnp.cumsum(logf, axis=1)
    k_pos = jnp.arange(S)

    def block(i):
        qs = i * FOX_Q_BLOCK
        qb = lax.dynamic_slice_in_dim(q, qs, FOX_Q_BLOCK, axis=1)
        cb = lax.dynamic_slice_in_dim(c, qs, FOX_Q_BLOCK, axis=1)
        return fox_attend(qb, k, v, cb, c, qs + jnp.arange(FOX_Q_BLOCK), k_pos)

    o = lax.map(block, jnp.arange(S // FOX_Q_BLOCK))
    o = jnp.swapaxes(o, 0, 1).reshape(B, S, FOX_HEADS * HEAD_DIM)
    return o @ w_out, k, v, logf


def fox_sample(h, w_in, b_f, w_out, k_pool, v_pool, f_pool, page_table):
    DB, T, _ = h.shape
    q, k, v, logf = fox_project(h, w_in, b_f)
    past = page_table.shape[1] * PAGE_SIZE
    k_all = jnp.concatenate([k_pool[page_table].reshape(DB, past, FOX_HEADS, HEAD_DIM), k], axis=1)
    v_all = jnp.concatenate([v_pool[page_table].reshape(DB, past, FOX_HEADS, HEAD_DIM), v], axis=1)
    f_all = jnp.concatenate([f_pool[page_table].reshape(DB, past, FOX_HEADS), logf], axis=1)
    c = jnp.cumsum(f_all, axis=1)
    o = fox_attend(q, k_all, v_all, c[:, past:], c, past + jnp.arange(T), jnp.arange(past + T))
    return o.reshape(DB, T, FOX_HEADS * HEAD_DIM) @ w_out, k, v, logf


def compress(x, pe, w1, w2):
    B, T, G, dh = x.shape
    r = CMP_LEN // CMP_STRIDE
    n_chunks = T // CMP_STRIDE
    nc = n_chunks - r + 1
    ch = x.reshape(B, n_chunks, CMP_STRIDE, G, dh)
    blk = jnp.concatenate([ch[:, m:m + nc] for m in range(r)], axis=2)
    blk = blk + pe[:, None, :]
    flat = blk.transpose(0, 1, 3, 2, 4).reshape(B, nc, G, CMP_LEN * dh)
    return jax.nn.gelu(flat @ w1, approximate=False) @ w2


def nsa_split(proj):
    B, T, _ = proj.shape
    nq = NSA_HEADS * HEAD_DIM
    nkv = NSA_KV_HEADS * HEAD_DIM
    q = proj[..., :nq].reshape(B, T, NSA_KV_HEADS, NSA_GROUP, HEAD_DIM)
    kv = proj[..., nq:nq + 6 * nkv].reshape(B, T, 6, NSA_KV_HEADS, HEAD_DIM)
    gl = proj[..., nq + 6 * nkv:].reshape(B, T, NSA_KV_HEADS, NSA_GROUP, 3)
    return q, gl, [kv[:, :, s] for s in range(6)]


def nsa_prompt(proj, cw, rel_bias):
    S = proj.shape[1]
    pe_k, w1_k, w2_k, pe_v, w1_v, w2_v = cw
    q, gl, (kc, vc, ks, vs, kw, vw) = nsa_split(proj)
    kcmp = compress(kc, pe_k, w1_k, w2_k)
    vcmp = compress(vc, pe_v, w1_v, w2_v)
    o = nsa_prompt_attention(q, gl, kcmp, vcmp, ks, vs, kw, vw, rel_bias)
    wb = min(WINDOW, S)
    return o, [kc, vc, ks, vs, kw[:, S - wb:], vw[:, S - wb:]]


def nsa_sample(proj, cw, rel_bias, ck_pool, cv_pool, sk_pool, sv_pool, wk_buf, wv_buf, page_table):
    DB, T, _ = proj.shape
    pe_k, w1_k, w2_k, pe_v, w1_v, w2_v = cw
    q, gl, (kc, vc, ks, vs, kw, vw) = nsa_split(proj)
    past = page_table.shape[1] * PAGE_SIZE
    L = past + T
    Lp = _round_up(L, SEL_BLOCK)

    def full(pool, new):
        old = pool[page_table].reshape(DB, past, NSA_KV_HEADS, HEAD_DIM)
        return jnp.pad(jnp.concatenate([old, new], axis=1), ((0, 0), (0, Lp - L), (0, 0), (0, 0)))

    kcmp = compress(full(ck_pool, kc), pe_k, w1_k, w2_k)
    vcmp = compress(full(cv_pool, vc), pe_v, w1_v, w2_v)
    kw_all = jnp.concatenate([wk_buf, kw], axis=1)
    vw_all = jnp.concatenate([wv_buf, vw], axis=1)
    o = nsa_sample_attention(q, gl, kcmp, vcmp, full(sk_pool, ks), full(sv_pool, vs), kw_all, vw_all,
                             rel_bias, past)
    return o, [kc, vc, ks, vs, kw_all[:, T:], vw_all[:, T:]]


def kernel(x_prompt, x_sample, p_prompt, p_sample, cache_fox_k, cache_fox_v, cache_fox_logf,
           cache_nsa_cmp_k, cache_nsa_cmp_v, cache_nsa_sel_k, cache_nsa_sel_v,
           cache_nsa_win_k, cache_nsa_win_v, page_table,
           norm_mix, norm_ffn, norm_ple, norm_final,
           fox_w_in, fox_b_f, fox_w_out, nsa_w_in, nsa_w_out,
           cmp_pe_k, cmp_w1_k, cmp_w2_k, cmp_pe_v, cmp_w1_v, cmp_w2_v, rel_bias,
           peer_wq, peer_sub_k1, peer_sub_k2, peer_u, peer_v, ple_w_proj, ple_w_gate):
    B, S, d = x_prompt.shape
    DB, T, _ = x_sample.shape
    n_p, n_s = B * S, DB * T
    bf = jnp.bfloat16
    rows = lambda a, b: jnp.concatenate([a.reshape(n_p, -1), b.reshape(n_s, -1)], axis=0)
    x = rows(x_prompt, x_sample)
    fox_p, fox_s, nsa_p, nsa_s = [], [], [], []
    for i in range(DEPTH):
        j = i // 2
        if i % 2 == 0:
            h = rmsnorm_pallas(x, norm_mix[i], jnp.float32)
            yp, *stp = fox_prompt(h[:n_p].reshape(B, S, d), fox_w_in[j], fox_b_f[j], fox_w_out[j])
            ys, *sts = fox_sample(h[n_p:].reshape(DB, T, d), fox_w_in[j], fox_b_f[j], fox_w_out[j],
                                  cache_fox_k[j], cache_fox_v[j], cache_fox_logf[j], page_table)
            fox_p.append(stp)
            fox_s.append(sts)
            y = rows(yp, ys)
        else:
            cw = (cmp_pe_k[j], cmp_w1_k[j], cmp_w2_k[j], cmp_pe_v[j], cmp_w1_v[j], cmp_w2_v[j])
            h = rmsnorm_pallas(x, norm_mix[i], bf)
            proj = linear_pallas(h, nsa_w_in[j])
            op, stp = nsa_prompt(proj[:n_p].reshape(B, S, -1), cw, rel_bias)
            os_, sts = nsa_sample(proj[n_p:].reshape(DB, T, -1), cw, rel_bias,
                                  cache_nsa_cmp_k[j], cache_nsa_cmp_v[j], cache_nsa_sel_k[j],
                                  cache_nsa_sel_v[j], cache_nsa_win_k[j], cache_nsa_win_v[j], page_table)
            nsa_p.append(stp)
            nsa_s.append(sts)
            y = linear_pallas(rows(op, os_), nsa_w_out[j])
        x = x + y
        h = rmsnorm_pallas(x, norm_ffn[i], bf)
        x = x + peer_pallas(h, peer_wq[i].T.astype(bf), peer_sub_k1[i].astype(bf), peer_sub_k2[i].astype(bf),
                            peer_u[i].astype(bf), peer_v[i].astype(bf))
        x = ple_pallas(x, rows(p_prompt[i], p_sample[i]), norm_ple[i], ple_w_proj[i], ple_w_gate[i])
    y = rmsnorm_pallas(x, norm_final, jnp.float32)
    st = lambda lst, k: jnp.stack([s[k] for s in lst])
    return (y[:n_p].reshape(B, S, d), y[n_p:].reshape(DB, T, d),
            st(fox_p, 0), st(fox_p, 1), st(fox_p, 2),
            st(fox_s, 0), st(fox_s, 1), st(fox_s, 2),
            st(nsa_p, 0), st(nsa_p, 1), st(nsa_p, 2), st(nsa_p, 3), st(nsa_p, 4), st(nsa_p, 5),
            st(nsa_s, 0), st(nsa_s, 1), st(nsa_s, 2), st(nsa_s, 3), st(nsa_s, 4), st(nsa_s, 5))
```

```python
import functools
import math

import jax
import jax.numpy as jnp
from jax import lax
from jax.experimental import pallas as pl
from jax.experimental.pallas import tpu as pltpu

D_MODEL = 1024
DEPTH = 2
PAGE_SIZE = 128
HEAD_DIM = 64
FOX_HEADS = D_MODEL // HEAD_DIM
NSA_HEADS = D_MODEL // HEAD_DIM
NSA_KV_HEADS = 4
NSA_GROUP = NSA_HEADS // NSA_KV_HEADS
CMP_LEN = 32
CMP_STRIDE = 16
SEL_BLOCK = 64
SEL_TOPK = 16
WINDOW = 512
N_BUCKETS = 32
MAX_DISTANCE = 128
PEER_HEADS = 8
PEER_TOPK = 16
N_KEYS = 128
N_EXPERTS = N_KEYS * N_KEYS
PEER_DKEY = 256
FORCE_SCORE = 1e4
RMS_EPS = 1e-6
NEG_INF = -1e30
TINY = 1e-30

VMEM_LIMIT_BYTES = 56 * 1024 * 1024
LANES = 128
ROW_TILE = 512
MAX_COL_TILE = 1024


def _round_up(x, m):
    return -(-x // m) * m


def _split3(x):
    hi = x.astype(jnp.bfloat16)
    r1 = x - hi.astype(jnp.float32)
    mid = r1.astype(jnp.bfloat16)
    lo = (r1 - mid.astype(jnp.float32)).astype(jnp.bfloat16)
    return hi, mid, lo


def _split3_outside_kernel(x):
    to_bf16 = lambda a: lax.reduce_precision(a, exponent_bits=8, mantissa_bits=7)
    hi = to_bf16(x)
    r1 = x - hi
    mid = to_bf16(r1)
    lo = to_bf16(r1 - mid)
    return hi.astype(jnp.bfloat16), mid.astype(jnp.bfloat16), lo.astype(jnp.bfloat16)


def _rms(x, g):
    return x * lax.rsqrt(jnp.mean(x * x, axis=-1, keepdims=True) + RMS_EPS) * g


def _rmsnorm_body(x_ref, g_ref, o_ref):
    o_ref[...] = _rms(x_ref[...], g_ref[...]).astype(o_ref.dtype)


def rmsnorm_pallas(x, g, out_dtype):
    n, d = x.shape
    return pl.pallas_call(
        _rmsnorm_body,
        grid=(n // ROW_TILE,),
        in_specs=[pl.BlockSpec((ROW_TILE, d), lambda i: (i, 0)),
                  pl.BlockSpec((1, d), lambda i: (0, 0))],
        out_specs=pl.BlockSpec((ROW_TILE, d), lambda i: (i, 0)),
        out_shape=jax.ShapeDtypeStruct((n, d), out_dtype),
        compiler_params=pltpu.CompilerParams(dimension_semantics=("parallel",)),
        name="rmsnorm",
    )(x, g.reshape(1, d))


def _linear_body(x_ref, w_ref, o_ref):
    o_ref[...] = jnp.dot(x_ref[...].astype(jnp.bfloat16), w_ref[...], preferred_element_type=jnp.float32)


def linear_pallas(x, w):
    n, k = x.shape
    m = w.shape[1]
    mp = _round_up(m, LANES)
    tn = max(t for t in range(LANES, MAX_COL_TILE + 1, LANES) if mp % t == 0)
    wb = jnp.pad(w.astype(jnp.bfloat16), ((0, 0), (0, mp - m)))
    out = pl.pallas_call(
        _linear_body,
        grid=(n // ROW_TILE, mp // tn),
        in_specs=[pl.BlockSpec((ROW_TILE, k), lambda i, j: (i, 0)),
                  pl.BlockSpec((k, tn), lambda i, j: (0, j))],
        out_specs=pl.BlockSpec((ROW_TILE, tn), lambda i, j: (i, j)),
        out_shape=jax.ShapeDtypeStruct((n, mp), jnp.float32),
        compiler_params=pltpu.CompilerParams(dimension_semantics=("parallel", "parallel"),
                                             vmem_limit_bytes=VMEM_LIMIT_BYTES),
        name="linear",
    )(x, wb)
    return out[:, :m]


def _ple_body(x_ref, p_ref, g_ref, wg_ref, wp_ref, o_ref):
    x = x_ref[...]
    h = _rms(x, g_ref[...]).astype(jnp.bfloat16)
    gate = 1.0 / (1.0 + jnp.exp(-jnp.dot(h, wg_ref[...], preferred_element_type=jnp.float32)))
    proj = jnp.dot(p_ref[...].astype(jnp.bfloat16), wp_ref[...], preferred_element_type=jnp.float32)
    o_ref[...] = x + gate * proj


def ple_pallas(x, p, g, w_proj, w_gate):
    n, d = x.shape
    dp = p.shape[1]
    return pl.pallas_call(
        _ple_body,
        grid=(n // ROW_TILE,),
        in_specs=[pl.BlockSpec((ROW_TILE, d), lambda i: (i, 0)),
                  pl.BlockSpec((ROW_TILE, dp), lambda i: (i, 0)),
                  pl.BlockSpec((1, d), lambda i: (0, 0)),
                  pl.BlockSpec((d, d), lambda i: (0, 0)),
                  pl.BlockSpec((dp, d), lambda i: (0, 0))],
        out_specs=pl.BlockSpec((ROW_TILE, d), lambda i: (i, 0)),
        out_shape=jax.ShapeDtypeStruct((n, d), jnp.float32),
        compiler_params=pltpu.CompilerParams(dimension_semantics=("parallel",),
                                             vmem_limit_bytes=VMEM_LIMIT_BYTES),
        name="ple",
    )(x, p, g.reshape(1, d), w_gate.astype(jnp.bfloat16), w_proj.astype(jnp.bfloat16))


PEER_TOKEN_TILE = 512
PEER_EXPERT_TILE = 1024
SQRT_HALF = 0.7071067811865476


def _gelu_exact(x):
    return 0.5 * x * (1.0 + lax.erf(x * SQRT_HALF))


def _top_rows(x, k):
    vals = []
    for _ in range(k):
        m = jnp.max(x, axis=0, keepdims=True)
        vals.append(m)
        x = jnp.where(x == m, NEG_INF, x)
    return vals


def _peer_route_body(h_ref, wqT_ref, k1_ref, k2_ref, s1_ref, c1_ref, s2_ref, e2_ref, tau_ref):
    h = h_ref[...]
    half = PEER_DKEY // 2
    nt = (((1,), (1,)), ((), ()))
    taus = []
    for hd in range(PEER_HEADS):
        qv = lax.dot_general(wqT_ref[hd * PEER_DKEY:(hd + 1) * PEER_DKEY, :], h, nt,
                             preferred_element_type=jnp.float32)
        s1 = jnp.dot(k1_ref[hd], qv[:half].astype(jnp.bfloat16), preferred_element_type=jnp.float32)
        s2 = jnp.dot(k2_ref[hd], qv[half:].astype(jnp.bfloat16), preferred_element_type=jnp.float32)
        v1 = _top_rows(s1, PEER_TOPK)
        v2 = _top_rows(s2, PEER_TOPK)
        v2_stack = jnp.concatenate(v2, axis=0)
        blocks = []
        for p in range(PEER_TOPK):
            n_p = PEER_TOPK // (p + 1)
            rows = -(-n_p // 8) * 8
            blk = v1[p] + v2_stack[:rows]
            if n_p < rows:
                r = lax.broadcasted_iota(jnp.int32, blk.shape, 0)
                blk = jnp.where(r < n_p, blk, NEG_INF)
            blocks.append(blk)
        c = _top_rows(jnp.concatenate(blocks, axis=0), PEER_TOPK)
        z = jnp.ones_like(c[0])
        for kk in range(1, PEER_TOPK):
            z = z + jnp.exp(c[kk] - c[0])
        taus.append(c[PEER_TOPK - 1])
        s1_ref[hd] = s1
        c1_ref[hd] = jnp.exp(s1 - v1[0]) / z
        s2_ref[hd] = s2
        e2_ref[hd] = jnp.exp(s2 - v2[0])
    tau_ref[...] = jnp.concatenate(taus, axis=0)


def _peer_dense_body(h_ref, u_ref, v_ref, s1_ref, c1_ref, s2_ref, e2_ref, tau_ref, y_ref):
    j = pl.program_id(1)

    @pl.when(j == 0)
    def _():
        y_ref[...] = jnp.zeros_like(y_ref)

    h = h_ref[...]
    act = lax.dot_general(u_ref[...], h, (((1,), (1,)), ((), ())),
                          preferred_element_type=jnp.float32)
    groups = PEER_EXPERT_TILE // N_KEYS
    parts = []
    for aa in range(groups):
        w = None
        for hd in range(PEER_HEADS):
            pair_sum = s2_ref[hd] + s1_ref[hd, aa:aa + 1, :]
            sel = jnp.where(pair_sum >= tau_ref[hd:hd + 1, :], e2_ref[hd], 0.0)
            term = sel * c1_ref[hd, aa:aa + 1, :]
            w = term if w is None else w + term
        g = _gelu_exact(act[aa * N_KEYS:(aa + 1) * N_KEYS])
        parts.append((w * g).astype(jnp.bfloat16))
    p = jnp.concatenate(parts, axis=0)
    y_ref[...] += lax.dot_general(p, v_ref[...], (((0,), (0,)), ((), ())),
                                  preferred_element_type=jnp.float32)


def peer_pallas(h, wqT, k1, k2, u, v):
    n, d = h.shape
    t = PEER_TOKEN_TILE
    nt = n // t
    hk = (PEER_HEADS, N_KEYS)
    route_shape = jax.ShapeDtypeStruct(hk + (n,), jnp.float32)
    route_spec = pl.BlockSpec(hk + (t,), lambda i: (0, 0, i))
    s1, c1, s2, e2, tau = pl.pallas_call(
        _peer_route_body,
        grid=(nt,),
        in_specs=[pl.BlockSpec((t, d), lambda i: (i, 0)),
                  pl.BlockSpec(wqT.shape, lambda i: (0, 0)),
                  pl.BlockSpec(k1.shape, lambda i: (0, 0, 0)),
                  pl.BlockSpec(k2.shape, lambda i: (0, 0, 0))],
        out_specs=[route_spec, route_spec, route_spec, route_spec,
                   pl.BlockSpec((PEER_HEADS, t), lambda i: (0, i))],
        out_shape=[route_shape, route_shape, route_shape, route_shape,
                   jax.ShapeDtypeStruct((PEER_HEADS, n), jnp.float32)],
        compiler_params=pltpu.CompilerParams(dimension_semantics=("parallel",),
                                             vmem_limit_bytes=VMEM_LIMIT_BYTES),
        name="peer_route",
    )(h, wqT, k1, k2)

    e = PEER_EXPERT_TILE
    groups = e // N_KEYS
    row_spec = pl.BlockSpec((PEER_HEADS, groups, t), lambda i, j: (0, j, i))
    full_spec = pl.BlockSpec(hk + (t,), lambda i, j: (0, 0, i))
    return pl.pallas_call(
        _peer_dense_body,
        grid=(nt, N_EXPERTS // e),
        in_specs=[pl.BlockSpec((t, d), lambda i, j: (i, 0)),
                  pl.BlockSpec((e, d), lambda i, j: (j, 0)),
                  pl.BlockSpec((e, d), lambda i, j: (j, 0)),
                  row_spec, row_spec, full_spec, full_spec,
                  pl.BlockSpec((PEER_HEADS, t), lambda i, j: (0, i))],
        out_specs=pl.BlockSpec((t, d), lambda i, j: (i, 0)),
        out_shape=jax.ShapeDtypeStruct((n, d), jnp.float32),
        compiler_params=pltpu.CompilerParams(dimension_semantics=("parallel", "arbitrary"),
                                             vmem_limit_bytes=VMEM_LIMIT_BYTES),
        name="peer_dense",
    )(h, u, v, s1, c1, s2, e2, tau)


ATT_TILE = 128
MASKED_BELOW = -0.5e30
KEY_PAIR = 2


def t5_bucket(dist):
    n = jnp.maximum(dist, 0)
    max_exact = N_BUCKETS // 2
    nf = jnp.maximum(n, max_exact).astype(jnp.float32)
    large = max_exact + (jnp.log(nf / max_exact) / math.log(MAX_DISTANCE / max_exact)
                         * (N_BUCKETS - max_exact)).astype(jnp.int32)
    large = jnp.minimum(large, N_BUCKETS - 1)
    return jnp.where(n < max_exact, n, large)


def _softmax_cols(s):
    m = jnp.max(s, axis=0, keepdims=True)
    e = jnp.where(s > MASKED_BELOW, jnp.exp(s - m), 0.0)
    l = jnp.sum(e, axis=0, keepdims=True)
    return e / jnp.maximum(l, TINY)


def _dot_f32_by_01(mat01, x):
    return sum(jnp.dot(mat01, part, preferred_element_type=jnp.float32) for part in _split3(x))


def _x_dot_01(x, mat01):
    return sum(jnp.dot(part, mat01, preferred_element_type=jnp.float32) for part in _split3(x))


def _online_tiles(k_ref, v_ref, qT, lo, hi, bias_of, mask_of, tile):
    width = qT.shape[1]
    tn = (((0,), (0,)), ((), ()))

    def step(j, carry):
        m, l, acc = carry
        rows = pl.ds(pl.multiple_of(j * tile, tile), tile)
        s = jnp.dot(k_ref[rows, :], qT, preferred_element_type=jnp.float32) + bias_of(j)
        s = mask_of(j, s)
        m_new = jnp.maximum(m, jnp.max(s, axis=0, keepdims=True))
        alpha = jnp.exp(m - m_new)
        e = jnp.where(s > MASKED_BELOW, jnp.exp(s - m_new), 0.0)
        l = alpha * l + jnp.sum(e, axis=0, keepdims=True)
        pv = lax.dot_general(v_ref[rows, :], e.astype(jnp.bfloat16), tn,
                             preferred_element_type=jnp.float32)
        return m_new, l, alpha * acc + pv

    init = (jnp.full((1, width), NEG_INF, jnp.float32), jnp.zeros((1, width), jnp.float32),
            jnp.zeros((HEAD_DIM, width), jnp.float32))
    m, l, acc = lax.fori_loop(lo, hi, step, init)
    return acc / jnp.maximum(l, TINY)


def _nsa_core(qT, gate_logits, q_pos, kc_ref, vc_ref, ks_ref, vs_ref, kw_ref, vw_ref, cmp_bias, mask_ref,
              sel_range, sel_bias, win_range, win_bias, *, tq, n_sel, sel_tile, win_tile):
    width = qT.shape[1]
    tn = (((0,), (0,)), ((), ()))
    p_cmp = _softmax_cols(jnp.dot(kc_ref[...], qT, preferred_element_type=jnp.float32) + cmp_bias)
    o_cmp = lax.dot_general(vc_ref[...], p_cmp.astype(jnp.bfloat16), tn, preferred_element_type=jnp.float32)
    ncp = p_cmp.shape[0]
    n_keys = mask_ref.shape[0]
    n_blocks = n_keys // SEL_BLOCK
    nb = lax.broadcasted_iota(jnp.int32, (n_blocks, ncp), 0) * SEL_BLOCK
    cs = lax.broadcasted_iota(jnp.int32, (n_blocks, ncp), 1) * CMP_STRIDE
    overlap = jnp.where((cs < nb + SEL_BLOCK) & (cs + CMP_LEN > nb), 1.0, 0.0).astype(jnp.bfloat16)
    la = lax.broadcasted_iota(jnp.int32, (width, width), 0)
    lb = lax.broadcasted_iota(jnp.int32, (width, width), 1)
    same_query = jnp.where((la % tq) == (lb % tq), 1.0, 0.0).astype(jnp.bfloat16)
    imp = _x_dot_01(_dot_f32_by_01(overlap, p_cmp), same_query)
    blk = lax.broadcasted_iota(jnp.int32, (n_blocks, width), 0)
    cur = q_pos // SEL_BLOCK
    forced = (blk == 0) | (blk == cur) | (blk == cur - 1)
    score = jnp.where(blk <= cur, jnp.where(forced, FORCE_SCORE, imp), -1.0)
    rank = jnp.zeros((n_blocks, width), jnp.float32)
    for mrow in range(n_blocks):
        row = score[mrow:mrow + 1, :]
        ahead = (row > score) | ((row == score) & (blk > mrow))
        rank = rank + jnp.where(ahead, 1.0, 0.0)
    sel = jnp.where(rank < n_sel, 1.0, 0.0).astype(jnp.bfloat16)
    kb = lax.broadcasted_iota(jnp.int32, (n_keys, n_blocks), 0) // SEL_BLOCK
    nn = lax.broadcasted_iota(jnp.int32, (n_keys, n_blocks), 1)
    expand = jnp.where(kb == nn, 1.0, 0.0).astype(jnp.bfloat16)
    mask_ref[...] = jnp.dot(expand, sel, preferred_element_type=jnp.float32)

    def sel_mask(j, s):
        mk = mask_ref[pl.ds(pl.multiple_of(j * sel_tile, sel_tile), sel_tile), :]
        return jnp.where(mk > 0.5, s, NEG_INF)

    o_sel = _online_tiles(ks_ref, vs_ref, qT, sel_range[0], sel_range[1], sel_bias, sel_mask, sel_tile)
    o_win = _online_tiles(kw_ref, vw_ref, qT, win_range[0], win_range[1], win_bias, lambda j, s: s, win_tile)
    g = 1.0 / (1.0 + jnp.exp(-gate_logits))
    return g[0:1] * o_cmp + g[1:2] * o_sel + g[2:3] * o_win


def _nsa_prompt_body(qT_ref, gate_ref, kc_ref, vc_ref, ks_ref, vs_ref, kw_ref, vw_ref, cb_ref, tz_ref,
                     o_ref, mask_ref, *, n_sel):
    i = pl.program_id(2)
    tq = ATT_TILE
    width = qT_ref.shape[1]
    q_pos = i * tq + lax.broadcasted_iota(jnp.int32, (1, width), 1) % tq
    masked_tile = tz_ref.shape[0] - 1

    def pair_bias(max_delta):
        def bias(jj):
            tiles = []
            for j in (KEY_PAIR * jj + t for t in range(KEY_PAIR)):
                idx = jnp.where(j > i, masked_tile, jnp.minimum(i - j, max_delta))
                tiles.append(tz_ref[idx])
            return jnp.concatenate(tiles, axis=0)
        return bias

    first_win = jnp.maximum(i - WINDOW // ATT_TILE, 0)
    o_ref[...] = _nsa_core(
        qT_ref[...], gate_ref[...], q_pos, kc_ref, vc_ref, ks_ref, vs_ref, kw_ref, vw_ref, cb_ref[...], mask_ref,
        (0, i // KEY_PAIR + 1), pair_bias(2),
        (first_win // KEY_PAIR, i // KEY_PAIR + 1), pair_bias(masked_tile),
        tq=tq, n_sel=n_sel, sel_tile=KEY_PAIR * ATT_TILE, win_tile=KEY_PAIR * ATT_TILE)


def _nsa_sample_body(qT_ref, gate_ref, kc_ref, vc_ref, ks_ref, vs_ref, kw_ref, vw_ref, cb_ref, sb_ref, wb_ref,
                     o_ref, mask_ref, *, n_sel, tq, past):
    width = qT_ref.shape[1]
    q_pos = past + lax.broadcasted_iota(jnp.int32, (1, width), 1) % tq
    o_ref[...] = _nsa_core(
        qT_ref[...], gate_ref[...], q_pos, kc_ref, vc_ref, ks_ref, vs_ref, kw_ref, vw_ref, cb_ref[...], mask_ref,
        (0, 1), lambda j: sb_ref[...],
        (0, 1), lambda j: wb_ref[...],
        tq=tq, n_sel=n_sel, sel_tile=sb_ref.shape[0], win_tile=wb_ref.shape[0])


def _bias_lanes(tbl, dist, ok):
    b = jnp.where(ok[..., None, None], tbl[t5_bucket(dist)], NEG_INF)
    nd = b.ndim
    b = jnp.moveaxis(b, (nd - 2, nd - 1), (0, nd - 2))
    return b.reshape(b.shape[:-2] + (b.shape[-2] * b.shape[-1],))


def nsa_prompt_attention(q, gate_logits, kcmp, vcmp, ks, vs, kw, vw, rel_bias):
    B, S, G, R, dh = q.shape
    t = ATT_TILE
    n_qt = S // t
    bf = jnp.bfloat16
    tbl = rel_bias.reshape(N_BUCKETS, G, R).astype(jnp.float32)
    qT = (q * (HEAD_DIM ** -0.5)).astype(bf).reshape(B, n_qt, t, G, R, dh)
    qT = qT.transpose(0, 3, 1, 5, 4, 2).reshape(B, G, n_qt, dh, R * t)
    gT = gate_logits.reshape(B, n_qt, t, G, R, 3).transpose(0, 3, 1, 5, 4, 2).reshape(B, G, n_qt, 3, R * t)
    n_cmp = kcmp.shape[1]
    ncp = _round_up(n_cmp, 8)
    padc = lambda x: jnp.pad(x.astype(bf).transpose(0, 2, 1, 3), ((0, 0), (0, 0), (0, ncp - n_cmp), (0, 0)))
    tr = lambda x: x.astype(bf).transpose(0, 2, 1, 3)
    qp = jnp.arange(S).reshape(n_qt, 1, t)
    c_idx = jnp.arange(ncp).reshape(1, ncp, 1)
    dist = qp - (c_idx * CMP_STRIDE + CMP_LEN - 1)
    cb = _bias_lanes(tbl, dist, (dist >= 0) & (c_idx < n_cmp))
    n_delta = WINDOW // t + 2
    d = (jnp.arange(n_delta).reshape(-1, 1, 1) * t + jnp.arange(t).reshape(1, 1, t)
         - jnp.arange(t).reshape(1, t, 1))
    tz = _bias_lanes(tbl, d, (d >= 0) & (d < WINDOW))
    body = functools.partial(_nsa_prompt_body, n_sel=min(SEL_TOPK, S // SEL_BLOCK))
    kv_spec = pl.BlockSpec((None, None, S, dh), lambda b, g, i: (b, g, 0, 0))
    cmp_spec = pl.BlockSpec((None, None, ncp, dh), lambda b, g, i: (b, g, 0, 0))
    oT = pl.pallas_call(
        body,
        grid=(B, G, n_qt),
        in_specs=[pl.BlockSpec((None, None, None, dh, R * t), lambda b, g, i: (b, g, i, 0, 0)),
                  pl.BlockSpec((None, None, None, 3, R * t), lambda b, g, i: (b, g, i, 0, 0)),
                  cmp_spec, cmp_spec, kv_spec, kv_spec, kv_spec, kv_spec,
                  pl.BlockSpec((None, None, ncp, R * t), lambda b, g, i: (g, i, 0, 0)),
                  pl.BlockSpec((None,) + tz.shape[1:], lambda b, g, i: (g, 0, 0, 0))],
        out_specs=pl.BlockSpec((None, None, None, dh, R * t), lambda b, g, i: (b, g, i, 0, 0)),
        out_shape=jax.ShapeDtypeStruct((B, G, n_qt, dh, R * t), jnp.float32),
        scratch_shapes=[pltpu.VMEM((S, R * t), jnp.float32)],
        compiler_params=pltpu.CompilerParams(dimension_semantics=("parallel", "parallel", "arbitrary"),
                                             vmem_limit_bytes=VMEM_LIMIT_BYTES),
        name="nsa_prompt_attention",
    )(qT, gT, padc(kcmp), padc(vcmp), tr(ks), tr(vs), tr(kw), tr(vw), cb, tz)
    o = oT.reshape(B, G, n_qt, dh, R, t).transpose(0, 2, 5, 1, 4, 3)
    return o.reshape(B, S, G * R * dh)


def nsa_sample_attention(q, gate_logits, kcmp, vcmp, ks_full, vs_full, kw_all, vw_all, rel_bias, past):
    DB, T, G, R, dh = q.shape
    bf = jnp.bfloat16
    width = R * T
    tbl = rel_bias.reshape(N_BUCKETS, G, R).astype(jnp.float32)
    qT = (q * (HEAD_DIM ** -0.5)).astype(bf).transpose(0, 2, 4, 3, 1).reshape(DB, G, dh, width)
    gT = gate_logits.transpose(0, 2, 4, 3, 1).reshape(DB, G, 3, width)
    q_pos = past + jnp.arange(T)

    def keys(x, n_pad):
        x = x.astype(bf).transpose(0, 2, 1, 3)
        return jnp.pad(x, ((0, 0), (0, 0), (0, n_pad - x.shape[2]), (0, 0)))

    n_cmp = kcmp.shape[1]
    ncp = _round_up(n_cmp, 8)
    c_idx = jnp.arange(ncp).reshape(ncp, 1)
    dist = q_pos.reshape(1, T) - (c_idx * CMP_STRIDE + CMP_LEN - 1)
    cb = _bias_lanes(tbl, dist, (dist >= 0) & (c_idx < n_cmp))
    lp = ks_full.shape[1]
    n_sel_keys = _round_up(lp, ATT_TILE)
    k_idx = jnp.arange(n_sel_keys).reshape(-1, 1)
    dist = q_pos.reshape(1, T) - k_idx
    sb = _bias_lanes(tbl, dist, dist >= 0)
    n_win = kw_all.shape[1]
    n_win_keys = _round_up(n_win, ATT_TILE)
    w_idx = jnp.arange(n_win_keys).reshape(-1, 1)
    w_pos = past - (n_win - T) + w_idx
    dist = q_pos.reshape(1, T) - w_pos
    wb = _bias_lanes(tbl, dist, (dist >= 0) & (dist < WINDOW) & (w_pos >= 0) & (w_idx < n_win))
    body = functools.partial(_nsa_sample_body, n_sel=min(SEL_TOPK, lp // SEL_BLOCK), tq=T, past=past)
    spec4 = lambda n: pl.BlockSpec((None, None, n, dh), lambda g, b: (b, g, 0, 0))
    table = lambda x: pl.BlockSpec((None,) + x.shape[1:], lambda g, b: (g, 0, 0))
    oT = pl.pallas_call(
        body,
        grid=(G, DB),
        in_specs=[pl.BlockSpec((None, None, dh, width), lambda g, b: (b, g, 0, 0)),
                  pl.BlockSpec((None, None, 3, width), lambda g, b: (b, g, 0, 0)),
                  spec4(ncp), spec4(ncp), spec4(n_sel_keys), spec4(n_sel_keys), spec4(n_win_keys), spec4(n_win_keys),
                  table(cb), table(sb), table(wb)],
        out_specs=pl.BlockSpec((None, None, dh, width), lambda g, b: (b, g, 0, 0)),
        out_shape=jax.ShapeDtypeStruct((DB, G, dh, width), jnp.float32),
        scratch_shapes=[pltpu.VMEM((n_sel_keys, width), jnp.float32)],
        compiler_params=pltpu.CompilerParams(dimension_semantics=("parallel", "parallel"),
                                             vmem_limit_bytes=VMEM_LIMIT_BYTES),
        name="nsa_sample_attention",
    )(qT, gT, keys(kcmp, ncp), keys(vcmp, ncp), keys(ks_full, n_sel_keys), keys(vs_full, n_sel_keys),
      keys(kw_all, n_win_keys), keys(vw_all, n_win_keys), cb, sb, wb)
    o = oT.reshape(DB, G, dh, R, T).transpose(0, 4, 1, 3, 2)
    return o.reshape(DB, T, G * R * dh)


FOX_Q_TILE = 512
FOX_K_TILE = 256


def _softmax_step(s, v, carry, masked):
    m, l, acc = carry
    m_new = jnp.maximum(m, jnp.max(s, axis=0, keepdims=True))
    alpha = jnp.exp(m - m_new)
    e = jnp.exp(s - m_new)
    if masked:
        e = jnp.where(s > MASKED_BELOW, e, 0.0)
    l = alpha * l + jnp.sum(e, axis=0, keepdims=True)
    pv = lax.dot_general(v, e.astype(jnp.bfloat16), (((0,), (0,)), ((), ())),
                         preferred_element_type=jnp.float32)
    return m_new, l, alpha * acc + pv


def _fox_prompt_body(qT_ref, k_ref, v_ref, o_ref):
    i = pl.program_id(2)
    qT = qT_ref[...]
    width = qT.shape[1]

    def tile(j):
        rows = pl.ds(pl.multiple_of(j * FOX_K_TILE, FOX_K_TILE), FOX_K_TILE)
        return jnp.dot(k_ref[rows, :], qT, preferred_element_type=jnp.float32), v_ref[rows, :]

    def full_step(j, carry):
        s, v = tile(j)
        return _softmax_step(s, v, carry, masked=False)

    carry = (jnp.full((1, width), NEG_INF, jnp.float32), jnp.zeros((1, width), jnp.float32),
             jnp.zeros((HEAD_DIM, width), jnp.float32))
    ratio = FOX_Q_TILE // FOX_K_TILE
    carry = lax.fori_loop(0, i * ratio, full_step, carry)
    q_pos = i * FOX_Q_TILE + lax.broadcasted_iota(jnp.int32, (FOX_K_TILE, width), 1)
    for dj in range(ratio):
        j = i * ratio + dj
        s, v = tile(j)
        k_pos = j * FOX_K_TILE + lax.broadcasted_iota(jnp.int32, (FOX_K_TILE, width), 0)
        s = jnp.where(k_pos <= q_pos, s, NEG_INF)
        carry = _softmax_step(s, v, carry, masked=True)
    m, l, acc = carry
    o_ref[...] = acc / jnp.maximum(l, TINY)


def _augment(c):
    hi, mid, lo = _split3_outside_kernel(c)
    one = jnp.ones_like(hi)
    return jnp.stack([hi, mid, lo, one, one, one], axis=-1)


def fox_prompt_attention(q, k, v, logf):
    B, S, H, dh = q.shape
    bf = jnp.bfloat16
    c = jnp.cumsum(logf, axis=1).transpose(0, 2, 1)
    aug = _augment(c)
    pad = jnp.zeros((B, H, S, LANES - dh - 6), bf)
    kb = k.astype(bf).transpose(0, 2, 1, 3)
    k_aug = jnp.concatenate([kb, -aug[..., :3], aug[..., 3:], pad], axis=-1)
    qb = (q * (dh ** -0.5)).astype(bf).transpose(0, 2, 1, 3)
    q_aug = jnp.concatenate([qb, aug[..., 3:], aug[..., :3], pad], axis=-1)
    qT = q_aug.transpose(0, 1, 3, 2)
    vb = v.astype(bf).transpose(0, 2, 1, 3)
    oT = pl.pallas_call(
        _fox_prompt_body,
        grid=(B, H, S // FOX_Q_TILE),
        in_specs=[pl.BlockSpec((None, None, LANES, FOX_Q_TILE), lambda b, h, i: (b, h, 0, i)),
                  pl.BlockSpec((None, None, S, LANES), lambda b, h, i: (b, h, 0, 0)),
                  pl.BlockSpec((None, None, S, dh), lambda b, h, i: (b, h, 0, 0))],
        out_specs=pl.BlockSpec((None, None, dh, FOX_Q_TILE), lambda b, h, i: (b, h, 0, i)),
        out_shape=jax.ShapeDtypeStruct((B, H, dh, S), jnp.float32),
        compiler_params=pltpu.CompilerParams(dimension_semantics=("parallel", "parallel", "arbitrary"),
                                             vmem_limit_bytes=VMEM_LIMIT_BYTES),
        name="fox_prompt_attention",
    )(qT, k_aug, vb)
    return oT.transpose(0, 3, 1, 2).reshape(B, S, H * dh)


def _fox_sample_body(pt_ref, qT_ref, e_ref, kp_ref, vp_ref, cp_ref, kn_ref, vn_ref, cn_ref, o_ref,
                     m_ref, l_ref, acc_ref, *, n_q):
    p = pl.program_id(1)
    width = qT_ref.shape[1]

    @pl.when(p == 0)
    def _():
        m_ref[...] = jnp.full(m_ref.shape, NEG_INF, jnp.float32)
        l_ref[...] = jnp.zeros(l_ref.shape, jnp.float32)
        acc_ref[...] = jnp.zeros(acc_ref.shape, jnp.float32)

    qT = qT_ref[...]
    decay = e_ref[...]

    def attend(k_ref, v_ref, c_ref, causal):
        n_tok = k_ref.shape[0]
        rows = n_tok * FOX_HEADS
        k2 = k_ref[...].reshape(rows, HEAD_DIM).astype(jnp.bfloat16)
        v2 = v_ref[...].reshape(rows, HEAD_DIM).astype(jnp.bfloat16)
        s = jnp.dot(k2, qT, preferred_element_type=jnp.float32)
        s = s + jnp.dot(c_ref[...], decay, preferred_element_type=jnp.float32)
        r = lax.broadcasted_iota(jnp.int32, (rows, width), 0)
        lane = lax.broadcasted_iota(jnp.int32, (rows, width), 1)
        ok = (r % FOX_HEADS) == (lane // n_q)
        if causal:
            ok = ok & ((r // FOX_HEADS) <= (lane % n_q))
        s = jnp.where(ok, s, NEG_INF)
        m, l, acc = _softmax_step(s, v2, (m_ref[...], l_ref[...], acc_ref[...]), masked=True)
        m_ref[...] = m
        l_ref[...] = l
        acc_ref[...] = acc

    attend(kp_ref, vp_ref, cp_ref, causal=False)

    @pl.when(p == pl.num_programs(1) - 1)
    def _():
        attend(kn_ref, vn_ref, cn_ref, causal=True)
        o_ref[...] = acc_ref[...] / jnp.maximum(l_ref[...], TINY)


def fox_sample_attention(q, k, v, logf, k_pool, v_pool, f_pool, page_table):
    DB, T, H, dh = q.shape
    n_pages = page_table.shape[1]
    past = n_pages * PAGE_SIZE
    bf = jnp.bfloat16
    width = H * T
    f_all = jnp.concatenate([f_pool[page_table].reshape(DB, past, H), logf], axis=1)
    c = jnp.cumsum(f_all, axis=1)
    aug_k = _augment(c)
    aug_k = jnp.concatenate([aug_k, jnp.zeros(aug_k.shape[:-1] + (2,), bf)], axis=-1)
    cp = aug_k[:, :past].reshape(DB, n_pages, PAGE_SIZE * H, 8)
    cn = aug_k[:, past:].reshape(DB, T * H, 8)
    c_q = c[:, past:].transpose(0, 2, 1).reshape(DB, width)
    hi, mid, lo = _split3_outside_kernel(c_q)
    neg = -jnp.ones_like(hi)
    zero = jnp.zeros_like(hi)
    decay = jnp.stack([neg, neg, neg, hi, mid, lo, zero, zero], axis=1)
    qT = (q * (dh ** -0.5)).astype(bf).transpose(0, 3, 2, 1).reshape(DB, dh, width)
    body = functools.partial(_fox_sample_body, n_q=T)
    page_spec = pl.BlockSpec((None, PAGE_SIZE, H, dh), lambda b, p, pt: (pt[b, p], 0, 0, 0))
    new_spec = pl.BlockSpec((None, T, H, dh), lambda b, p, pt: (b, 0, 0, 0))
    oT = pl.pallas_call(
        body,
        grid_spec=pltpu.PrefetchScalarGridSpec(
            num_scalar_prefetch=1,
            grid=(DB, n_pages),
            in_specs=[pl.BlockSpec((None, dh, width), lambda b, p, pt: (b, 0, 0)),
                      pl.BlockSpec((None, 8, width), lambda b, p, pt: (b, 0, 0)),
                      page_spec, page_spec,
                      pl.BlockSpec((None, None, PAGE_SIZE * H, 8), lambda b, p, pt: (b, p, 0, 0)),
                      new_spec, new_spec,
                      pl.BlockSpec((None, T * H, 8), lambda b, p, pt: (b, 0, 0))],
            out_specs=pl.BlockSpec((None, dh, width), lambda b, p, pt: (b, 0, 0)),
            scratch_shapes=[pltpu.VMEM((1, width), jnp.float32), pltpu.VMEM((1, width), jnp.float32),
                            pltpu.VMEM((dh, width), jnp.float32)]),
        out_shape=jax.ShapeDtypeStruct((DB, dh, width), jnp.float32),
        compiler_params=pltpu.CompilerParams(dimension_semantics=("parallel", "arbitrary"),
                                             vmem_limit_bytes=VMEM_LIMIT_BYTES),
        name="fox_sample_attention",
    )(page_table, qT, decay, k_pool, v_pool, cp, k, v, cn)
    return oT.reshape(DB, dh, H, T).transpose(0, 3, 2, 1).reshape(DB, T, H * dh)


def compress(x, pe, w1, w2):
    B, T, G, dh = x.shape
    r = CMP_LEN // CMP_STRIDE
    n_chunks = T // CMP_STRIDE
    nc = n_chunks - r + 1
    ch = x.reshape(B, n_chunks, CMP_STRIDE, G, dh)
    blk = jnp.concatenate([ch[:, m:m + nc] for m in range(r)], axis=2)
    blk = blk + pe[:, None, :]
    flat = blk.transpose(0, 1, 3, 2, 4).reshape(B, nc, G, CMP_LEN * dh)
    return jax.nn.gelu(flat @ w1, approximate=False) @ w2


def nsa_split(proj):
    B, T, _ = proj.shape
    nq = NSA_HEADS * HEAD_DIM
    nkv = NSA_KV_HEADS * HEAD_DIM
    q = proj[..., :nq].reshape(B, T, NSA_KV_HEADS, NSA_GROUP, HEAD_DIM)
    kv = proj[..., nq:nq + 6 * nkv].reshape(B, T, 6, NSA_KV_HEADS, HEAD_DIM)
    gl = proj[..., nq + 6 * nkv:].reshape(B, T, NSA_KV_HEADS, NSA_GROUP, 3)
    return q, gl, [kv[:, :, s] for s in range(6)]


def nsa_prompt(proj, cw, rel_bias):
    S = proj.shape[1]
    pe_k, w1_k, w2_k, pe_v, w1_v, w2_v = cw
    q, gl, (kc, vc, ks, vs, kw, vw) = nsa_split(proj)
    kcmp = compress(kc, pe_k, w1_k, w2_k)
    vcmp = compress(vc, pe_v, w1_v, w2_v)
    o = nsa_prompt_attention(q, gl, kcmp, vcmp, ks, vs, kw, vw, rel_bias)
    wb = min(WINDOW, S)
    return o, [kc, vc, ks, vs, kw[:, S - wb:], vw[:, S - wb:]]


def nsa_sample(proj, cw, rel_bias, ck_pool, cv_pool, sk_pool, sv_pool, wk_buf, wv_buf, page_table):
    DB, T, _ = proj.shape
    pe_k, w1_k, w2_k, pe_v, w1_v, w2_v = cw
    q, gl, (kc, vc, ks, vs, kw, vw) = nsa_split(proj)
    past = page_table.shape[1] * PAGE_SIZE
    L = past + T
    Lp = _round_up(L, SEL_BLOCK)

    def full(pool, new):
        old = pool[page_table].reshape(DB, past, NSA_KV_HEADS, HEAD_DIM)
        return jnp.pad(jnp.concatenate([old, new], axis=1), ((0, 0), (0, Lp - L), (0, 0), (0, 0)))

    kcmp = compress(full(ck_pool, kc), pe_k, w1_k, w2_k)
    vcmp = compress(full(cv_pool, vc), pe_v, w1_v, w2_v)
    kw_all = jnp.concatenate([wk_buf, kw], axis=1)
    vw_all = jnp.concatenate([wv_buf, vw], axis=1)
    o = nsa_sample_attention(q, gl, kcmp, vcmp, full(sk_pool, ks), full(sv_pool, vs), kw_all, vw_all,
                             rel_bias, past)
    return o, [kc, vc, ks, vs, kw_all[:, T:], vw_all[:, T:]]


def kernel(x_prompt, x_sample, p_prompt, p_sample, cache_fox_k, cache_fox_v, cache_fox_logf,
           cache_nsa_cmp_k, cache_nsa_cmp_v, cache_nsa_sel_k, cache_nsa_sel_v,
           cache_nsa_win_k, cache_nsa_win_v, page_table,
           norm_mix, norm_ffn, norm_ple, norm_final,
           fox_w_in, fox_b_f, fox_w_out, nsa_w_in, nsa_w_out,
           cmp_pe_k, cmp_w1_k, cmp_w2_k, cmp_pe_v, cmp_w1_v, cmp_w2_v, rel_bias,
           peer_wq, peer_sub_k1, peer_sub_k2, peer_u, peer_v, ple_w_proj, ple_w_gate):
    B, S, d = x_prompt.shape
    DB, T, _ = x_sample.shape
    n_p, n_s = B * S, DB * T
    bf = jnp.bfloat16
    rows = lambda a, b: jnp.concatenate([a.reshape(n_p, -1), b.reshape(n_s, -1)], axis=0)
    x = rows(x_prompt, x_sample)
    fox_p, fox_s, nsa_p, nsa_s = [], [], [], []
    for i in range(DEPTH):
        j = i // 2
        if i % 2 == 0:
            h = rmsnorm_pallas(x, norm_mix[i], bf)
            proj = linear_pallas(h, fox_w_in[j])
            nh = FOX_HEADS * HEAD_DIM
            logf = jax.nn.log_sigmoid(proj[:, 3 * nh:] + fox_b_f[j])
            heads = lambda a, lead: a.reshape(lead + (FOX_HEADS, -1))
            qp, kp, vp = (heads(proj[:n_p, s * nh:(s + 1) * nh], (B, S)) for s in range(3))
            qs, ks_, vs_ = (heads(proj[n_p:, s * nh:(s + 1) * nh], (DB, T)) for s in range(3))
            fp, fs = logf[:n_p].reshape(B, S, FOX_HEADS), logf[n_p:].reshape(DB, T, FOX_HEADS)
            op = fox_prompt_attention(qp, kp, vp, fp)
            os_ = fox_sample_attention(qs, ks_, vs_, fs, cache_fox_k[j], cache_fox_v[j], cache_fox_logf[j],
                                       page_table)
            fox_p.append([kp, vp, fp])
            fox_s.append([ks_, vs_, fs])
            y = linear_pallas(rows(op, os_), fox_w_out[j])
        else:
            cw = (cmp_pe_k[j], cmp_w1_k[j], cmp_w2_k[j], cmp_pe_v[j], cmp_w1_v[j], cmp_w2_v[j])
            h = rmsnorm_pallas(x, norm_mix[i], bf)
            proj = linear_pallas(h, nsa_w_in[j])
            op, stp = nsa_prompt(proj[:n_p].reshape(B, S, -1), cw, rel_bias)
            os_, sts = nsa_sample(proj[n_p:].reshape(DB, T, -1), cw, rel_bias,
                                  cache_nsa_cmp_k[j], cache_nsa_cmp_v[j], cache_nsa_sel_k[j],
                                  cache_nsa_sel_v[j], cache_nsa_win_k[j], cache_nsa_win_v[j], page_table)
            nsa_p.append(stp)
            nsa_s.append(sts)
            y = linear_pallas(rows(op, os_), nsa_w_out[j])
        x = x + y
        h = rmsnorm_pallas(x, norm_ffn[i], bf)
        x = x + peer_pallas(h, peer_wq[i].T.astype(bf), peer_sub_k1[i].astype(bf), peer_sub_k2[i].astype(bf),
                            peer_u[i].astype(bf), peer_v[i].astype(bf))
        x = ple_pallas(x, rows(p_prompt[i], p_sample[i]), norm_ple[i], ple_w_proj[i], ple_w_gate[i])
    y = rmsnorm_pallas(x, norm_final, jnp.float32)
    st = lambda lst, k: jnp.stack([s[k] for s in lst])
    return (y[:n_p].reshape(B, S, d), y[n_p:].reshape(DB, T, d),
            st(fox_p, 0), st(fox_p, 1), st(fox_p, 2),
            st(fox_s, 0), st(fox_s, 1), st(fox_s, 2),
            st(nsa_p, 0), st(nsa_p, 1), st(nsa_p, 2), st(nsa_p, 3), st(nsa_p, 4), st(nsa_p, 5),
            st(nsa_s, 0), st(nsa_s, 1), st(nsa_s, 2), st(nsa_s, 3), st(nsa_s, 4), st(nsa_s, 5))
```

```python
import functools
import math

import jax
import jax.numpy as jnp
from jax import lax
from jax.experimental import pallas as pl
from jax.experimental.pallas import tpu as pltpu

D_MODEL = 1024
DEPTH = 2
PAGE_SIZE = 128
HEAD_DIM = 64
FOX_HEADS = D_MODEL // HEAD_DIM
NSA_HEADS = D_MODEL // HEAD_DIM
NSA_KV_HEADS = 4
NSA_GROUP = NSA_HEADS // NSA_KV_HEADS
CMP_LEN = 32
CMP_STRIDE = 16
SEL_BLOCK = 64
SEL_TOPK = 16
WINDOW = 512
N_BUCKETS = 32
MAX_DISTANCE = 128
PEER_HEADS = 8
PEER_TOPK = 16
N_KEYS = 128
N_EXPERTS = N_KEYS * N_KEYS
PEER_DKEY = 256
FORCE_SCORE = 1e4
RMS_EPS = 1e-6
NEG_INF = -1e30
TINY = 1e-30

VMEM_LIMIT_BYTES = 56 * 1024 * 1024
LANES = 128
ROW_TILE = 512
MAX_COL_TILE = 1024


def _round_up(x, m):
    return -(-x // m) * m


def _split3(x):
    hi = x.astype(jnp.bfloat16)
    r1 = x - hi.astype(jnp.float32)
    mid = r1.astype(jnp.bfloat16)
    lo = (r1 - mid.astype(jnp.float32)).astype(jnp.bfloat16)
    return hi, mid, lo


def _split3_outside_kernel(x):
    to_bf16 = lambda a: lax.reduce_precision(a, exponent_bits=8, mantissa_bits=7)
    hi = to_bf16(x)
    r1 = x - hi
    mid = to_bf16(r1)
    lo = to_bf16(r1 - mid)
    return hi.astype(jnp.bfloat16), mid.astype(jnp.bfloat16), lo.astype(jnp.bfloat16)


def _rms(x, g):
    return x * lax.rsqrt(jnp.mean(x * x, axis=-1, keepdims=True) + RMS_EPS) * g


def _rmsnorm_body(x_ref, g_ref, o_ref):
    o_ref[...] = _rms(x_ref[...], g_ref[...]).astype(o_ref.dtype)


def rmsnorm_pallas(x, g, out_dtype):
    n, d = x.shape
    return pl.pallas_call(
        _rmsnorm_body,
        grid=(n // ROW_TILE,),
        in_specs=[pl.BlockSpec((ROW_TILE, d), lambda i: (i, 0)),
                  pl.BlockSpec((1, d), lambda i: (0, 0))],
        out_specs=pl.BlockSpec((ROW_TILE, d), lambda i: (i, 0)),
        out_shape=jax.ShapeDtypeStruct((n, d), out_dtype),
        compiler_params=pltpu.CompilerParams(dimension_semantics=("parallel",)),
        name="rmsnorm",
    )(x, g.reshape(1, d))


def _linear_body(x_ref, w_ref, o_ref):
    o_ref[...] = jnp.dot(x_ref[...].astype(jnp.bfloat16), w_ref[...], preferred_element_type=jnp.float32)


def linear_pallas(x, w):
    n, k = x.shape
    m = w.shape[1]
    mp = _round_up(m, LANES)
    tn = max(t for t in range(LANES, MAX_COL_TILE + 1, LANES) if mp % t == 0)
    wb = jnp.pad(w.astype(jnp.bfloat16), ((0, 0), (0, mp - m)))
    out = pl.pallas_call(
        _linear_body,
        grid=(n // ROW_TILE, mp // tn),
        in_specs=[pl.BlockSpec((ROW_TILE, k), lambda i, j: (i, 0)),
                  pl.BlockSpec((k, tn), lambda i, j: (0, j))],
        out_specs=pl.BlockSpec((ROW_TILE, tn), lambda i, j: (i, j)),
        out_shape=jax.ShapeDtypeStruct((n, mp), jnp.float32),
        compiler_params=pltpu.CompilerParams(dimension_semantics=("parallel", "parallel"),
                                             vmem_limit_bytes=VMEM_LIMIT_BYTES),
        name="linear",
    )(x, wb)
    return out[:, :m]


def _ple_body(x_ref, p_ref, g_ref, wg_ref, wp_ref, o_ref):
    x = x_ref[...]
    h = _rms(x, g_ref[...]).astype(jnp.bfloat16)
    gate = 1.0 / (1.0 + jnp.exp(-jnp.dot(h, wg_ref[...], preferred_element_type=jnp.float32)))
    proj = jnp.dot(p_ref[...].astype(jnp.bfloat16), wp_ref[...], preferred_element_type=jnp.float32)
    o_ref[...] = x + gate * proj


def ple_pallas(x, p, g, w_proj, w_gate):
    n, d = x.shape
    dp = p.shape[1]
    return pl.pallas_call(
        _ple_body,
        grid=(n // ROW_TILE,),
        in_specs=[pl.BlockSpec((ROW_TILE, d), lambda i: (i, 0)),
                  pl.BlockSpec((ROW_TILE, dp), lambda i: (i, 0)),
                  pl.BlockSpec((1, d), lambda i: (0, 0)),
                  pl.BlockSpec((d, d), lambda i: (0, 0)),
                  pl.BlockSpec((dp, d), lambda i: (0, 0))],
        out_specs=pl.BlockSpec((ROW_TILE, d), lambda i: (i, 0)),
        out_shape=jax.ShapeDtypeStruct((n, d), jnp.float32),
        compiler_params=pltpu.CompilerParams(dimension_semantics=("parallel",),
                                             vmem_limit_bytes=VMEM_LIMIT_BYTES),
        name="ple",
    )(x, p, g.reshape(1, d), w_gate.astype(jnp.bfloat16), w_proj.astype(jnp.bfloat16))


PEER_TOKEN_TILE = 512
PEER_EXPERT_TILE = 1024
SQRT_HALF = 0.7071067811865476


def _gelu_exact(x):
    return 0.5 * x * (1.0 + lax.erf(x * SQRT_HALF))


def _top_rows(x, k):
    vals = []
    for _ in range(k):
        m = jnp.max(x, axis=0, keepdims=True)
        vals.append(m)
        x = jnp.where(x == m, NEG_INF, x)
    return vals


def _peer_route_body(h_ref, wqT_ref, k1_ref, k2_ref, s1_ref, c1_ref, s2_ref, e2_ref, tau_ref):
    h = h_ref[...]
    half = PEER_DKEY // 2
    nt = (((1,), (1,)), ((), ()))
    taus = []
    for hd in range(PEER_HEADS):
        qv = lax.dot_general(wqT_ref[hd * PEER_DKEY:(hd + 1) * PEER_DKEY, :], h, nt,
                             preferred_element_type=jnp.float32)
        s1 = jnp.dot(k1_ref[hd], qv[:half].astype(jnp.bfloat16), preferred_element_type=jnp.float32)
        s2 = jnp.dot(k2_ref[hd], qv[half:].astype(jnp.bfloat16), preferred_element_type=jnp.float32)
        v1 = _top_rows(s1, PEER_TOPK)
        v2 = _top_rows(s2, PEER_TOPK)
        v2_stack = jnp.concatenate(v2, axis=0)
        blocks = []
        for p in range(PEER_TOPK):
            n_p = PEER_TOPK // (p + 1)
            rows = -(-n_p // 8) * 8
            blk = v1[p] + v2_stack[:rows]
            if n_p < rows:
                r = lax.broadcasted_iota(jnp.int32, blk.shape, 0)
                blk = jnp.where(r < n_p, blk, NEG_INF)
            blocks.append(blk)
        c = _top_rows(jnp.concatenate(blocks, axis=0), PEER_TOPK)
        z = jnp.ones_like(c[0])
        for kk in range(1, PEER_TOPK):
            z = z + jnp.exp(c[kk] - c[0])
        taus.append(c[PEER_TOPK - 1])
        s1_ref[hd] = s1
        c1_ref[hd] = jnp.exp(s1 - v1[0]) / z
        s2_ref[hd] = s2
        e2_ref[hd] = jnp.exp(s2 - v2[0])
    tau_ref[...] = jnp.concatenate(taus, axis=0)


def _peer_dense_body(h_ref, u_ref, v_ref, s1_ref, c1_ref, s2_ref, e2_ref, tau_ref, y_ref):
    j = pl.program_id(1)

    @pl.when(j == 0)
    def _():
        y_ref[...] = jnp.zeros_like(y_ref)

    h = h_ref[...]
    act = lax.dot_general(u_ref[...], h, (((1,), (1,)), ((), ())),
                          preferred_element_type=jnp.float32)
    groups = PEER_EXPERT_TILE // N_KEYS
    parts = []
    for aa in range(groups):
        w = None
        for hd in range(PEER_HEADS):
            pair_sum = s2_ref[hd] + s1_ref[hd, aa:aa + 1, :]
            sel = jnp.where(pair_sum >= tau_ref[hd:hd + 1, :], e2_ref[hd], 0.0)
            term = sel * c1_ref[hd, aa:aa + 1, :]
            w = term if w is None else w + term
        g = _gelu_exact(act[aa * N_KEYS:(aa + 1) * N_KEYS])
        parts.append((w * g).astype(jnp.bfloat16))
    p = jnp.concatenate(parts, axis=0)
    y_ref[...] += lax.dot_general(p, v_ref[...], (((0,), (0,)), ((), ())),
                                  preferred_element_type=jnp.float32)


def peer_pallas(h, wqT, k1, k2, u, v):
    n, d = h.shape
    t = PEER_TOKEN_TILE
    nt = n // t
    hk = (PEER_HEADS, N_KEYS)
    route_shape = jax.ShapeDtypeStruct(hk + (n,), jnp.float32)
    route_spec = pl.BlockSpec(hk + (t,), lambda i: (0, 0, i))
    s1, c1, s2, e2, tau = pl.pallas_call(
        _peer_route_body,
        grid=(nt,),
        in_specs=[pl.BlockSpec((t, d), lambda i: (i, 0)),
                  pl.BlockSpec(wqT.shape, lambda i: (0, 0)),
                  pl.BlockSpec(k1.shape, lambda i: (0, 0, 0)),
                  pl.BlockSpec(k2.shape, lambda i: (0, 0, 0))],
        out_specs=[route_spec, route_spec, route_spec, route_spec,
                   pl.BlockSpec((PEER_HEADS, t), lambda i: (0, i))],
        out_shape=[route_shape, route_shape, route_shape, route_shape,
                   jax.ShapeDtypeStruct((PEER_HEADS, n), jnp.float32)],
        compiler_params=pltpu.CompilerParams(dimension_semantics=("parallel",),
                                             vmem_limit_bytes=VMEM_LIMIT_BYTES),
        name="peer_route",
    )(h, wqT, k1, k2)

    e = PEER_EXPERT_TILE
    groups = e // N_KEYS
    row_spec = pl.BlockSpec((PEER_HEADS, groups, t), lambda i, j: (0, j, i))
    full_spec = pl.BlockSpec(hk + (t,), lambda i, j: (0, 0, i))
    return pl.pallas_call(
        _peer_dense_body,
        grid=(nt, N_EXPERTS // e),
        in_specs=[pl.BlockSpec((t, d), lambda i, j: (i, 0)),
                  pl.BlockSpec((e, d), lambda i, j: (j, 0)),
                  pl.BlockSpec((e, d), lambda i, j: (j, 0)),
                  row_spec, row_spec, full_spec, full_spec,
                  pl.BlockSpec((PEER_HEADS, t), lambda i, j: (0, i))],
        out_specs=pl.BlockSpec((t, d), lambda i, j: (i, 0)),
        out_shape=jax.ShapeDtypeStruct((n, d), jnp.float32),
        compiler_params=pltpu.CompilerParams(dimension_semantics=("parallel", "arbitrary"),
                                             vmem_limit_bytes=VMEM_LIMIT_BYTES),
        name="peer_dense",
    )(h, u, v, s1, c1, s2, e2, tau)


ATT_TILE = 128
MASKED_BELOW = -0.5e30
KEY_PAIR = 2


def t5_bucket(dist):
    n = jnp.maximum(dist, 0)
    max_exact = N_BUCKETS // 2
    nf = jnp.maximum(n, max_exact).astype(jnp.float32)
    large = max_exact + (jnp.log(nf / max_exact) / math.log(MAX_DISTANCE / max_exact)
                         * (N_BUCKETS - max_exact)).astype(jnp.int32)
    large = jnp.minimum(large, N_BUCKETS - 1)
    return jnp.where(n < max_exact, n, large)


def _softmax_cols(s):
    m = jnp.max(s, axis=0, keepdims=True)
    e = jnp.where(s > MASKED_BELOW, jnp.exp(s - m), 0.0)
    l = jnp.sum(e, axis=0, keepdims=True)
    return e / jnp.maximum(l, TINY)


def _dot_f32_by_01(mat01, x):
    return sum(jnp.dot(mat01, part, preferred_element_type=jnp.float32) for part in _split3(x))


def _x_dot_01(x, mat01):
    return sum(jnp.dot(part, mat01, preferred_element_type=jnp.float32) for part in _split3(x))


def _online_tiles(k_ref, v_ref, qT, lo, hi, bias_of, mask_of, tile):
    width = qT.shape[1]
    tn = (((0,), (0,)), ((), ()))

    def step(j, carry):
        m, l, acc = carry
        rows = pl.ds(pl.multiple_of(j * tile, tile), tile)
        s = jnp.dot(k_ref[rows, :], qT, preferred_element_type=jnp.float32) + bias_of(j)
        s = mask_of(j, s)
        m_new = jnp.maximum(m, jnp.max(s, axis=0, keepdims=True))
        alpha = jnp.exp(m - m_new)
        e = jnp.where(s > MASKED_BELOW, jnp.exp(s - m_new), 0.0)
        l = alpha * l + jnp.sum(e, axis=0, keepdims=True)
        pv = lax.dot_general(v_ref[rows, :], e.astype(jnp.bfloat16), tn,
                             preferred_element_type=jnp.float32)
        return m_new, l, alpha * acc + pv

    init = (jnp.full((1, width), NEG_INF, jnp.float32), jnp.zeros((1, width), jnp.float32),
            jnp.zeros((HEAD_DIM, width), jnp.float32))
    m, l, acc = lax.fori_loop(lo, hi, step, init)
    return acc / jnp.maximum(l, TINY)


def _nsa_core(qT, gate_logits, q_pos, kc_ref, vc_ref, ks_ref, vs_ref, kw_ref, vw_ref, cmp_bias, mask_ref,
              sel_range, sel_bias, win_range, win_bias, *, tq, n_sel, sel_tile, win_tile):
    width = qT.shape[1]
    tn = (((0,), (0,)), ((), ()))
    p_cmp = _softmax_cols(jnp.dot(kc_ref[...], qT, preferred_element_type=jnp.float32) + cmp_bias)
    o_cmp = lax.dot_general(vc_ref[...], p_cmp.astype(jnp.bfloat16), tn, preferred_element_type=jnp.float32)
    ncp = p_cmp.shape[0]
    n_keys = mask_ref.shape[0]
    n_blocks = n_keys // SEL_BLOCK
    nb = lax.broadcasted_iota(jnp.int32, (n_blocks, ncp), 0) * SEL_BLOCK
    cs = lax.broadcasted_iota(jnp.int32, (n_blocks, ncp), 1) * CMP_STRIDE
    overlap = jnp.where((cs < nb + SEL_BLOCK) & (cs + CMP_LEN > nb), 1.0, 0.0).astype(jnp.bfloat16)
    la = lax.broadcasted_iota(jnp.int32, (width, width), 0)
    lb = lax.broadcasted_iota(jnp.int32, (width, width), 1)
    same_query = jnp.where((la % tq) == (lb % tq), 1.0, 0.0).astype(jnp.bfloat16)
    imp = _x_dot_01(_dot_f32_by_01(overlap, p_cmp), same_query)
    blk = lax.broadcasted_iota(jnp.int32, (n_blocks, width), 0)
    cur = q_pos // SEL_BLOCK
    forced = (blk == 0) | (blk == cur) | (blk == cur - 1)
    score = jnp.where(blk <= cur, jnp.where(forced, FORCE_SCORE, imp), -1.0)
    rank = jnp.zeros((n_blocks, width), jnp.float32)
    for mrow in range(n_blocks):
        row = score[mrow:mrow + 1, :]
        ahead = (row > score) | ((row == score) & (blk > mrow))
        rank = rank + jnp.where(ahead, 1.0, 0.0)
    sel = jnp.where(rank < n_sel, 1.0, 0.0).astype(jnp.bfloat16)
    kb = lax.broadcasted_iota(jnp.int32, (n_keys, n_blocks), 0) // SEL_BLOCK
    nn = lax.broadcasted_iota(jnp.int32, (n_keys, n_blocks), 1)
    expand = jnp.where(kb == nn, 1.0, 0.0).astype(jnp.bfloat16)
    mask_ref[...] = jnp.dot(expand, sel, preferred_element_type=jnp.float32)

    def sel_mask(j, s):
        mk = mask_ref[pl.ds(pl.multiple_of(j * sel_tile, sel_tile), sel_tile), :]
        return jnp.where(mk > 0.5, s, NEG_INF)

    o_sel = _online_tiles(ks_ref, vs_ref, qT, sel_range[0], sel_range[1], sel_bias, sel_mask, sel_tile)
    o_win = _online_tiles(kw_ref, vw_ref, qT, win_range[0], win_range[1], win_bias, lambda j, s: s, win_tile)
    g = 1.0 / (1.0 + jnp.exp(-gate_logits))
    return g[0:1] * o_cmp + g[1:2] * o_sel + g[2:3] * o_win


def _nsa_prompt_body(qT_ref, gate_ref, kc_ref, vc_ref, ks_ref, vs_ref, kw_ref, vw_ref, cb_ref, tz_ref,
                     o_ref, mask_ref, *, n_sel):
    i = pl.program_id(2)
    tq = ATT_TILE
    width = qT_ref.shape[1]
    q_pos = i * tq + lax.broadcasted_iota(jnp.int32, (1, width), 1) % tq
    masked_tile = tz_ref.shape[0] - 1

    def pair_bias(max_delta):
        def bias(jj):
            tiles = []
            for j in (KEY_PAIR * jj + t for t in range(KEY_PAIR)):
                idx = jnp.where(j > i, masked_tile, jnp.minimum(i - j, max_delta))
                tiles.append(tz_ref[idx])
            return jnp.concatenate(tiles, axis=0)
        return bias

    first_win = jnp.maximum(i - WINDOW // ATT_TILE, 0)
    o_ref[...] = _nsa_core(
        qT_ref[...], gate_ref[...], q_pos, kc_ref, vc_ref, ks_ref, vs_ref, kw_ref, vw_ref, cb_ref[...], mask_ref,
        (0, i // KEY_PAIR + 1), pair_bias(2),
        (first_win // KEY_PAIR, i // KEY_PAIR + 1), pair_bias(masked_tile),
        tq=tq, n_sel=n_sel, sel_tile=KEY_PAIR * ATT_TILE, win_tile=KEY_PAIR * ATT_TILE)


def _nsa_sample_body(qT_ref, gate_ref, kc_ref, vc_ref, ks_ref, vs_ref, kw_ref, vw_ref, cb_ref, sb_ref, wb_ref,
                     o_ref, mask_ref, *, n_sel, tq, past):
    width = qT_ref.shape[1]
    q_pos = past + lax.broadcasted_iota(jnp.int32, (1, width), 1) % tq
    o_ref[...] = _nsa_core(
        qT_ref[...], gate_ref[...], q_pos, kc_ref, vc_ref, ks_ref, vs_ref, kw_ref, vw_ref, cb_ref[...], mask_ref,
        (0, 1), lambda j: sb_ref[...],
        (0, 1), lambda j: wb_ref[...],
        tq=tq, n_sel=n_sel, sel_tile=sb_ref.shape[0], win_tile=wb_ref.shape[0])


def _bias_lanes(tbl, dist, ok):
    onehot = (t5_bucket(dist)[..., None] == jnp.arange(N_BUCKETS)).astype(jnp.float32)
    vals = jnp.dot(onehot.reshape(-1, N_BUCKETS), tbl.reshape(N_BUCKETS, -1),
                   precision=lax.Precision.HIGHEST).reshape(dist.shape + tbl.shape[1:])
    b = jnp.where(ok[..., None, None], vals, NEG_INF)
    nd = b.ndim
    b = jnp.moveaxis(b, (nd - 2, nd - 1), (0, nd - 2))
    return b.reshape(b.shape[:-2] + (b.shape[-2] * b.shape[-1],))


def nsa_prompt_attention(q, gate_logits, kcmp, vcmp, ks, vs, kw, vw, rel_bias):
    B, S, G, R, dh = q.shape
    t = ATT_TILE
    n_qt = S // t
    bf = jnp.bfloat16
    tbl = rel_bias.reshape(N_BUCKETS, G, R).astype(jnp.float32)
    qT = (q * (HEAD_DIM ** -0.5)).astype(bf).reshape(B, n_qt, t, G, R, dh)
    qT = qT.transpose(0, 3, 1, 5, 4, 2).reshape(B, G, n_qt, dh, R * t)
    gT = gate_logits.reshape(B, n_qt, t, G, R, 3).transpose(0, 3, 1, 5, 4, 2).reshape(B, G, n_qt, 3, R * t)
    n_cmp = kcmp.shape[1]
    ncp = _round_up(n_cmp, 8)
    padc = lambda x: jnp.pad(x.astype(bf).transpose(0, 2, 1, 3), ((0, 0), (0, 0), (0, ncp - n_cmp), (0, 0)))
    tr = lambda x: x.astype(bf).transpose(0, 2, 1, 3)
    qp = jnp.arange(S).reshape(n_qt, 1, t)
    c_idx = jnp.arange(ncp).reshape(1, ncp, 1)
    dist = qp - (c_idx * CMP_STRIDE + CMP_LEN - 1)
    cb = _bias_lanes(tbl, dist, (dist >= 0) & (c_idx < n_cmp))
    n_delta = WINDOW // t + 2
    d = (jnp.arange(n_delta).reshape(-1, 1, 1) * t + jnp.arange(t).reshape(1, 1, t)
         - jnp.arange(t).reshape(1, t, 1))
    tz = _bias_lanes(tbl, d, (d >= 0) & (d < WINDOW))
    body = functools.partial(_nsa_prompt_body, n_sel=min(SEL_TOPK, S // SEL_BLOCK))
    kv_spec = pl.BlockSpec((None, None, S, dh), lambda b, g, i: (b, g, 0, 0))
    cmp_spec = pl.BlockSpec((None, None, ncp, dh), lambda b, g, i: (b, g, 0, 0))
    oT = pl.pallas_call(
        body,
        grid=(B, G, n_qt),
        in_specs=[pl.BlockSpec((None, None, None, dh, R * t), lambda b, g, i: (b, g, i, 0, 0)),
                  pl.BlockSpec((None, None, None, 3, R * t), lambda b, g, i: (b, g, i, 0, 0)),
                  cmp_spec, cmp_spec, kv_spec, kv_spec, kv_spec, kv_spec,
                  pl.BlockSpec((None, None, ncp, R * t), lambda b, g, i: (g, i, 0, 0)),
                  pl.BlockSpec((None,) + tz.shape[1:], lambda b, g, i: (g, 0, 0, 0))],
        out_specs=pl.BlockSpec((None, None, None, dh, R * t), lambda b, g, i: (b, g, i, 0, 0)),
        out_shape=jax.ShapeDtypeStruct((B, G, n_qt, dh, R * t), jnp.float32),
        scratch_shapes=[pltpu.VMEM((S, R * t), jnp.float32)],
        compiler_params=pltpu.CompilerParams(dimension_semantics=("parallel", "parallel", "arbitrary"),
                                             vmem_limit_bytes=VMEM_LIMIT_BYTES),
        name="nsa_prompt_attention",
    )(qT, gT, padc(kcmp), padc(vcmp), tr(ks), tr(vs), tr(kw), tr(vw), cb, tz)
    o = oT.reshape(B, G, n_qt, dh, R, t).transpose(0, 2, 5, 1, 4, 3)
    return o.reshape(B, S, G * R * dh)


def nsa_sample_attention(q, gate_logits, kcmp, vcmp, ks_full, vs_full, kw_all, vw_all, rel_bias, past):
    DB, T, G, R, dh = q.shape
    bf = jnp.bfloat16
    width = R * T
    tbl = rel_bias.reshape(N_BUCKETS, G, R).astype(jnp.float32)
    qT = (q * (HEAD_DIM ** -0.5)).astype(bf).transpose(0, 2, 4, 3, 1).reshape(DB, G, dh, width)
    gT = gate_logits.transpose(0, 2, 4, 3, 1).reshape(DB, G, 3, width)
    q_pos = past + jnp.arange(T)

    def keys(x, n_pad):
        x = x.astype(bf).transpose(0, 2, 1, 3)
        return jnp.pad(x, ((0, 0), (0, 0), (0, n_pad - x.shape[2]), (0, 0)))

    n_cmp = kcmp.shape[1]
    ncp = _round_up(n_cmp, 8)
    c_idx = jnp.arange(ncp).reshape(ncp, 1)
    dist = q_pos.reshape(1, T) - (c_idx * CMP_STRIDE + CMP_LEN - 1)
    cb = _bias_lanes(tbl, dist, (dist >= 0) & (c_idx < n_cmp))
    lp = ks_full.shape[1]
    n_sel_keys = _round_up(lp, ATT_TILE)
    k_idx = jnp.arange(n_sel_keys).reshape(-1, 1)
    dist = q_pos.reshape(1, T) - k_idx
    sb = _bias_lanes(tbl, dist, dist >= 0)
    n_win = kw_all.shape[1]
    n_win_keys = _round_up(n_win, ATT_TILE)
    w_idx = jnp.arange(n_win_keys).reshape(-1, 1)
    w_pos = past - (n_win - T) + w_idx
    dist = q_pos.reshape(1, T) - w_pos
    wb = _bias_lanes(tbl, dist, (dist >= 0) & (dist < WINDOW) & (w_pos >= 0) & (w_idx < n_win))
    body = functools.partial(_nsa_sample_body, n_sel=min(SEL_TOPK, lp // SEL_BLOCK), tq=T, past=past)
    spec4 = lambda n: pl.BlockSpec((None, None, n, dh), lambda g, b: (b, g, 0, 0))
    table = lambda x: pl.BlockSpec((None,) + x.shape[1:], lambda g, b: (g, 0, 0))
    oT = pl.pallas_call(
        body,
        grid=(G, DB),
        in_specs=[pl.BlockSpec((None, None, dh, width), lambda g, b: (b, g, 0, 0)),
                  pl.BlockSpec((None, None, 3, width), lambda g, b: (b, g, 0, 0)),
                  spec4(ncp), spec4(ncp), spec4(n_sel_keys), spec4(n_sel_keys), spec4(n_win_keys), spec4(n_win_keys),
                  table(cb), table(sb), table(wb)],
        out_specs=pl.BlockSpec((None, None, dh, width), lambda g, b: (b, g, 0, 0)),
        out_shape=jax.ShapeDtypeStruct((DB, G, dh, width), jnp.float32),
        scratch_shapes=[pltpu.VMEM((n_sel_keys, width), jnp.float32)],
        compiler_params=pltpu.CompilerParams(dimension_semantics=("parallel", "parallel"),
                                             vmem_limit_bytes=VMEM_LIMIT_BYTES),
        name="nsa_sample_attention",
    )(qT, gT, keys(kcmp, ncp), keys(vcmp, ncp), keys(ks_full, n_sel_keys), keys(vs_full, n_sel_keys),
      keys(kw_all, n_win_keys), keys(vw_all, n_win_keys), cb, sb, wb)
    o = oT.reshape(DB, G, dh, R, T).transpose(0, 4, 1, 3, 2)
    return o.reshape(DB, T, G * R * dh)


FOX_Q_TILE = 512
FOX_K_TILE = 256


def _softmax_step(s, v, carry, masked):
    m, l, acc = carry
    m_new = jnp.maximum(m, jnp.max(s, axis=0, keepdims=True))
    alpha = jnp.exp(m - m_new)
    e = jnp.exp(s - m_new)
    if masked:
        e = jnp.where(s > MASKED_BELOW, e, 0.0)
    l = alpha * l + jnp.sum(e, axis=0, keepdims=True)
    pv = lax.dot_general(v, e.astype(jnp.bfloat16), (((0,), (0,)), ((), ())),
                         preferred_element_type=jnp.float32)
    return m_new, l, alpha * acc + pv


def _fox_prompt_body(qT_ref, k_ref, v_ref, o_ref):
    i = pl.program_id(2)
    qT = qT_ref[...]
    width = qT.shape[1]

    def tile(j):
        rows = pl.ds(pl.multiple_of(j * FOX_K_TILE, FOX_K_TILE), FOX_K_TILE)
        return jnp.dot(k_ref[rows, :], qT, preferred_element_type=jnp.float32), v_ref[rows, :]

    def full_step(j, carry):
        s, v = tile(j)
        return _softmax_step(s, v, carry, masked=False)

    carry = (jnp.full((1, width), NEG_INF, jnp.float32), jnp.zeros((1, width), jnp.float32),
             jnp.zeros((HEAD_DIM, width), jnp.float32))
    ratio = FOX_Q_TILE // FOX_K_TILE
    carry = lax.fori_loop(0, i * ratio, full_step, carry)
    q_pos = i * FOX_Q_TILE + lax.broadcasted_iota(jnp.int32, (FOX_K_TILE, width), 1)
    for dj in range(ratio):
        j = i * ratio + dj
        s, v = tile(j)
        k_pos = j * FOX_K_TILE + lax.broadcasted_iota(jnp.int32, (FOX_K_TILE, width), 0)
        s = jnp.where(k_pos <= q_pos, s, NEG_INF)
        carry = _softmax_step(s, v, carry, masked=True)
    m, l, acc = carry
    o_ref[...] = acc / jnp.maximum(l, TINY)


def _augment(c):
    hi, mid, lo = _split3_outside_kernel(c)
    one = jnp.ones_like(hi)
    return jnp.stack([hi, mid, lo, one, one, one], axis=-1)


def fox_prompt_attention(q, k, v, logf):
    B, S, H, dh = q.shape
    bf = jnp.bfloat16
    c = jnp.cumsum(logf, axis=1).transpose(0, 2, 1)
    aug = _augment(c)
    pad = jnp.zeros((B, H, S, LANES - dh - 6), bf)
    kb = k.astype(bf).transpose(0, 2, 1, 3)
    k_aug = jnp.concatenate([kb, -aug[..., :3], aug[..., 3:], pad], axis=-1)
    qb = (q * (dh ** -0.5)).astype(bf).transpose(0, 2, 1, 3)
    q_aug = jnp.concatenate([qb, aug[..., 3:], aug[..., :3], pad], axis=-1)
    qT = q_aug.transpose(0, 1, 3, 2)
    vb = v.astype(bf).transpose(0, 2, 1, 3)
    oT = pl.pallas_call(
        _fox_prompt_body,
        grid=(B, H, S // FOX_Q_TILE),
        in_specs=[pl.BlockSpec((None, None, LANES, FOX_Q_TILE), lambda b, h, i: (b, h, 0, i)),
                  pl.BlockSpec((None, None, S, LANES), lambda b, h, i: (b, h, 0, 0)),
                  pl.BlockSpec((None, None, S, dh), lambda b, h, i: (b, h, 0, 0))],
        out_specs=pl.BlockSpec((None, None, dh, FOX_Q_TILE), lambda b, h, i: (b, h, 0, i)),
        out_shape=jax.ShapeDtypeStruct((B, H, dh, S), jnp.float32),
        compiler_params=pltpu.CompilerParams(dimension_semantics=("parallel", "parallel", "arbitrary"),
                                             vmem_limit_bytes=VMEM_LIMIT_BYTES),
        name="fox_prompt_attention",
    )(qT, k_aug, vb)
    return oT.transpose(0, 3, 1, 2).reshape(B, S, H * dh)


def _fox_sample_body(pt_ref, qbd_ref, kp_ref, vp_ref, bp_ref, kn_ref, vn_ref, bn_ref, o_ref,
                     m_ref, l_ref, acc_ref, *, n_q):
    p = pl.program_id(1)
    width = qbd_ref.shape[0]
    rows = kn_ref.shape[0]

    @pl.when(p == 0)
    def _():
        m_ref[...] = jnp.full(m_ref.shape, NEG_INF, jnp.float32)
        l_ref[...] = jnp.zeros(l_ref.shape, jnp.float32)
        acc_ref[...] = jnp.zeros(acc_ref.shape, jnp.float32)

    def attend(k2, v2, bias_t):
        s_t = jnp.dot(qbd_ref[...], k2, preferred_element_type=jnp.float32).T + bias_t
        m = m_ref[...]
        m_new = jnp.maximum(m, jnp.max(s_t, axis=0, keepdims=True))
        alpha = jnp.exp(m - m_new)
        e = jnp.where(s_t > MASKED_BELOW, jnp.exp(s_t - m_new), 0.0)
        m_ref[...] = m_new
        l_ref[...] = alpha * l_ref[...] + jnp.sum(e, axis=0, keepdims=True)
        acc_ref[...] = alpha * acc_ref[...] + jnp.dot(v2, e.astype(jnp.bfloat16),
                                                      preferred_element_type=jnp.float32)

    page = lambda ref: ref[...].reshape(rows, PAGE_SIZE).astype(jnp.bfloat16)
    attend(page(kp_ref), page(vp_ref), bp_ref[...])

    @pl.when(p == pl.num_programs(1) - 1)
    def _():
        attend(kn_ref[...], vn_ref[...], bn_ref[...])
        out = acc_ref[...] / jnp.maximum(l_ref[...], TINY)
        r = lax.broadcasted_iota(jnp.int32, (rows, width), 0) // HEAD_DIM
        c = lax.broadcasted_iota(jnp.int32, (rows, width), 1) // n_q
        out = jnp.where(r == c, out, 0.0)
        la = lax.broadcasted_iota(jnp.int32, (width, width), 0) % n_q
        lb = lax.broadcasted_iota(jnp.int32, (width, width), 1)
        gather_q = jnp.where(la == lb, 1.0, 0.0).astype(jnp.bfloat16)
        o_ref[...] = _x_dot_01(out, gather_q)[:, :n_q]


def fox_sample_attention(q, k, v, logf, k_pool, v_pool, f_pool, page_table):
    DB, T, H, dh = q.shape
    n_pages = page_table.shape[1]
    past = n_pages * PAGE_SIZE
    bf = jnp.bfloat16
    width = H * T
    rows = H * dh
    f_all = jnp.concatenate([f_pool[page_table].reshape(DB, past, H), logf], axis=1)
    c = jnp.cumsum(f_all, axis=1)
    c_q = c[:, past:].transpose(0, 2, 1).reshape(DB, 1, 1, width)
    c_k = jnp.repeat(c[:, :past].reshape(DB, n_pages, PAGE_SIZE, H), T, axis=-1)
    bias_past = c_q - c_k
    tok = jnp.arange(PAGE_SIZE).reshape(1, PAGE_SIZE, 1)
    qi = (jnp.arange(width) % T).reshape(1, 1, width)
    c_new = jnp.pad(jnp.repeat(c[:, past:], T, axis=-1), ((0, 0), (0, PAGE_SIZE - T), (0, 0)))
    bias_new = jnp.where((tok <= qi) & (tok < T), c_q[:, 0] - c_new, NEG_INF)
    qh = (q * (dh ** -0.5)).transpose(0, 2, 1, 3)
    same_head = jnp.eye(H, dtype=qh.dtype).reshape(1, H, 1, H, 1)
    qbd = (qh[:, :, :, None, :] * same_head).astype(bf).reshape(DB, width, rows)
    new_t = lambda x: jnp.pad(x.astype(bf).transpose(0, 2, 3, 1),
                              ((0, 0), (0, 0), (0, 0), (0, PAGE_SIZE - T))).reshape(DB, rows, PAGE_SIZE)
    pool_t = lambda x: x.transpose(0, 2, 3, 1)
    body = functools.partial(_fox_sample_body, n_q=T)
    page_spec = pl.BlockSpec((None, H, dh, PAGE_SIZE), lambda b, p, pt: (pt[b, p], 0, 0, 0))
    per_b = lambda shape: pl.BlockSpec((None,) + shape, lambda b, p, pt: (b,) + (0,) * len(shape))
    o = pl.pallas_call(
        body,
        grid_spec=pltpu.PrefetchScalarGridSpec(
            num_scalar_prefetch=1,
            grid=(DB, n_pages),
            in_specs=[per_b((width, rows)), page_spec, page_spec,
                      pl.BlockSpec((None, None, PAGE_SIZE, width), lambda b, p, pt: (b, p, 0, 0)),
                      per_b((rows, PAGE_SIZE)), per_b((rows, PAGE_SIZE)), per_b((PAGE_SIZE, width))],
            out_specs=per_b((rows, T)),
            scratch_shapes=[pltpu.VMEM((1, width), jnp.float32), pltpu.VMEM((1, width), jnp.float32),
                            pltpu.VMEM((rows, width), jnp.float32)]),
        out_shape=jax.ShapeDtypeStruct((DB, rows, T), jnp.float32),
        compiler_params=pltpu.CompilerParams(dimension_semantics=("parallel", "arbitrary"),
                                             vmem_limit_bytes=VMEM_LIMIT_BYTES),
        name="fox_sample_attention",
    )(page_table, qbd, pool_t(k_pool), pool_t(v_pool), bias_past, new_t(k), new_t(v), bias_new)
    return o.reshape(DB, H, dh, T).transpose(0, 3, 1, 2).reshape(DB, T, H * dh)


def compress(x, pe, w1, w2):
    B, T, G, dh = x.shape
    r = CMP_LEN // CMP_STRIDE
    n_chunks = T // CMP_STRIDE
    nc = n_chunks - r + 1
    ch = x.reshape(B, n_chunks, CMP_STRIDE, G, dh)
    blk = jnp.concatenate([ch[:, m:m + nc] for m in range(r)], axis=2)
    blk = blk + pe[:, None, :]
    flat = blk.transpose(0, 1, 3, 2, 4).reshape(B, nc, G, CMP_LEN * dh)
    return jax.nn.gelu(flat @ w1, approximate=False) @ w2


def nsa_split(proj):
    B, T, _ = proj.shape
    nq = NSA_HEADS * HEAD_DIM
    nkv = NSA_KV_HEADS * HEAD_DIM
    q = proj[..., :nq].reshape(B, T, NSA_KV_HEADS, NSA_GROUP, HEAD_DIM)
    kv = proj[..., nq:nq + 6 * nkv].reshape(B, T, 6, NSA_KV_HEADS, HEAD_DIM)
    gl = proj[..., nq + 6 * nkv:].reshape(B, T, NSA_KV_HEADS, NSA_GROUP, 3)
    return q, gl, [kv[:, :, s] for s in range(6)]


def nsa_prompt(proj, cw, rel_bias):
    S = proj.shape[1]
    pe_k, w1_k, w2_k, pe_v, w1_v, w2_v = cw
    q, gl, (kc, vc, ks, vs, kw, vw) = nsa_split(proj)
    kcmp = compress(kc, pe_k, w1_k, w2_k)
    vcmp = compress(vc, pe_v, w1_v, w2_v)
    o = nsa_prompt_attention(q, gl, kcmp, vcmp, ks, vs, kw, vw, rel_bias)
    wb = min(WINDOW, S)
    return o, [kc, vc, ks, vs, kw[:, S - wb:], vw[:, S - wb:]]


def nsa_sample(proj, cw, rel_bias, ck_pool, cv_pool, sk_pool, sv_pool, wk_buf, wv_buf, page_table):
    DB, T, _ = proj.shape
    pe_k, w1_k, w2_k, pe_v, w1_v, w2_v = cw
    q, gl, (kc, vc, ks, vs, kw, vw) = nsa_split(proj)
    past = page_table.shape[1] * PAGE_SIZE
    L = past + T
    Lp = _round_up(L, SEL_BLOCK)

    def full(pool, new):
        old = pool[page_table].reshape(DB, past, NSA_KV_HEADS, HEAD_DIM)
        return jnp.pad(jnp.concatenate([old, new], axis=1), ((0, 0), (0, Lp - L), (0, 0), (0, 0)))

    kcmp = compress(full(ck_pool, kc), pe_k, w1_k, w2_k)
    vcmp = compress(full(cv_pool, vc), pe_v, w1_v, w2_v)
    kw_all = jnp.concatenate([wk_buf, kw], axis=1)
    vw_all = jnp.concatenate([wv_buf, vw], axis=1)
    o = nsa_sample_attention(q, gl, kcmp, vcmp, full(sk_pool, ks), full(sv_pool, vs), kw_all, vw_all,
                             rel_bias, past)
    return o, [kc, vc, ks, vs, kw_all[:, T:], vw_all[:, T:]]


def kernel(x_prompt, x_sample, p_prompt, p_sample, cache_fox_k, cache_fox_v, cache_fox_logf,
           cache_nsa_cmp_k, cache_nsa_cmp_v, cache_nsa_sel_k, cache_nsa_sel_v,
           cache_nsa_win_k, cache_nsa_win_v, page_table,
           norm_mix, norm_ffn, norm_ple, norm_final,
           fox_w_in, fox_b_f, fox_w_out, nsa_w_in, nsa_w_out,
           cmp_pe_k, cmp_w1_k, cmp_w2_k, cmp_pe_v, cmp_w1_v, cmp_w2_v, rel_bias,
           peer_wq, peer_sub_k1, peer_sub_k2, peer_u, peer_v, ple_w_proj, ple_w_gate):
    B, S, d = x_prompt.shape
    DB, T, _ = x_sample.shape
    n_p, n_s = B * S, DB * T
    bf = jnp.bfloat16
    rows = lambda a, b: jnp.concatenate([a.reshape(n_p, -1), b.reshape(n_s, -1)], axis=0)
    x = rows(x_prompt, x_sample)
    fox_p, fox_s, nsa_p, nsa_s = [], [], [], []
    for i in range(DEPTH):
        j = i // 2
        if i % 2 == 0:
            h = rmsnorm_pallas(x, norm_mix[i], bf)
            proj = linear_pallas(h, fox_w_in[j])
            nh = FOX_HEADS * HEAD_DIM
            logf = jax.nn.log_sigmoid(proj[:, 3 * nh:] + fox_b_f[j])
            heads = lambda a, lead: a.reshape(lead + (FOX_HEADS, -1))
            qp, kp, vp = (heads(proj[:n_p, s * nh:(s + 1) * nh], (B, S)) for s in range(3))
            qs, ks_, vs_ = (heads(proj[n_p:, s * nh:(s + 1) * nh], (DB, T)) for s in range(3))
            fp, fs = logf[:n_p].reshape(B, S, FOX_HEADS), logf[n_p:].reshape(DB, T, FOX_HEADS)
            op = fox_prompt_attention(qp, kp, vp, fp)
            os_ = fox_sample_attention(qs, ks_, vs_, fs, cache_fox_k[j], cache_fox_v[j], cache_fox_logf[j],
                                       page_table)
            fox_p.append([kp, vp, fp])
            fox_s.append([ks_, vs_, fs])
            y = linear_pallas(rows(op, os_), fox_w_out[j])
        else:
            cw = (cmp_pe_k[j], cmp_w1_k[j], cmp_w2_k[j], cmp_pe_v[j], cmp_w1_v[j], cmp_w2_v[j])
            h = rmsnorm_pallas(x, norm_mix[i], bf)
            proj = linear_pallas(h, nsa_w_in[j])
            op, stp = nsa_prompt(proj[:n_p].reshape(B, S, -1), cw, rel_bias)
            os_, sts = nsa_sample(proj[n_p:].reshape(DB, T, -1), cw, rel_bias,
                                  cache_nsa_cmp_k[j], cache_nsa_cmp_v[j], cache_nsa_sel_k[j],
                                  cache_nsa_sel_v[j], cache_nsa_win_k[j], cache_nsa_win_v[j], page_table)
            nsa_p.append(stp)
            nsa_s.append(sts)
            y = linear_pallas(rows(op, os_), nsa_w_out[j])
        x = x + y
        h = rmsnorm_pallas(x, norm_ffn[i], bf)
        x = x + peer_pallas(h, peer_wq[i].T.astype(bf), peer_sub_k1[i].astype(bf), peer_sub_k2[i].astype(bf),
                            peer_u[i].astype(bf), peer_v[i].astype(bf))
        x = ple_pallas(x, rows(p_prompt[i], p_sample[i]), norm_ple[i], ple_w_proj[i], ple_w_gate[i])
    y = rmsnorm_pallas(x, norm_final, jnp.float32)
    st = lambda lst, k: jnp.stack([s[k] for s in lst])
    return (y[:n_p].reshape(B, S, d), y[n_p:].reshape(DB, T, d),
            st(fox_p, 0), st(fox_p, 1), st(fox_p, 2),
            st(fox_s, 0), st(fox_s, 1), st(fox_s, 2),
            st(nsa_p, 0), st(nsa_p, 1), st(nsa_p, 2), st(nsa_p, 3), st(nsa_p, 4), st(nsa_p, 5),
            st(nsa_s, 0), st(nsa_s, 1), st(nsa_s, 2), st(nsa_s, 3), st(nsa_s, 4), st(nsa_s, 5))
```

```python
import functools
import math

import jax
import jax.numpy as jnp
from jax import lax
from jax.experimental import pallas as pl
from jax.experimental.pallas import tpu as pltpu

D_MODEL = 1024
DEPTH = 2
PAGE_SIZE = 128
HEAD_DIM = 64
FOX_HEADS = D_MODEL // HEAD_DIM
NSA_HEADS = D_MODEL // HEAD_DIM
NSA_KV_HEADS = 4
NSA_GROUP = NSA_HEADS // NSA_KV_HEADS
CMP_LEN = 32
CMP_STRIDE = 16
SEL_BLOCK = 64
SEL_TOPK = 16
WINDOW = 512
N_BUCKETS = 32
MAX_DISTANCE = 128
PEER_HEADS = 8
PEER_TOPK = 16
N_KEYS = 128
N_EXPERTS = N_KEYS * N_KEYS
PEER_DKEY = 256
FORCE_SCORE = 1e4
RMS_EPS = 1e-6
NEG_INF = -1e30
TINY = 1e-30

VMEM_LIMIT_BYTES = 56 * 1024 * 1024
LANES = 128
ROW_TILE = 512
MAX_COL_TILE = 1024


def _round_up(x, m):
    return -(-x // m) * m


def _split3(x):
    hi = x.astype(jnp.bfloat16)
    r1 = x - hi.astype(jnp.float32)
    mid = r1.astype(jnp.bfloat16)
    lo = (r1 - mid.astype(jnp.float32)).astype(jnp.bfloat16)
    return hi, mid, lo


def _split3_outside_kernel(x):
    to_bf16 = lambda a: lax.reduce_precision(a, exponent_bits=8, mantissa_bits=7)
    hi = to_bf16(x)
    r1 = x - hi
    mid = to_bf16(r1)
    lo = to_bf16(r1 - mid)
    return hi.astype(jnp.bfloat16), mid.astype(jnp.bfloat16), lo.astype(jnp.bfloat16)


def _rms(x, g):
    return x * lax.rsqrt(jnp.mean(x * x, axis=-1, keepdims=True) + RMS_EPS) * g


def _rmsnorm_body(x_ref, g_ref, o_ref):
    o_ref[...] = _rms(x_ref[...], g_ref[...]).astype(o_ref.dtype)


def rmsnorm_pallas(x, g, out_dtype):
    n, d = x.shape
    return pl.pallas_call(
        _rmsnorm_body,
        grid=(n // ROW_TILE,),
        in_specs=[pl.BlockSpec((ROW_TILE, d), lambda i: (i, 0)),
                  pl.BlockSpec((1, d), lambda i: (0, 0))],
        out_specs=pl.BlockSpec((ROW_TILE, d), lambda i: (i, 0)),
        out_shape=jax.ShapeDtypeStruct((n, d), out_dtype),
        compiler_params=pltpu.CompilerParams(dimension_semantics=("parallel",)),
        name="rmsnorm",
    )(x, g.reshape(1, d))


def _linear_body(x_ref, w_ref, o_ref):
    o_ref[...] = jnp.dot(x_ref[...].astype(jnp.bfloat16), w_ref[...], preferred_element_type=jnp.float32)


def linear_pallas(x, w):
    n, k = x.shape
    m = w.shape[1]
    mp = _round_up(m, LANES)
    tn = max(t for t in range(LANES, MAX_COL_TILE + 1, LANES) if mp % t == 0)
    wb = jnp.pad(w.astype(jnp.bfloat16), ((0, 0), (0, mp - m)))
    out = pl.pallas_call(
        _linear_body,
        grid=(n // ROW_TILE, mp // tn),
        in_specs=[pl.BlockSpec((ROW_TILE, k), lambda i, j: (i, 0)),
                  pl.BlockSpec((k, tn), lambda i, j: (0, j))],
        out_specs=pl.BlockSpec((ROW_TILE, tn), lambda i, j: (i, j)),
        out_shape=jax.ShapeDtypeStruct((n, mp), jnp.float32),
        compiler_params=pltpu.CompilerParams(dimension_semantics=("parallel", "parallel"),
                                             vmem_limit_bytes=VMEM_LIMIT_BYTES),
        name="linear",
    )(x, wb)
    return out[:, :m]


def _ple_body(x_ref, p_ref, g_ref, wg_ref, wp_ref, o_ref):
    x = x_ref[...]
    h = _rms(x, g_ref[...]).astype(jnp.bfloat16)
    gate = 1.0 / (1.0 + jnp.exp(-jnp.dot(h, wg_ref[...], preferred_element_type=jnp.float32)))
    proj = jnp.dot(p_ref[...].astype(jnp.bfloat16), wp_ref[...], preferred_element_type=jnp.float32)
    o_ref[...] = x + gate * proj


def ple_pallas(x, p, g, w_proj, w_gate):
    n, d = x.shape
    dp = p.shape[1]
    return pl.pallas_call(
        _ple_body,
        grid=(n // ROW_TILE,),
        in_specs=[pl.BlockSpec((ROW_TILE, d), lambda i: (i, 0)),
                  pl.BlockSpec((ROW_TILE, dp), lambda i: (i, 0)),
                  pl.BlockSpec((1, d), lambda i: (0, 0)),
                  pl.BlockSpec((d, d), lambda i: (0, 0)),
                  pl.BlockSpec((dp, d), lambda i: (0, 0))],
        out_specs=pl.BlockSpec((ROW_TILE, d), lambda i: (i, 0)),
        out_shape=jax.ShapeDtypeStruct((n, d), jnp.float32),
        compiler_params=pltpu.CompilerParams(dimension_semantics=("parallel",),
                                             vmem_limit_bytes=VMEM_LIMIT_BYTES),
        name="ple",
    )(x, p, g.reshape(1, d), w_gate.astype(jnp.bfloat16), w_proj.astype(jnp.bfloat16))


PEER_TOKEN_TILE = 512
PEER_EXPERT_TILE = 1024
SQRT_HALF = 0.7071067811865476


def _gelu_exact(x):
    return 0.5 * x * (1.0 + lax.erf(x * SQRT_HALF))


def _top_rows(x, k):
    vals = []
    for _ in range(k):
        m = jnp.max(x, axis=0, keepdims=True)
        vals.append(m)
        x = jnp.where(x == m, NEG_INF, x)
    return vals


def _peer_route_body(h_ref, wqT_ref, k1_ref, k2_ref, n1_ref, c1_ref, r2_ref, e2_ref):
    h = h_ref[...]
    half = PEER_DKEY // 2
    nt = (((1,), (1,)), ((), ()))
    for hd in range(PEER_HEADS):
        qv = lax.dot_general(wqT_ref[hd * PEER_DKEY:(hd + 1) * PEER_DKEY, :], h, nt,
                             preferred_element_type=jnp.float32)
        s1 = jnp.dot(k1_ref[hd], qv[:half].astype(jnp.bfloat16), preferred_element_type=jnp.float32)
        s2 = jnp.dot(k2_ref[hd], qv[half:].astype(jnp.bfloat16), preferred_element_type=jnp.float32)
        v1 = _top_rows(s1, PEER_TOPK)
        v2 = _top_rows(s2, PEER_TOPK)
        v2_stack = jnp.concatenate(v2, axis=0)
        blocks = []
        for p in range(PEER_TOPK):
            n_p = PEER_TOPK // (p + 1)
            rows = -(-n_p // 8) * 8
            blk = v1[p] + v2_stack[:rows]
            if n_p < rows:
                r = lax.broadcasted_iota(jnp.int32, blk.shape, 0)
                blk = jnp.where(r < n_p, blk, NEG_INF)
            blocks.append(blk)
        c = _top_rows(jnp.concatenate(blocks, axis=0), PEER_TOPK)
        z = jnp.ones_like(c[0])
        for kk in range(1, PEER_TOPK):
            z = z + jnp.exp(c[kk] - c[0])
        tau = c[PEER_TOPK - 1]
        n1 = jnp.zeros_like(s1)
        r2 = jnp.zeros_like(s2)
        for q in range(PEER_TOPK):
            n1 = n1 + jnp.where(s1 + v2[q] >= tau, 1.0, 0.0)
            r2 = r2 + jnp.where(v2[q] > s2, 1.0, 0.0)
        n1_ref[hd] = n1
        c1_ref[hd] = jnp.exp(s1 - v1[0]) / z
        r2_ref[hd] = r2.astype(jnp.bfloat16)
        e2_ref[hd] = jnp.exp(s2 - v2[0]).astype(jnp.bfloat16)


def _peer_dense_body(h_ref, u_ref, v_ref, n1_ref, c1_ref, r2_ref, e2_ref, y_ref):
    j = pl.program_id(1)

    @pl.when(j == 0)
    def _():
        y_ref[...] = jnp.zeros_like(y_ref)

    h = h_ref[...]
    act = lax.dot_general(u_ref[...], h, (((1,), (1,)), ((), ())),
                          preferred_element_type=jnp.float32)
    groups = PEER_EXPERT_TILE // N_KEYS
    parts = []
    for aa in range(groups):
        w = None
        for hd in range(PEER_HEADS):
            picked = r2_ref[hd] < n1_ref[hd, aa:aa + 1, :].astype(jnp.bfloat16)
            term = jnp.where(picked, e2_ref[hd], 0.0) * c1_ref[hd, aa:aa + 1, :].astype(jnp.bfloat16)
            w = term if w is None else w + term
        g = _gelu_exact(act[aa * N_KEYS:(aa + 1) * N_KEYS]).astype(jnp.bfloat16)
        parts.append(w * g)
    p = jnp.concatenate(parts, axis=0)
    y_ref[...] += lax.dot_general(p, v_ref[...], (((0,), (0,)), ((), ())),
                                  preferred_element_type=jnp.float32)


def peer_pallas(h, wqT, k1, k2, u, v):
    n, d = h.shape
    t = PEER_TOKEN_TILE
    nt = n // t
    hk = (PEER_HEADS, N_KEYS)
    route = lambda dt: jax.ShapeDtypeStruct(hk + (n,), dt)
    route_spec = pl.BlockSpec(hk + (t,), lambda i: (0, 0, i))
    n1, c1, r2, e2 = pl.pallas_call(
        _peer_route_body,
        grid=(nt,),
        in_specs=[pl.BlockSpec((t, d), lambda i: (i, 0)),
                  pl.BlockSpec(wqT.shape, lambda i: (0, 0)),
                  pl.BlockSpec(k1.shape, lambda i: (0, 0, 0)),
                  pl.BlockSpec(k2.shape, lambda i: (0, 0, 0))],
        out_specs=[route_spec] * 4,
        out_shape=[route(jnp.float32), route(jnp.float32), route(jnp.bfloat16), route(jnp.bfloat16)],
        compiler_params=pltpu.CompilerParams(dimension_semantics=("parallel",),
                                             vmem_limit_bytes=VMEM_LIMIT_BYTES),
        name="peer_route",
    )(h, wqT, k1, k2)

    e = PEER_EXPERT_TILE
    groups = e // N_KEYS
    row_spec = pl.BlockSpec((PEER_HEADS, groups, t), lambda i, j: (0, j, i))
    full_spec = pl.BlockSpec(hk + (t,), lambda i, j: (0, 0, i))
    return pl.pallas_call(
        _peer_dense_body,
        grid=(nt, N_EXPERTS // e),
        in_specs=[pl.BlockSpec((t, d), lambda i, j: (i, 0)),
                  pl.BlockSpec((e, d), lambda i, j: (j, 0)),
                  pl.BlockSpec((e, d), lambda i, j: (j, 0)),
                  row_spec, row_spec, full_spec, full_spec],
        out_specs=pl.BlockSpec((t, d), lambda i, j: (i, 0)),
        out_shape=jax.ShapeDtypeStruct((n, d), jnp.float32),
        compiler_params=pltpu.CompilerParams(dimension_semantics=("parallel", "arbitrary"),
                                             vmem_limit_bytes=VMEM_LIMIT_BYTES),
        name="peer_dense",
    )(h, u, v, n1, c1, r2, e2)


ATT_TILE = 128
MASKED_BELOW = -0.5e30
KEY_PAIR = 2


def t5_bucket(dist):
    n = jnp.maximum(dist, 0)
    max_exact = N_BUCKETS // 2
    nf = jnp.maximum(n, max_exact).astype(jnp.float32)
    large = max_exact + (jnp.log(nf / max_exact) / math.log(MAX_DISTANCE / max_exact)
                         * (N_BUCKETS - max_exact)).astype(jnp.int32)
    large = jnp.minimum(large, N_BUCKETS - 1)
    return jnp.where(n < max_exact, n, large)


def _softmax_cols(s):
    m = jnp.max(s, axis=0, keepdims=True)
    e = jnp.where(s > MASKED_BELOW, jnp.exp(s - m), 0.0)
    l = jnp.sum(e, axis=0, keepdims=True)
    return e / jnp.maximum(l, TINY)


def _dot_f32_by_01(mat01, x):
    return sum(jnp.dot(mat01, part, preferred_element_type=jnp.float32) for part in _split3(x))


def _x_dot_01(x, mat01):
    return sum(jnp.dot(part, mat01, preferred_element_type=jnp.float32) for part in _split3(x))


def _online_tiles(k_ref, v_ref, qT, lo, hi, bias_of, mask_of, tile):
    width = qT.shape[1]
    tn = (((0,), (0,)), ((), ()))

    def step(j, carry):
        m, l, acc = carry
        rows = pl.ds(pl.multiple_of(j * tile, tile), tile)
        s = jnp.dot(k_ref[rows, :], qT, preferred_element_type=jnp.float32) + bias_of(j)
        s = mask_of(j, s)
        m_new = jnp.maximum(m, jnp.max(s, axis=0, keepdims=True))
        alpha = jnp.exp(m - m_new)
        e = jnp.where(s > MASKED_BELOW, jnp.exp(s - m_new), 0.0)
        l = alpha * l + jnp.sum(e, axis=0, keepdims=True)
        pv = lax.dot_general(v_ref[rows, :], e.astype(jnp.bfloat16), tn,
                             preferred_element_type=jnp.float32)
        return m_new, l, alpha * acc + pv

    init = (jnp.full((1, width), NEG_INF, jnp.float32), jnp.zeros((1, width), jnp.float32),
            jnp.zeros((HEAD_DIM, width), jnp.float32))
    m, l, acc = lax.fori_loop(lo, hi, step, init)
    return acc / jnp.maximum(l, TINY)


def _nsa_core(qT, gate_logits, q_pos, kc_ref, vc_ref, ks_ref, vs_ref, kw_ref, vw_ref, cmp_bias, mask_ref,
              sel_range, sel_bias, win_range, win_bias, *, tq, n_sel, sel_tile, win_tile):
    width = qT.shape[1]
    tn = (((0,), (0,)), ((), ()))
    p_cmp = _softmax_cols(jnp.dot(kc_ref[...], qT, preferred_element_type=jnp.float32) + cmp_bias)
    o_cmp = lax.dot_general(vc_ref[...], p_cmp.astype(jnp.bfloat16), tn, preferred_element_type=jnp.float32)
    ncp = p_cmp.shape[0]
    n_keys = mask_ref.shape[0]
    n_blocks = n_keys // SEL_BLOCK
    nb = lax.broadcasted_iota(jnp.int32, (n_blocks, ncp), 0) * SEL_BLOCK
    cs = lax.broadcasted_iota(jnp.int32, (n_blocks, ncp), 1) * CMP_STRIDE
    overlap = jnp.where((cs < nb + SEL_BLOCK) & (cs + CMP_LEN > nb), 1.0, 0.0).astype(jnp.bfloat16)
    la = lax.broadcasted_iota(jnp.int32, (width, width), 0)
    lb = lax.broadcasted_iota(jnp.int32, (width, width), 1)
    same_query = jnp.where((la % tq) == (lb % tq), 1.0, 0.0).astype(jnp.bfloat16)
    imp = _x_dot_01(_dot_f32_by_01(overlap, p_cmp), same_query)
    blk = lax.broadcasted_iota(jnp.int32, (n_blocks, width), 0)
    cur = q_pos // SEL_BLOCK
    forced = (blk == 0) | (blk == cur) | (blk == cur - 1)
    score = jnp.where(blk <= cur, jnp.where(forced, FORCE_SCORE, imp), -1.0)
    rank = jnp.zeros((n_blocks, width), jnp.float32)
    for mrow in range(n_blocks):
        row = score[mrow:mrow + 1, :]
        ahead = (row > score) | ((row == score) & (blk > mrow))
        rank = rank + jnp.where(ahead, 1.0, 0.0)
    sel = jnp.where(rank < n_sel, 1.0, 0.0).astype(jnp.bfloat16)
    kb = lax.broadcasted_iota(jnp.int32, (n_keys, n_blocks), 0) // SEL_BLOCK
    nn = lax.broadcasted_iota(jnp.int32, (n_keys, n_blocks), 1)
    expand = jnp.where(kb == nn, 1.0, 0.0).astype(jnp.bfloat16)
    mask_ref[...] = jnp.dot(expand, sel, preferred_element_type=jnp.float32)

    def sel_mask(j, s):
        mk = mask_ref[pl.ds(pl.multiple_of(j * sel_tile, sel_tile), sel_tile), :]
        return jnp.where(mk > 0.5, s, NEG_INF)

    o_sel = _online_tiles(ks_ref, vs_ref, qT, sel_range[0], sel_range[1], sel_bias, sel_mask, sel_tile)
    o_win = _online_tiles(kw_ref, vw_ref, qT, win_range[0], win_range[1], win_bias, lambda j, s: s, win_tile)
    g = 1.0 / (1.0 + jnp.exp(-gate_logits))
    return g[0:1] * o_cmp + g[1:2] * o_sel + g[2:3] * o_win


def _nsa_prompt_body(qT_ref, gate_ref, kc_ref, vc_ref, ks_ref, vs_ref, kw_ref, vw_ref, cb_ref, tz_ref,
                     o_ref, mask_ref, *, n_sel):
    i = pl.program_id(2)
    tq = ATT_TILE
    width = qT_ref.shape[1]
    q_pos = i * tq + lax.broadcasted_iota(jnp.int32, (1, width), 1) % tq
    masked_tile = tz_ref.shape[0] - 1

    def pair_bias(max_delta):
        def bias(jj):
            tiles = []
            for j in (KEY_PAIR * jj + t for t in range(KEY_PAIR)):
                idx = jnp.where(j > i, masked_tile, jnp.minimum(i - j, max_delta))
                tiles.append(tz_ref[idx])
            return jnp.concatenate(tiles, axis=0)
        return bias

    first_win = jnp.maximum(i - WINDOW // ATT_TILE, 0)
    o_ref[...] = _nsa_core(
        qT_ref[...], gate_ref[...], q_pos, kc_ref, vc_ref, ks_ref, vs_ref, kw_ref, vw_ref, cb_ref[...], mask_ref,
        (0, i // KEY_PAIR + 1), pair_bias(2),
        (first_win // KEY_PAIR, i // KEY_PAIR + 1), pair_bias(masked_tile),
        tq=tq, n_sel=n_sel, sel_tile=KEY_PAIR * ATT_TILE, win_tile=KEY_PAIR * ATT_TILE)


def _nsa_sample_body(qT_ref, gate_ref, kc_ref, vc_ref, ks_ref, vs_ref, kw_ref, vw_ref, cb_ref, sb_ref, wb_ref,
                     o_ref, mask_ref, *, n_sel, tq, past):
    width = qT_ref.shape[1]
    q_pos = past + lax.broadcasted_iota(jnp.int32, (1, width), 1) % tq
    o_ref[...] = _nsa_core(
        qT_ref[...], gate_ref[...], q_pos, kc_ref, vc_ref, ks_ref, vs_ref, kw_ref, vw_ref, cb_ref[...], mask_ref,
        (0, 1), lambda j: sb_ref[...],
        (0, 1), lambda j: wb_ref[...],
        tq=tq, n_sel=n_sel, sel_tile=sb_ref.shape[0], win_tile=wb_ref.shape[0])


def _bias_lanes(tbl, dist, ok):
    onehot = (t5_bucket(dist)[..., None] == jnp.arange(N_BUCKETS)).astype(jnp.float32)
    vals = jnp.dot(onehot.reshape(-1, N_BUCKETS), tbl.reshape(N_BUCKETS, -1),
                   precision=lax.Precision.HIGHEST).reshape(dist.shape + tbl.shape[1:])
    b = jnp.where(ok[..., None, None], vals, NEG_INF)
    nd = b.ndim
    b = jnp.moveaxis(b, (nd - 2, nd - 1), (0, nd - 2))
    return b.reshape(b.shape[:-2] + (b.shape[-2] * b.shape[-1],))


def nsa_prompt_attention(q, gate_logits, kcmp, vcmp, ks, vs, kw, vw, rel_bias):
    B, S, G, R, dh = q.shape
    t = ATT_TILE
    n_qt = S // t
    bf = jnp.bfloat16
    tbl = rel_bias.reshape(N_BUCKETS, G, R).astype(jnp.float32)
    qT = (q * (HEAD_DIM ** -0.5)).astype(bf).reshape(B, n_qt, t, G, R, dh)
    qT = qT.transpose(0, 3, 1, 5, 4, 2).reshape(B, G, n_qt, dh, R * t)
    gT = gate_logits.reshape(B, n_qt, t, G, R, 3).transpose(0, 3, 1, 5, 4, 2).reshape(B, G, n_qt, 3, R * t)
    n_cmp = kcmp.shape[1]
    ncp = _round_up(n_cmp, 8)
    padc = lambda x: jnp.pad(x.astype(bf).transpose(0, 2, 1, 3), ((0, 0), (0, 0), (0, ncp - n_cmp), (0, 0)))
    tr = lambda x: x.astype(bf).transpose(0, 2, 1, 3)
    qp = jnp.arange(S).reshape(n_qt, 1, t)
    c_idx = jnp.arange(ncp).reshape(1, ncp, 1)
    dist = qp - (c_idx * CMP_STRIDE + CMP_LEN - 1)
    cb = _bias_lanes(tbl, dist, (dist >= 0) & (c_idx < n_cmp))
    n_delta = WINDOW // t + 2
    d = (jnp.arange(n_delta).reshape(-1, 1, 1) * t + jnp.arange(t).reshape(1, 1, t)
         - jnp.arange(t).reshape(1, t, 1))
    tz = _bias_lanes(tbl, d, (d >= 0) & (d < WINDOW))
    body = functools.partial(_nsa_prompt_body, n_sel=min(SEL_TOPK, S // SEL_BLOCK))
    kv_spec = pl.BlockSpec((None, None, S, dh), lambda b, g, i: (b, g, 0, 0))
    cmp_spec = pl.BlockSpec((None, None, ncp, dh), lambda b, g, i: (b, g, 0, 0))
    oT = pl.pallas_call(
        body,
        grid=(B, G, n_qt),
        in_specs=[pl.BlockSpec((None, None, None, dh, R * t), lambda b, g, i: (b, g, i, 0, 0)),
                  pl.BlockSpec((None, None, None, 3, R * t), lambda b, g, i: (b, g, i, 0, 0)),
                  cmp_spec, cmp_spec, kv_spec, kv_spec, kv_spec, kv_spec,
                  pl.BlockSpec((None, None, ncp, R * t), lambda b, g, i: (g, i, 0, 0)),
                  pl.BlockSpec((None,) + tz.shape[1:], lambda b, g, i: (g, 0, 0, 0))],
        out_specs=pl.BlockSpec((None, None, None, dh, R * t), lambda b, g, i: (b, g, i, 0, 0)),
        out_shape=jax.ShapeDtypeStruct((B, G, n_qt, dh, R * t), jnp.float32),
        scratch_shapes=[pltpu.VMEM((S, R * t), jnp.float32)],
        compiler_params=pltpu.CompilerParams(dimension_semantics=("parallel", "parallel", "arbitrary"),
                                             vmem_limit_bytes=VMEM_LIMIT_BYTES),
        name="nsa_prompt_attention",
    )(qT, gT, padc(kcmp), padc(vcmp), tr(ks), tr(vs), tr(kw), tr(vw), cb, tz)
    o = oT.reshape(B, G, n_qt, dh, R, t).transpose(0, 2, 5, 1, 4, 3)
    return o.reshape(B, S, G * R * dh)


def nsa_sample_attention(q, gate_logits, kcmp, vcmp, ks_full, vs_full, kw_all, vw_all, rel_bias, past):
    DB, T, G, R, dh = q.shape
    bf = jnp.bfloat16
    width = R * T
    tbl = rel_bias.reshape(N_BUCKETS, G, R).astype(jnp.float32)
    qT = (q * (HEAD_DIM ** -0.5)).astype(bf).transpose(0, 2, 4, 3, 1).reshape(DB, G, dh, width)
    gT = gate_logits.transpose(0, 2, 4, 3, 1).reshape(DB, G, 3, width)
    q_pos = past + jnp.arange(T)

    def keys(x, n_pad):
        x = x.astype(bf).transpose(0, 2, 1, 3)
        return jnp.pad(x, ((0, 0), (0, 0), (0, n_pad - x.shape[2]), (0, 0)))

    n_cmp = kcmp.shape[1]
    ncp = _round_up(n_cmp, 8)
    c_idx = jnp.arange(ncp).reshape(ncp, 1)
    dist = q_pos.reshape(1, T) - (c_idx * CMP_STRIDE + CMP_LEN - 1)
    cb = _bias_lanes(tbl, dist, (dist >= 0) & (c_idx < n_cmp))
    lp = ks_full.shape[1]
    n_sel_keys = _round_up(lp, ATT_TILE)
    k_idx = jnp.arange(n_sel_keys).reshape(-1, 1)
    dist = q_pos.reshape(1, T) - k_idx
    sb = _bias_lanes(tbl, dist, dist >= 0)
    n_win = kw_all.shape[1]
    n_win_keys = _round_up(n_win, ATT_TILE)
    w_idx = jnp.arange(n_win_keys).reshape(-1, 1)
    w_pos = past - (n_win - T) + w_idx
    dist = q_pos.reshape(1, T) - w_pos
    wb = _bias_lanes(tbl, dist, (dist >= 0) & (dist < WINDOW) & (w_pos >= 0) & (w_idx < n_win))
    body = functools.partial(_nsa_sample_body, n_sel=min(SEL_TOPK, lp // SEL_BLOCK), tq=T, past=past)
    spec4 = lambda n: pl.BlockSpec((None, None, n, dh), lambda g, b: (b, g, 0, 0))
    table = lambda x: pl.BlockSpec((None,) + x.shape[1:], lambda g, b: (g, 0, 0))
    oT = pl.pallas_call(
        body,
        grid=(G, DB),
        in_specs=[pl.BlockSpec((None, None, dh, width), lambda g, b: (b, g, 0, 0)),
                  pl.BlockSpec((None, None, 3, width), lambda g, b: (b, g, 0, 0)),
                  spec4(ncp), spec4(ncp), spec4(n_sel_keys), spec4(n_sel_keys), spec4(n_win_keys), spec4(n_win_keys),
                  table(cb), table(sb), table(wb)],
        out_specs=pl.BlockSpec((None, None, dh, width), lambda g, b: (b, g, 0, 0)),
        out_shape=jax.ShapeDtypeStruct((DB, G, dh, width), jnp.float32),
        scratch_shapes=[pltpu.VMEM((n_sel_keys, width), jnp.float32)],
        compiler_params=pltpu.CompilerParams(dimension_semantics=("parallel", "parallel"),
                                             vmem_limit_bytes=VMEM_LIMIT_BYTES),
        name="nsa_sample_attention",
    )(qT, gT, keys(kcmp, ncp), keys(vcmp, ncp), keys(ks_full, n_sel_keys), keys(vs_full, n_sel_keys),
      keys(kw_all, n_win_keys), keys(vw_all, n_win_keys), cb, sb, wb)
    o = oT.reshape(DB, G, dh, R, T).transpose(0, 4, 1, 3, 2)
    return o.reshape(DB, T, G * R * dh)


FOX_Q_TILE = 512
FOX_K_TILE = 256


def _softmax_step(s, v, carry, masked):
    m, l, acc = carry
    m_new = jnp.maximum(m, jnp.max(s, axis=0, keepdims=True))
    alpha = jnp.exp(m - m_new)
    e = jnp.exp(s - m_new)
    if masked:
        e = jnp.where(s > MASKED_BELOW, e, 0.0)
    l = alpha * l + jnp.sum(e, axis=0, keepdims=True)
    pv = lax.dot_general(v, e.astype(jnp.bfloat16), (((0,), (0,)), ((), ())),
                         preferred_element_type=jnp.float32)
    return m_new, l, alpha * acc + pv


def _fox_prompt_body(qT_ref, k_ref, v_ref, o_ref):
    i = pl.program_id(2)
    qT = qT_ref[...]
    width = qT.shape[1]

    def tile(j):
        rows = pl.ds(pl.multiple_of(j * FOX_K_TILE, FOX_K_TILE), FOX_K_TILE)
        return jnp.dot(k_ref[rows, :], qT, preferred_element_type=jnp.float32), v_ref[rows, :]

    def full_step(j, carry):
        s, v = tile(j)
        return _softmax_step(s, v, carry, masked=False)

    carry = (jnp.full((1, width), NEG_INF, jnp.float32), jnp.zeros((1, width), jnp.float32),
             jnp.zeros((HEAD_DIM, width), jnp.float32))
    ratio = FOX_Q_TILE // FOX_K_TILE
    carry = lax.fori_loop(0, i * ratio, full_step, carry)
    q_pos = i * FOX_Q_TILE + lax.broadcasted_iota(jnp.int32, (FOX_K_TILE, width), 1)
    for dj in range(ratio):
        j = i * ratio + dj
        s, v = tile(j)
        k_pos = j * FOX_K_TILE + lax.broadcasted_iota(jnp.int32, (FOX_K_TILE, width), 0)
        s = jnp.where(k_pos <= q_pos, s, NEG_INF)
        carry = _softmax_step(s, v, carry, masked=True)
    m, l, acc = carry
    o_ref[...] = acc / jnp.maximum(l, TINY)


def _augment(c):
    hi, mid, lo = _split3_outside_kernel(c)
    one = jnp.ones_like(hi)
    return jnp.stack([hi, mid, lo, one, one, one], axis=-1)


def fox_prompt_attention(q, k, v, logf):
    B, S, H, dh = q.shape
    bf = jnp.bfloat16
    c = jnp.cumsum(logf, axis=1).transpose(0, 2, 1)
    aug = _augment(c)
    pad = jnp.zeros((B, H, S, LANES - dh - 6), bf)
    kb = k.astype(bf).transpose(0, 2, 1, 3)
    k_aug = jnp.concatenate([kb, -aug[..., :3], aug[..., 3:], pad], axis=-1)
    qb = (q * (dh ** -0.5)).astype(bf).transpose(0, 2, 1, 3)
    q_aug = jnp.concatenate([qb, aug[..., 3:], aug[..., :3], pad], axis=-1)
    qT = q_aug.transpose(0, 1, 3, 2)
    vb = v.astype(bf).transpose(0, 2, 1, 3)
    oT = pl.pallas_call(
        _fox_prompt_body,
        grid=(B, H, S // FOX_Q_TILE),
        in_specs=[pl.BlockSpec((None, None, LANES, FOX_Q_TILE), lambda b, h, i: (b, h, 0, i)),
                  pl.BlockSpec((None, None, S, LANES), lambda b, h, i: (b, h, 0, 0)),
                  pl.BlockSpec((None, None, S, dh), lambda b, h, i: (b, h, 0, 0))],
        out_specs=pl.BlockSpec((None, None, dh, FOX_Q_TILE), lambda b, h, i: (b, h, 0, i)),
        out_shape=jax.ShapeDtypeStruct((B, H, dh, S), jnp.float32),
        compiler_params=pltpu.CompilerParams(dimension_semantics=("parallel", "parallel", "arbitrary"),
                                             vmem_limit_bytes=VMEM_LIMIT_BYTES),
        name="fox_prompt_attention",
    )(qT, k_aug, vb)
    return oT.transpose(0, 3, 1, 2).reshape(B, S, H * dh)


def _fox_sample_body(pt_ref, qbd_ref, kp_ref, vp_ref, bp_ref, kn_ref, vn_ref, bn_ref, o_ref,
                     m_ref, l_ref, acc_ref, *, n_q):
    p = pl.program_id(1)
    width = qbd_ref.shape[0]
    rows = kn_ref.shape[0]

    @pl.when(p == 0)
    def _():
        m_ref[...] = jnp.full(m_ref.shape, NEG_INF, jnp.float32)
        l_ref[...] = jnp.zeros(l_ref.shape, jnp.float32)
        acc_ref[...] = jnp.zeros(acc_ref.shape, jnp.float32)

    def attend(k2, v2, bias_t):
        s_t = jnp.dot(qbd_ref[...], k2, preferred_element_type=jnp.float32).T + bias_t
        m = m_ref[...]
        m_new = jnp.maximum(m, jnp.max(s_t, axis=0, keepdims=True))
        alpha = jnp.exp(m - m_new)
        e = jnp.where(s_t > MASKED_BELOW, jnp.exp(s_t - m_new), 0.0)
        m_ref[...] = m_new
        l_ref[...] = alpha * l_ref[...] + jnp.sum(e, axis=0, keepdims=True)
        acc_ref[...] = alpha * acc_ref[...] + jnp.dot(v2, e.astype(jnp.bfloat16),
                                                      preferred_element_type=jnp.float32)

    page = lambda ref: ref[...].reshape(rows, PAGE_SIZE).astype(jnp.bfloat16)
    attend(page(kp_ref), page(vp_ref), bp_ref[...])

    @pl.when(p == pl.num_programs(1) - 1)
    def _():
        attend(kn_ref[...], vn_ref[...], bn_ref[...])
        out = acc_ref[...] / jnp.maximum(l_ref[...], TINY)
        r = lax.broadcasted_iota(jnp.int32, (rows, width), 0) // HEAD_DIM
        c = lax.broadcasted_iota(jnp.int32, (rows, width), 1) // n_q
        out = jnp.where(r == c, out, 0.0)
        la = lax.broadcasted_iota(jnp.int32, (width, width), 0) % n_q
        lb = lax.broadcasted_iota(jnp.int32, (width, width), 1)
        gather_q = jnp.where(la == lb, 1.0, 0.0).astype(jnp.bfloat16)
        o_ref[...] = _x_dot_01(out, gather_q)[:, :n_q]


def fox_sample_attention(q, k, v, logf, k_pool, v_pool, f_pool, page_table):
    DB, T, H, dh = q.shape
    n_pages = page_table.shape[1]
    past = n_pages * PAGE_SIZE
    bf = jnp.bfloat16
    width = H * T
    rows = H * dh
    f_all = jnp.concatenate([f_pool[page_table].reshape(DB, past, H), logf], axis=1)
    c = jnp.cumsum(f_all, axis=1)
    c_q = c[:, past:].transpose(0, 2, 1).reshape(DB, 1, 1, width)
    c_k = jnp.repeat(c[:, :past].reshape(DB, n_pages, PAGE_SIZE, H), T, axis=-1)
    bias_past = c_q - c_k
    tok = jnp.arange(PAGE_SIZE).reshape(1, PAGE_SIZE, 1)
    qi = (jnp.arange(width) % T).reshape(1, 1, width)
    c_new = jnp.pad(jnp.repeat(c[:, past:], T, axis=-1), ((0, 0), (0, PAGE_SIZE - T), (0, 0)))
    bias_new = jnp.where((tok <= qi) & (tok < T), c_q[:, 0] - c_new, NEG_INF)
    qh = (q * (dh ** -0.5)).transpose(0, 2, 1, 3)
    same_head = jnp.eye(H, dtype=qh.dtype).reshape(1, H, 1, H, 1)
    qbd = (qh[:, :, :, None, :] * same_head).astype(bf).reshape(DB, width, rows)
    new_t = lambda x: jnp.pad(x.astype(bf).transpose(0, 2, 3, 1),
                              ((0, 0), (0, 0), (0, 0), (0, PAGE_SIZE - T))).reshape(DB, rows, PAGE_SIZE)
    pool_t = lambda x: x.transpose(0, 2, 3, 1)
    body = functools.partial(_fox_sample_body, n_q=T)
    page_spec = pl.BlockSpec((None, H, dh, PAGE_SIZE), lambda b, p, pt: (pt[b, p], 0, 0, 0))
    per_b = lambda shape: pl.BlockSpec((None,) + shape, lambda b, p, pt: (b,) + (0,) * len(shape))
    o = pl.pallas_call(
        body,
        grid_spec=pltpu.PrefetchScalarGridSpec(
            num_scalar_prefetch=1,
            grid=(DB, n_pages),
            in_specs=[per_b((width, rows)), page_spec, page_spec,
                      pl.BlockSpec((None, None, PAGE_SIZE, width), lambda b, p, pt: (b, p, 0, 0)),
                      per_b((rows, PAGE_SIZE)), per_b((rows, PAGE_SIZE)), per_b((PAGE_SIZE, width))],
            out_specs=per_b((rows, T)),
            scratch_shapes=[pltpu.VMEM((1, width), jnp.float32), pltpu.VMEM((1, width), jnp.float32),
                            pltpu.VMEM((rows, width), jnp.float32)]),
        out_shape=jax.ShapeDtypeStruct((DB, rows, T), jnp.float32),
        compiler_params=pltpu.CompilerParams(dimension_semantics=("parallel", "arbitrary"),
                                             vmem_limit_bytes=VMEM_LIMIT_BYTES),
        name="fox_sample_attention",
    )(page_table, qbd, pool_t(k_pool), pool_t(v_pool), bias_past, new_t(k), new_t(v), bias_new)
    return o.reshape(DB, H, dh, T).transpose(0, 3, 1, 2).reshape(DB, T, H * dh)


def compress(x, pe, w1, w2):
    B, T, G, dh = x.shape
    r = CMP_LEN // CMP_STRIDE
    n_chunks = T // CMP_STRIDE
    nc = n_chunks - r + 1
    ch = x.reshape(B, n_chunks, CMP_STRIDE, G, dh)
    blk = jnp.concatenate([ch[:, m:m + nc] for m in range(r)], axis=2)
    blk = blk + pe[:, None, :]
    flat = blk.transpose(0, 1, 3, 2, 4).reshape(B, nc, G, CMP_LEN * dh)
    return jax.nn.gelu(flat @ w1, approximate=False) @ w2


def nsa_split(proj):
    B, T, _ = proj.shape
    nq = NSA_HEADS * HEAD_DIM
    nkv = NSA_KV_HEADS * HEAD_DIM
    q = proj[..., :nq].reshape(B, T, NSA_KV_HEADS, NSA_GROUP, HEAD_DIM)
    kv = proj[..., nq:nq + 6 * nkv].reshape(B, T, 6, NSA_KV_HEADS, HEAD_DIM)
    gl = proj[..., nq + 6 * nkv:].reshape(B, T, NSA_KV_HEADS, NSA_GROUP, 3)
    return q, gl, [kv[:, :, s] for s in range(6)]


def nsa_prompt(proj, cw, rel_bias):
    S = proj.shape[1]
    pe_k, w1_k, w2_k, pe_v, w1_v, w2_v = cw
    q, gl, (kc, vc, ks, vs, kw, vw) = nsa_split(proj)
    kcmp = compress(kc, pe_k, w1_k, w2_k)
    vcmp = compress(vc, pe_v, w1_v, w2_v)
    o = nsa_prompt_attention(q, gl, kcmp, vcmp, ks, vs, kw, vw, rel_bias)
    wb = min(WINDOW, S)
    return o, [kc, vc, ks, vs, kw[:, S - wb:], vw[:, S - wb:]]


def nsa_sample(proj, cw, rel_bias, ck_pool, cv_pool, sk_pool, sv_pool, wk_buf, wv_buf, page_table):
    DB, T, _ = proj.shape
    pe_k, w1_k, w2_k, pe_v, w1_v, w2_v = cw
    q, gl, (kc, vc, ks, vs, kw, vw) = nsa_split(proj)
    past = page_table.shape[1] * PAGE_SIZE
    L = past + T
    Lp = _round_up(L, SEL_BLOCK)

    def full(pool, new):
        old = pool[page_table].reshape(DB, past, NSA_KV_HEADS, HEAD_DIM)
        return jnp.pad(jnp.concatenate([old, new], axis=1), ((0, 0), (0, Lp - L), (0, 0), (0, 0)))

    kcmp = compress(full(ck_pool, kc), pe_k, w1_k, w2_k)
    vcmp = compress(full(cv_pool, vc), pe_v, w1_v, w2_v)
    kw_all = jnp.concatenate([wk_buf, kw], axis=1)
    vw_all = jnp.concatenate([wv_buf, vw], axis=1)
    o = nsa_sample_attention(q, gl, kcmp, vcmp, full(sk_pool, ks), full(sv_pool, vs), kw_all, vw_all,
                             rel_bias, past)
    return o, [kc, vc, ks, vs, kw_all[:, T:], vw_all[:, T:]]


def kernel(x_prompt, x_sample, p_prompt, p_sample, cache_fox_k, cache_fox_v, cache_fox_logf,
           cache_nsa_cmp_k, cache_nsa_cmp_v, cache_nsa_sel_k, cache_nsa_sel_v,
           cache_nsa_win_k, cache_nsa_win_v, page_table,
           norm_mix, norm_ffn, norm_ple, norm_final,
           fox_w_in, fox_b_f, fox_w_out, nsa_w_in, nsa_w_out,
           cmp_pe_k, cmp_w1_k, cmp_w2_k, cmp_pe_v, cmp_w1_v, cmp_w2_v, rel_bias,
           peer_wq, peer_sub_k1, peer_sub_k2, peer_u, peer_v, ple_w_proj, ple_w_gate):
    B, S, d = x_prompt.shape
    DB, T, _ = x_sample.shape
    n_p, n_s = B * S, DB * T
    bf = jnp.bfloat16
    rows = lambda a, b: jnp.concatenate([a.reshape(n_p, -1), b.reshape(n_s, -1)], axis=0)
    x = rows(x_prompt, x_sample)
    fox_p, fox_s, nsa_p, nsa_s = [], [], [], []
    for i in range(DEPTH):
        j = i // 2
        if i % 2 == 0:
            h = rmsnorm_pallas(x, norm_mix[i], bf)
            proj = linear_pallas(h, fox_w_in[j])
            nh = FOX_HEADS * HEAD_DIM
            logf = jax.nn.log_sigmoid(proj[:, 3 * nh:] + fox_b_f[j])
            heads = lambda a, lead: a.reshape(lead + (FOX_HEADS, -1))
            qp, kp, vp = (heads(proj[:n_p, s * nh:(s + 1) * nh], (B, S)) for s in range(3))
            qs, ks_, vs_ = (heads(proj[n_p:, s * nh:(s + 1) * nh], (DB, T)) for s in range(3))
            fp, fs = logf[:n_p].reshape(B, S, FOX_HEADS), logf[n_p:].reshape(DB, T, FOX_HEADS)
            op = fox_prompt_attention(qp, kp, vp, fp)
            os_ = fox_sample_attention(qs, ks_, vs_, fs, cache_fox_k[j], cache_fox_v[j], cache_fox_logf[j],
                                       page_table)
            fox_p.append([kp, vp, fp])
            fox_s.append([ks_, vs_, fs])
            y = linear_pallas(rows(op, os_), fox_w_out[j])
        else:
            cw = (cmp_pe_k[j], cmp_w1_k[j], cmp_w2_k[j], cmp_pe_v[j], cmp_w1_v[j], cmp_w2_v[j])
            h = rmsnorm_pallas(x, norm_mix[i], bf)
            proj = linear_pallas(h, nsa_w_in[j])
            op, stp = nsa_prompt(proj[:n_p].reshape(B, S, -1), cw, rel_bias)
            os_, sts = nsa_sample(proj[n_p:].reshape(DB, T, -1), cw, rel_bias,
                                  cache_nsa_cmp_k[j], cache_nsa_cmp_v[j], cache_nsa_sel_k[j],
                                  cache_nsa_sel_v[j], cache_nsa_win_k[j], cache_nsa_win_v[j], page_table)
            nsa_p.append(stp)
            nsa_s.append(sts)
            y = linear_pallas(rows(op, os_), nsa_w_out[j])
        x = x + y
        h = rmsnorm_pallas(x, norm_ffn[i], bf)
        x = x + peer_pallas(h, peer_wq[i].T.astype(bf), peer_sub_k1[i].astype(bf), peer_sub_k2[i].astype(bf),
                            peer_u[i].astype(bf), peer_v[i].astype(bf))
        x = ple_pallas(x, rows(p_prompt[i], p_sample[i]), norm_ple[i], ple_w_proj[i], ple_w_gate[i])
    y = rmsnorm_pallas(x, norm_final, jnp.float32)
    st = lambda lst, k: jnp.stack([s[k] for s in lst])
    return (y[:n_p].reshape(B, S, d), y[n_p:].reshape(DB, T, d),
            st(fox_p, 0), st(fox_p, 1), st(fox_p, 2),
            st(fox_s, 0), st(fox_s, 1), st(fox_s, 2),
            st(nsa_p, 0), st(nsa_p, 1), st(nsa_p, 2), st(nsa_p, 3), st(nsa_p, 4), st(nsa_p, 5),
            st(nsa_s, 0), st(nsa_s, 1), st(nsa_s, 2), st(nsa_s, 3), st(nsa_s, 4), st(nsa_s, 5))
```

```python
import functools
import math

import jax
import jax.numpy as jnp
from jax import lax
from jax.experimental import pallas as pl
from jax.experimental.pallas import tpu as pltpu

D_MODEL = 1024
DEPTH = 2
PAGE_SIZE = 128
HEAD_DIM = 64
FOX_HEADS = D_MODEL // HEAD_DIM
NSA_HEADS = D_MODEL // HEAD_DIM
NSA_KV_HEADS = 4
NSA_GROUP = NSA_HEADS // NSA_KV_HEADS
CMP_LEN = 32
CMP_STRIDE = 16
SEL_BLOCK = 64
SEL_TOPK = 16
WINDOW = 512
N_BUCKETS = 32
MAX_DISTANCE = 128
PEER_HEADS = 8
PEER_TOPK = 16
N_KEYS = 128
N_EXPERTS = N_KEYS * N_KEYS
PEER_DKEY = 256
FORCE_SCORE = 1e4
RMS_EPS = 1e-6
NEG_INF = -1e30
TINY = 1e-30

VMEM_LIMIT_BYTES = 56 * 1024 * 1024
LANES = 128
ROW_TILE = 512
MAX_COL_TILE = 1024


def _round_up(x, m):
    return -(-x // m) * m


def _split3(x):
    hi = x.astype(jnp.bfloat16)
    r1 = x - hi.astype(jnp.float32)
    mid = r1.astype(jnp.bfloat16)
    lo = (r1 - mid.astype(jnp.float32)).astype(jnp.bfloat16)
    return hi, mid, lo


def _split3_outside_kernel(x):
    to_bf16 = lambda a: lax.reduce_precision(a, exponent_bits=8, mantissa_bits=7)
    hi = to_bf16(x)
    r1 = x - hi
    mid = to_bf16(r1)
    lo = to_bf16(r1 - mid)
    return hi.astype(jnp.bfloat16), mid.astype(jnp.bfloat16), lo.astype(jnp.bfloat16)


def _rms(x, g):
    return x * lax.rsqrt(jnp.mean(x * x, axis=-1, keepdims=True) + RMS_EPS) * g


def _rmsnorm_body(x_ref, g_ref, o_ref):
    o_ref[...] = _rms(x_ref[...], g_ref[...]).astype(o_ref.dtype)


def rmsnorm_pallas(x, g, out_dtype):
    n, d = x.shape
    return pl.pallas_call(
        _rmsnorm_body,
        grid=(n // ROW_TILE,),
        in_specs=[pl.BlockSpec((ROW_TILE, d), lambda i: (i, 0)),
                  pl.BlockSpec((1, d), lambda i: (0, 0))],
        out_specs=pl.BlockSpec((ROW_TILE, d), lambda i: (i, 0)),
        out_shape=jax.ShapeDtypeStruct((n, d), out_dtype),
        compiler_params=pltpu.CompilerParams(dimension_semantics=("parallel",)),
        name="rmsnorm",
    )(x, g.reshape(1, d))


def _linear_body(x_ref, w_ref, o_ref):
    o_ref[...] = jnp.dot(x_ref[...].astype(jnp.bfloat16), w_ref[...], preferred_element_type=jnp.float32)


def linear_pallas(x, w):
    n, k = x.shape
    m = w.shape[1]
    mp = _round_up(m, LANES)
    tn = max(t for t in range(LANES, MAX_COL_TILE + 1, LANES) if mp % t == 0)
    wb = jnp.pad(w.astype(jnp.bfloat16), ((0, 0), (0, mp - m)))
    out = pl.pallas_call(
        _linear_body,
        grid=(n // ROW_TILE, mp // tn),
        in_specs=[pl.BlockSpec((ROW_TILE, k), lambda i, j: (i, 0)),
                  pl.BlockSpec((k, tn), lambda i, j: (0, j))],
        out_specs=pl.BlockSpec((ROW_TILE, tn), lambda i, j: (i, j)),
        out_shape=jax.ShapeDtypeStruct((n, mp), jnp.float32),
        compiler_params=pltpu.CompilerParams(dimension_semantics=("parallel", "parallel"),
                                             vmem_limit_bytes=VMEM_LIMIT_BYTES),
        name="linear",
    )(x, wb)
    return out[:, :m]


def _ple_body(x_ref, p_ref, g_ref, wg_ref, wp_ref, o_ref):
    x = x_ref[...]
    h = _rms(x, g_ref[...]).astype(jnp.bfloat16)
    gate = 1.0 / (1.0 + jnp.exp(-jnp.dot(h, wg_ref[...], preferred_element_type=jnp.float32)))
    proj = jnp.dot(p_ref[...].astype(jnp.bfloat16), wp_ref[...], preferred_element_type=jnp.float32)
    o_ref[...] = x + gate * proj


def ple_pallas(x, p, g, w_proj, w_gate):
    n, d = x.shape
    dp = p.shape[1]
    return pl.pallas_call(
        _ple_body,
        grid=(n // ROW_TILE,),
        in_specs=[pl.BlockSpec((ROW_TILE, d), lambda i: (i, 0)),
                  pl.BlockSpec((ROW_TILE, dp), lambda i: (i, 0)),
                  pl.BlockSpec((1, d), lambda i: (0, 0)),
                  pl.BlockSpec((d, d), lambda i: (0, 0)),
                  pl.BlockSpec((dp, d), lambda i: (0, 0))],
        out_specs=pl.BlockSpec((ROW_TILE, d), lambda i: (i, 0)),
        out_shape=jax.ShapeDtypeStruct((n, d), jnp.float32),
        compiler_params=pltpu.CompilerParams(dimension_semantics=("parallel",),
                                             vmem_limit_bytes=VMEM_LIMIT_BYTES),
        name="ple",
    )(x, p, g.reshape(1, d), w_gate.astype(jnp.bfloat16), w_proj.astype(jnp.bfloat16))


PEER_TOKEN_TILE = 512
PEER_EXPERT_TILE = 1024
SQRT_HALF = 0.7071067811865476


def _gelu_exact(x):
    return 0.5 * x * (1.0 + lax.erf(x * SQRT_HALF))


def _top_rows(x, k):
    vals = []
    for _ in range(k):
        m = jnp.max(x, axis=0, keepdims=True)
        vals.append(m)
        x = jnp.where(x == m, NEG_INF, x)
    return vals


def _peer_route_body(h_ref, wqT_ref, k1_ref, k2_ref, n1_ref, c1_ref, r2_ref, e2_ref):
    h = h_ref[...]
    half = PEER_DKEY // 2
    nt = (((1,), (1,)), ((), ()))
    for hd in range(PEER_HEADS):
        qv = lax.dot_general(wqT_ref[hd * PEER_DKEY:(hd + 1) * PEER_DKEY, :], h, nt,
                             preferred_element_type=jnp.float32)
        s1 = jnp.dot(k1_ref[hd], qv[:half].astype(jnp.bfloat16), preferred_element_type=jnp.float32)
        s2 = jnp.dot(k2_ref[hd], qv[half:].astype(jnp.bfloat16), preferred_element_type=jnp.float32)
        v1 = _top_rows(s1, PEER_TOPK)
        v2 = _top_rows(s2, PEER_TOPK)
        v2_stack = jnp.concatenate(v2, axis=0)
        blocks = []
        for p in range(PEER_TOPK):
            n_p = PEER_TOPK // (p + 1)
            rows = -(-n_p // 8) * 8
            blk = v1[p] + v2_stack[:rows]
            if n_p < rows:
                r = lax.broadcasted_iota(jnp.int32, blk.shape, 0)
                blk = jnp.where(r < n_p, blk, NEG_INF)
            blocks.append(blk)
        c = _top_rows(jnp.concatenate(blocks, axis=0), PEER_TOPK)
        z = jnp.ones_like(c[0])
        for kk in range(1, PEER_TOPK):
            z = z + jnp.exp(c[kk] - c[0])
        tau = c[PEER_TOPK - 1]
        n1 = jnp.zeros_like(s1)
        r2 = jnp.zeros_like(s2)
        for q in range(PEER_TOPK):
            n1 = n1 + jnp.where(s1 + v2[q] >= tau, 1.0, 0.0)
            r2 = r2 + jnp.where(v2[q] > s2, 1.0, 0.0)
        n1_ref[hd] = n1
        c1_ref[hd] = jnp.exp(s1 - v1[0]) / z
        r2_ref[hd] = r2.astype(jnp.bfloat16)
        e2_ref[hd] = jnp.exp(s2 - v2[0]).astype(jnp.bfloat16)


def _peer_dense_body(h_ref, u_ref, v_ref, n1_ref, c1_ref, r2_ref, e2_ref, y_ref):
    j = pl.program_id(1)

    @pl.when(j == 0)
    def _():
        y_ref[...] = jnp.zeros_like(y_ref)

    h = h_ref[...]
    act = lax.dot_general(u_ref[...], h, (((1,), (1,)), ((), ())),
                          preferred_element_type=jnp.float32)
    groups = PEER_EXPERT_TILE // N_KEYS
    parts = []
    for aa in range(groups):
        w = None
        for hd in range(PEER_HEADS):
            picked = r2_ref[hd] < n1_ref[hd, aa:aa + 1, :].astype(jnp.bfloat16)
            term = jnp.where(picked, e2_ref[hd], 0.0) * c1_ref[hd, aa:aa + 1, :].astype(jnp.bfloat16)
            w = term if w is None else w + term
        g = _gelu_exact(act[aa * N_KEYS:(aa + 1) * N_KEYS]).astype(jnp.bfloat16)
        parts.append(w * g)
    p = jnp.concatenate(parts, axis=0)
    y_ref[...] += lax.dot_general(p, v_ref[...], (((0,), (0,)), ((), ())),
                                  preferred_element_type=jnp.float32)


def peer_pallas(h, wqT, k1, k2, u, v):
    n, d = h.shape
    t = PEER_TOKEN_TILE
    nt = n // t
    hk = (PEER_HEADS, N_KEYS)
    route = lambda dt: jax.ShapeDtypeStruct(hk + (n,), dt)
    route_spec = pl.BlockSpec(hk + (t,), lambda i: (0, 0, i))
    n1, c1, r2, e2 = pl.pallas_call(
        _peer_route_body,
        grid=(nt,),
        in_specs=[pl.BlockSpec((t, d), lambda i: (i, 0)),
                  pl.BlockSpec(wqT.shape, lambda i: (0, 0)),
                  pl.BlockSpec(k1.shape, lambda i: (0, 0, 0)),
                  pl.BlockSpec(k2.shape, lambda i: (0, 0, 0))],
        out_specs=[route_spec] * 4,
        out_shape=[route(jnp.float32), route(jnp.float32), route(jnp.bfloat16), route(jnp.bfloat16)],
        compiler_params=pltpu.CompilerParams(dimension_semantics=("parallel",),
                                             vmem_limit_bytes=VMEM_LIMIT_BYTES),
        name="peer_route",
    )(h, wqT, k1, k2)

    e = PEER_EXPERT_TILE
    groups = e // N_KEYS
    row_spec = pl.BlockSpec((PEER_HEADS, groups, t), lambda i, j: (0, j, i))
    full_spec = pl.BlockSpec(hk + (t,), lambda i, j: (0, 0, i))
    return pl.pallas_call(
        _peer_dense_body,
        grid=(nt, N_EXPERTS // e),
        in_specs=[pl.BlockSpec((t, d), lambda i, j: (i, 0)),
                  pl.BlockSpec((e, d), lambda i, j: (j, 0)),
                  pl.BlockSpec((e, d), lambda i, j: (j, 0)),
                  row_spec, row_spec, full_spec, full_spec],
        out_specs=pl.BlockSpec((t, d), lambda i, j: (i, 0)),
        out_shape=jax.ShapeDtypeStruct((n, d), jnp.float32),
        compiler_params=pltpu.CompilerParams(dimension_semantics=("parallel", "arbitrary"),
                                             vmem_limit_bytes=VMEM_LIMIT_BYTES),
        name="peer_dense",
    )(h, u, v, n1, c1, r2, e2)


ATT_TILE = 128
MASKED_BELOW = -0.5e30
KEY_PAIR = 2


def t5_bucket(dist):
    n = jnp.maximum(dist, 0)
    max_exact = N_BUCKETS // 2
    nf = jnp.maximum(n, max_exact).astype(jnp.float32)
    large = max_exact + (jnp.log(nf / max_exact) / math.log(MAX_DISTANCE / max_exact)
                         * (N_BUCKETS - max_exact)).astype(jnp.int32)
    large = jnp.minimum(large, N_BUCKETS - 1)
    return jnp.where(n < max_exact, n, large)


def _softmax_cols(s):
    m = jnp.max(s, axis=0, keepdims=True)
    e = jnp.where(s > MASKED_BELOW, jnp.exp(s - m), 0.0)
    l = jnp.sum(e, axis=0, keepdims=True)
    return e / jnp.maximum(l, TINY)


def _dot_f32_by_01(mat01, x):
    return sum(jnp.dot(mat01, part, preferred_element_type=jnp.float32) for part in _split3(x))


def _x_dot_01(x, mat01):
    return sum(jnp.dot(part, mat01, preferred_element_type=jnp.float32) for part in _split3(x))


def _online_tiles(k_ref, v_ref, qT, lo, hi, bias_of, mask_of, tile):
    width = qT.shape[1]
    tn = (((0,), (0,)), ((), ()))

    def step(j, carry):
        m, l, acc = carry
        rows = pl.ds(pl.multiple_of(j * tile, tile), tile)
        s = jnp.dot(k_ref[rows, :], qT, preferred_element_type=jnp.float32) + bias_of(j)
        s = mask_of(j, s)
        m_new = jnp.maximum(m, jnp.max(s, axis=0, keepdims=True))
        alpha = jnp.exp(m - m_new)
        e = jnp.where(s > MASKED_BELOW, jnp.exp(s - m_new), 0.0)
        l = alpha * l + jnp.sum(e, axis=0, keepdims=True)
        pv = lax.dot_general(v_ref[rows, :], e.astype(jnp.bfloat16), tn,
                             preferred_element_type=jnp.float32)
        return m_new, l, alpha * acc + pv

    init = (jnp.full((1, width), NEG_INF, jnp.float32), jnp.zeros((1, width), jnp.float32),
            jnp.zeros((HEAD_DIM, width), jnp.float32))
    m, l, acc = lax.fori_loop(lo, hi, step, init)
    return acc / jnp.maximum(l, TINY)


def _nsa_core(qT, gate_logits, q_pos, kc_ref, vc_ref, ks_ref, vs_ref, kw_ref, vw_ref, cmp_bias, mask_ref,
              sel_range, sel_bias, win_range, win_bias, *, tq, n_sel, sel_tile, win_tile):
    width = qT.shape[1]
    tn = (((0,), (0,)), ((), ()))
    p_cmp = _softmax_cols(jnp.dot(kc_ref[...], qT, preferred_element_type=jnp.float32) + cmp_bias)
    o_cmp = lax.dot_general(vc_ref[...], p_cmp.astype(jnp.bfloat16), tn, preferred_element_type=jnp.float32)
    ncp = p_cmp.shape[0]
    n_keys = mask_ref.shape[0]
    n_blocks = n_keys // SEL_BLOCK
    nb = lax.broadcasted_iota(jnp.int32, (n_blocks, ncp), 0) * SEL_BLOCK
    cs = lax.broadcasted_iota(jnp.int32, (n_blocks, ncp), 1) * CMP_STRIDE
    overlap = jnp.where((cs < nb + SEL_BLOCK) & (cs + CMP_LEN > nb), 1.0, 0.0).astype(jnp.bfloat16)
    la = lax.broadcasted_iota(jnp.int32, (width, width), 0)
    lb = lax.broadcasted_iota(jnp.int32, (width, width), 1)
    same_query = jnp.where((la % tq) == (lb % tq), 1.0, 0.0).astype(jnp.bfloat16)
    imp = _x_dot_01(_dot_f32_by_01(overlap, p_cmp), same_query)
    blk = lax.broadcasted_iota(jnp.int32, (n_blocks, width), 0)
    cur = q_pos // SEL_BLOCK
    forced = (blk == 0) | (blk == cur) | (blk == cur - 1)
    score = jnp.where(blk <= cur, jnp.where(forced, FORCE_SCORE, imp), -1.0)
    rank = jnp.zeros((n_blocks, width), jnp.float32)
    for mrow in range(n_blocks):
        row = score[mrow:mrow + 1, :]
        ahead = (row > score) | ((row == score) & (blk > mrow))
        rank = rank + jnp.where(ahead, 1.0, 0.0)
    sel = jnp.where(rank < n_sel, 1.0, 0.0).astype(jnp.bfloat16)
    kb = lax.broadcasted_iota(jnp.int32, (n_keys, n_blocks), 0) // SEL_BLOCK
    nn = lax.broadcasted_iota(jnp.int32, (n_keys, n_blocks), 1)
    expand = jnp.where(kb == nn, 1.0, 0.0).astype(jnp.bfloat16)
    mask_ref[...] = jnp.dot(expand, sel, preferred_element_type=jnp.float32)

    def sel_mask(j, s):
        mk = mask_ref[pl.ds(pl.multiple_of(j * sel_tile, sel_tile), sel_tile), :]
        return jnp.where(mk > 0.5, s, NEG_INF)

    o_sel = _online_tiles(ks_ref, vs_ref, qT, sel_range[0], sel_range[1], sel_bias, sel_mask, sel_tile)
    o_win = _online_tiles(kw_ref, vw_ref, qT, win_range[0], win_range[1], win_bias, lambda j, s: s, win_tile)
    g = 1.0 / (1.0 + jnp.exp(-gate_logits))
    return g[0:1] * o_cmp + g[1:2] * o_sel + g[2:3] * o_win


def _nsa_prompt_body(qT_ref, gate_ref, kc_ref, vc_ref, ks_ref, vs_ref, kw_ref, vw_ref, cb_ref, tz_ref,
                     o_ref, mask_ref, *, n_sel):
    i = pl.program_id(2)
    tq = ATT_TILE
    width = qT_ref.shape[1]
    q_pos = i * tq + lax.broadcasted_iota(jnp.int32, (1, width), 1) % tq
    masked_tile = tz_ref.shape[0] - 1

    def pair_bias(max_delta):
        def bias(jj):
            tiles = []
            for j in (KEY_PAIR * jj + t for t in range(KEY_PAIR)):
                idx = jnp.where(j > i, masked_tile, jnp.minimum(i - j, max_delta))
                tiles.append(tz_ref[idx])
            return jnp.concatenate(tiles, axis=0)
        return bias

    first_win = jnp.maximum(i - WINDOW // ATT_TILE, 0)
    o_ref[...] = _nsa_core(
        qT_ref[...], gate_ref[...], q_pos, kc_ref, vc_ref, ks_ref, vs_ref, kw_ref, vw_ref, cb_ref[...], mask_ref,
        (0, i // KEY_PAIR + 1), pair_bias(2),
        (first_win // KEY_PAIR, i // KEY_PAIR + 1), pair_bias(masked_tile),
        tq=tq, n_sel=n_sel, sel_tile=KEY_PAIR * ATT_TILE, win_tile=KEY_PAIR * ATT_TILE)


def _bias_lanes(tbl, dist, ok):
    onehot = (t5_bucket(dist)[..., None] == jnp.arange(N_BUCKETS)).astype(jnp.float32)
    vals = jnp.dot(onehot.reshape(-1, N_BUCKETS), tbl.reshape(N_BUCKETS, -1),
                   precision=lax.Precision.HIGHEST).reshape(dist.shape + tbl.shape[1:])
    b = jnp.where(ok[..., None, None], vals, NEG_INF)
    nd = b.ndim
    b = jnp.moveaxis(b, (nd - 2, nd - 1), (0, nd - 2))
    return b.reshape(b.shape[:-2] + (b.shape[-2] * b.shape[-1],))


def nsa_prompt_attention(q, gate_logits, kcmp, vcmp, ks, vs, kw, vw, rel_bias):
    B, S, G, R, dh = q.shape
    t = ATT_TILE
    n_qt = S // t
    bf = jnp.bfloat16
    tbl = rel_bias.reshape(N_BUCKETS, G, R).astype(jnp.float32)
    qT = (q * (HEAD_DIM ** -0.5)).astype(bf).reshape(B, n_qt, t, G, R, dh)
    qT = qT.transpose(0, 3, 1, 5, 4, 2).reshape(B, G, n_qt, dh, R * t)
    gT = gate_logits.reshape(B, n_qt, t, G, R, 3).transpose(0, 3, 1, 5, 4, 2).reshape(B, G, n_qt, 3, R * t)
    n_cmp = kcmp.shape[1]
    ncp = _round_up(n_cmp, 8)
    padc = lambda x: jnp.pad(x.astype(bf).transpose(0, 2, 1, 3), ((0, 0), (0, 0), (0, ncp - n_cmp), (0, 0)))
    tr = lambda x: x.astype(bf).transpose(0, 2, 1, 3)
    qp = jnp.arange(S).reshape(n_qt, 1, t)
    c_idx = jnp.arange(ncp).reshape(1, ncp, 1)
    dist = qp - (c_idx * CMP_STRIDE + CMP_LEN - 1)
    cb = _bias_lanes(tbl, dist, (dist >= 0) & (c_idx < n_cmp))
    n_delta = WINDOW // t + 2
    d = (jnp.arange(n_delta).reshape(-1, 1, 1) * t + jnp.arange(t).reshape(1, 1, t)
         - jnp.arange(t).reshape(1, t, 1))
    tz = _bias_lanes(tbl, d, (d >= 0) & (d < WINDOW))
    body = functools.partial(_nsa_prompt_body, n_sel=min(SEL_TOPK, S // SEL_BLOCK))
    kv_spec = pl.BlockSpec((None, None, S, dh), lambda b, g, i: (b, g, 0, 0))
    cmp_spec = pl.BlockSpec((None, None, ncp, dh), lambda b, g, i: (b, g, 0, 0))
    oT = pl.pallas_call(
        body,
        grid=(B, G, n_qt),
        in_specs=[pl.BlockSpec((None, None, None, dh, R * t), lambda b, g, i: (b, g, i, 0, 0)),
                  pl.BlockSpec((None, None, None, 3, R * t), lambda b, g, i: (b, g, i, 0, 0)),
                  cmp_spec, cmp_spec, kv_spec, kv_spec, kv_spec, kv_spec,
                  pl.BlockSpec((None, None, ncp, R * t), lambda b, g, i: (g, i, 0, 0)),
                  pl.BlockSpec((None,) + tz.shape[1:], lambda b, g, i: (g, 0, 0, 0))],
        out_specs=pl.BlockSpec((None, None, None, dh, R * t), lambda b, g, i: (b, g, i, 0, 0)),
        out_shape=jax.ShapeDtypeStruct((B, G, n_qt, dh, R * t), jnp.float32),
        scratch_shapes=[pltpu.VMEM((S, R * t), jnp.float32)],
        compiler_params=pltpu.CompilerParams(dimension_semantics=("parallel", "parallel", "arbitrary"),
                                             vmem_limit_bytes=VMEM_LIMIT_BYTES),
        name="nsa_prompt_attention",
    )(qT, gT, padc(kcmp), padc(vcmp), tr(ks), tr(vs), tr(kw), tr(vw), cb, tz)
    o = oT.reshape(B, G, n_qt, dh, R, t).transpose(0, 2, 5, 1, 4, 3)
    return o.reshape(B, S, G * R * dh)


SAMPLE_PAGES_PER_STEP = 4


def _softmax_lanes(s):
    m = jnp.max(s, axis=1, keepdims=True)
    e = jnp.where(s > MASKED_BELOW, jnp.exp(s - m), 0.0)
    return e, jnp.sum(e, axis=1, keepdims=True)


def _attend_lanes(qbd, k2, v2, bias, mask=None):
    s = jnp.dot(qbd, k2, preferred_element_type=jnp.float32) + bias
    if mask is not None:
        s = jnp.where(mask > 0.5, s, NEG_INF)
    e, l = _softmax_lanes(s)
    o = lax.dot_general(e.astype(jnp.bfloat16), v2, (((1,), (1,)), ((), ())),
                        preferred_element_type=jnp.float32)
    return o / jnp.maximum(l, TINY)


def _nsa_sample_body(pt_ref, qbd_ref, gate_ref, kc_ref, vc_ref, new_ref, wk_ref, wv_ref, cb_ref, sb_ref, wb_ref,
                     *rest, n_sel, tq, past, n_blocks):
    pps = SAMPLE_PAGES_PER_STEP
    k_pages, v_pages = rest[:pps], rest[pps:2 * pps]
    o_ref, ks_ref, vs_ref = rest[2 * pps:]
    step = pl.program_id(1)
    rows = ks_ref.shape[0]
    width = qbd_ref.shape[0]
    for i in range(pps):
        cols = pl.ds(pl.multiple_of((step * pps + i) * PAGE_SIZE, PAGE_SIZE), PAGE_SIZE)
        ks_ref[:, cols] = k_pages[i][...].reshape(rows, PAGE_SIZE).astype(jnp.bfloat16)
        vs_ref[:, cols] = v_pages[i][...].reshape(rows, PAGE_SIZE).astype(jnp.bfloat16)

    @pl.when(step == pl.num_programs(1) - 1)
    def _():
        qbd = qbd_ref[...]
        n_keys = ks_ref.shape[1]
        ks_ref[:, past:n_keys] = new_ref[0]
        vs_ref[:, past:n_keys] = new_ref[1]
        s_c = jnp.dot(qbd, kc_ref[...], preferred_element_type=jnp.float32) + cb_ref[...]
        e_c, l_c = _softmax_lanes(s_c)
        p_cmp = e_c / jnp.maximum(l_c, TINY)
        o_cmp = lax.dot_general(p_cmp.astype(jnp.bfloat16), vc_ref[...], (((1,), (1,)), ((), ())),
                                preferred_element_type=jnp.float32)
        ncp = p_cmp.shape[1]
        cs = lax.broadcasted_iota(jnp.int32, (ncp, LANES), 0) * CMP_STRIDE
        nb = lax.broadcasted_iota(jnp.int32, (ncp, LANES), 1) * SEL_BLOCK
        overlap_t = jnp.where((cs < nb + SEL_BLOCK) & (cs + CMP_LEN > nb), 1.0, 0.0).astype(jnp.bfloat16)
        ra = lax.broadcasted_iota(jnp.int32, (width, width), 0)
        rb = lax.broadcasted_iota(jnp.int32, (width, width), 1)
        per_group = NSA_GROUP * tq
        same_query = jnp.where((ra // per_group == rb // per_group) & (ra % tq == rb % tq), 1.0, 0.0)
        imp = _dot_f32_by_01(same_query.astype(jnp.bfloat16), _x_dot_01(p_cmp, overlap_t))
        blk = lax.broadcasted_iota(jnp.int32, (width, LANES), 1)
        cur = (past + lax.broadcasted_iota(jnp.int32, (width, LANES), 0) % tq) // SEL_BLOCK
        forced = (blk == 0) | (blk == cur) | (blk == cur - 1)
        score = jnp.where(blk <= cur, jnp.where(forced, FORCE_SCORE, imp), -1.0)
        score = jnp.where(blk < n_blocks, score, -2.0)
        score_t = score.T
        bt = lax.broadcasted_iota(jnp.int32, (LANES, width), 0)
        rank = jnp.zeros((LANES, width), jnp.float32)
        for mrow in range(n_blocks):
            row = score_t[mrow:mrow + 1, :]
            ahead = (row > score_t) | ((row == score_t) & (bt > mrow))
            rank = rank + jnp.where(ahead, 1.0, 0.0)
        sel = jnp.where(rank < n_sel, 1.0, 0.0).T.astype(jnp.bfloat16)
        kb = lax.broadcasted_iota(jnp.int32, (LANES, n_keys), 1) // SEL_BLOCK
        nn = lax.broadcasted_iota(jnp.int32, (LANES, n_keys), 0)
        expand = jnp.where(kb == nn, 1.0, 0.0).astype(jnp.bfloat16)
        mask = jnp.dot(sel, expand, preferred_element_type=jnp.float32)
        o_sel = _attend_lanes(qbd, ks_ref[...], vs_ref[...], sb_ref[...], mask)
        window = lambda ref, new: jnp.concatenate(
            [ref[...].reshape(rows, ref.shape[-1]).astype(jnp.bfloat16), new], axis=1)
        o_win = _attend_lanes(qbd, window(wk_ref, new_ref[2]), window(wv_ref, new_ref[3]), wb_ref[...])
        g = 1.0 / (1.0 + jnp.exp(-gate_ref[...]))
        o = g[:, 0:1] * o_cmp + g[:, 1:2] * o_sel + g[:, 2:3] * o_win
        r = lax.broadcasted_iota(jnp.int32, (width, rows), 0) // per_group
        c = lax.broadcasted_iota(jnp.int32, (width, rows), 1) // HEAD_DIM
        o = jnp.where(r == c, o, 0.0)
        fa = lax.broadcasted_iota(jnp.int32, (rows, LANES), 0) % HEAD_DIM
        fb = lax.broadcasted_iota(jnp.int32, (rows, LANES), 1)
        fold = jnp.where(fa == fb, 1.0, 0.0).astype(jnp.bfloat16)
        o_ref[...] = _x_dot_01(o, fold)[:, :HEAD_DIM]


def _rows_table(table):
    g, k, w = table.shape
    return table.transpose(0, 2, 1).reshape(g * w, k)


def nsa_sample_attention(q, gate_logits, kcmp, vcmp, new_kv, sk_pool, sv_pool, wk_buf, wv_buf, page_table, rel_bias):
    DB, T, G, R, dh = q.shape
    bf = jnp.bfloat16
    width = G * R * T
    rows = G * dh
    n_pages = page_table.shape[1]
    past = n_pages * PAGE_SIZE
    n_keys = past + PAGE_SIZE
    n_blocks = _round_up(past + T, SEL_BLOCK) // SEL_BLOCK
    tbl = rel_bias.reshape(N_BUCKETS, G, R).astype(jnp.float32)
    q_pos = past + jnp.arange(T)
    qg = (q * (HEAD_DIM ** -0.5)).transpose(0, 2, 3, 1, 4)
    same_group = jnp.eye(G, dtype=qg.dtype).reshape(1, G, 1, 1, G, 1)
    qbd = (qg[:, :, :, :, None, :] * same_group).astype(bf).reshape(DB, width, rows)
    gates = jnp.pad(gate_logits.transpose(0, 2, 3, 1, 4).reshape(DB, width, 3), ((0, 0), (0, 0), (0, 5)))
    n_cmp = kcmp.shape[1]
    ncp = _round_up(n_cmp, 8)
    cmp_t = lambda x: jnp.pad(x.astype(bf).transpose(0, 2, 3, 1), ((0, 0), (0, 0), (0, 0), (0, ncp - n_cmp))
                              ).reshape(DB, rows, ncp)
    c_idx = jnp.arange(ncp).reshape(ncp, 1)
    dist = q_pos.reshape(1, T) - (c_idx * CMP_STRIDE + CMP_LEN - 1)
    cb = _rows_table(_bias_lanes(tbl, dist, (dist >= 0) & (c_idx < n_cmp)))
    k_idx = jnp.arange(n_keys).reshape(-1, 1)
    dist = q_pos.reshape(1, T) - k_idx
    sb = _rows_table(_bias_lanes(tbl, dist, dist >= 0))
    wbuf = wk_buf.shape[1]
    n_win = wbuf + PAGE_SIZE
    w_idx = jnp.arange(n_win).reshape(-1, 1)
    w_pos = past - wbuf + w_idx
    dist = q_pos.reshape(1, T) - w_pos
    wb = _rows_table(_bias_lanes(tbl, dist, (dist >= 0) & (dist < WINDOW) & (w_pos >= 0) & (w_idx < wbuf + T)))
    new_t = lambda x: jnp.pad(x.astype(bf).transpose(0, 2, 3, 1),
                              ((0, 0), (0, 0), (0, 0), (0, PAGE_SIZE - T))).reshape(DB, rows, PAGE_SIZE)
    new = jnp.stack([new_t(x) for x in new_kv], axis=1)
    native = lambda x: x.transpose(0, 2, 3, 1)
    pps = SAMPLE_PAGES_PER_STEP
    body = functools.partial(_nsa_sample_body, n_sel=min(SEL_TOPK, n_blocks), tq=T, past=past, n_blocks=n_blocks)
    per_b = lambda shape: pl.BlockSpec((None,) + shape, lambda b, s, pt: (b,) + (0,) * len(shape))
    const = lambda x: pl.BlockSpec(x.shape, lambda b, s, pt: (0,) * x.ndim)
    page = lambda i: pl.BlockSpec((None, G, dh, PAGE_SIZE), lambda b, s, pt: (pt[b, s * pps + i], 0, 0, 0))
    sk_t, sv_t = native(sk_pool), native(sv_pool)
    o = pl.pallas_call(
        body,
        grid_spec=pltpu.PrefetchScalarGridSpec(
            num_scalar_prefetch=1,
            grid=(DB, n_pages // pps),
            in_specs=[per_b((width, rows)), per_b((width, 8)), per_b((rows, ncp)), per_b((rows, ncp)),
                      per_b((4, rows, PAGE_SIZE)), per_b((G, dh, wbuf)), per_b((G, dh, wbuf)),
                      const(cb), const(sb), const(wb)]
                     + [page(i) for i in range(pps)] * 2,
            out_specs=per_b((width, dh)),
            scratch_shapes=[pltpu.VMEM((rows, n_keys), bf), pltpu.VMEM((rows, n_keys), bf)]),
        out_shape=jax.ShapeDtypeStruct((DB, width, dh), jnp.float32),
        compiler_params=pltpu.CompilerParams(dimension_semantics=("parallel", "arbitrary"),
                                             vmem_limit_bytes=VMEM_LIMIT_BYTES),
        name="nsa_sample_attention",
    )(page_table, qbd, gates, cmp_t(kcmp), cmp_t(vcmp), new, native(wk_buf), native(wv_buf), cb, sb, wb,
      *([sk_t] * pps), *([sv_t] * pps))
    return o.reshape(DB, G, R, T, dh).transpose(0, 3, 1, 2, 4).reshape(DB, T, G * R * dh)


FOX_Q_TILE = 512
FOX_K_TILE = 256


def _softmax_step(s, v, carry, masked):
    m, l, acc = carry
    m_new = jnp.maximum(m, jnp.max(s, axis=0, keepdims=True))
    alpha = jnp.exp(m - m_new)
    e = jnp.exp(s - m_new)
    if masked:
        e = jnp.where(s > MASKED_BELOW, e, 0.0)
    l = alpha * l + jnp.sum(e, axis=0, keepdims=True)
    pv = lax.dot_general(v, e.astype(jnp.bfloat16), (((0,), (0,)), ((), ())),
                         preferred_element_type=jnp.float32)
    return m_new, l, alpha * acc + pv


def _fox_prompt_body(qT_ref, k_ref, v_ref, o_ref):
    i = pl.program_id(2)
    qT = qT_ref[...]
    width = qT.shape[1]

    def tile(j):
        rows = pl.ds(pl.multiple_of(j * FOX_K_TILE, FOX_K_TILE), FOX_K_TILE)
        return jnp.dot(k_ref[rows, :], qT, preferred_element_type=jnp.float32), v_ref[rows, :]

    def full_step(j, carry):
        s, v = tile(j)
        return _softmax_step(s, v, carry, masked=False)

    carry = (jnp.full((1, width), NEG_INF, jnp.float32), jnp.zeros((1, width), jnp.float32),
             jnp.zeros((HEAD_DIM, width), jnp.float32))
    ratio = FOX_Q_TILE // FOX_K_TILE
    carry = lax.fori_loop(0, i * ratio, full_step, carry)
    q_pos = i * FOX_Q_TILE + lax.broadcasted_iota(jnp.int32, (FOX_K_TILE, width), 1)
    for dj in range(ratio):
        j = i * ratio + dj
        s, v = tile(j)
        k_pos = j * FOX_K_TILE + lax.broadcasted_iota(jnp.int32, (FOX_K_TILE, width), 0)
        s = jnp.where(k_pos <= q_pos, s, NEG_INF)
        carry = _softmax_step(s, v, carry, masked=True)
    m, l, acc = carry
    o_ref[...] = acc / jnp.maximum(l, TINY)


def _augment(c):
    hi, mid, lo = _split3_outside_kernel(c)
    one = jnp.ones_like(hi)
    return jnp.stack([hi, mid, lo, one, one, one], axis=-1)


def fox_prompt_attention(q, k, v, logf):
    B, S, H, dh = q.shape
    bf = jnp.bfloat16
    c = jnp.cumsum(logf, axis=1).transpose(0, 2, 1)
    aug = _augment(c)
    pad = jnp.zeros((B, H, S, LANES - dh - 6), bf)
    kb = k.astype(bf).transpose(0, 2, 1, 3)
    k_aug = jnp.concatenate([kb, -aug[..., :3], aug[..., 3:], pad], axis=-1)
    qb = (q * (dh ** -0.5)).astype(bf).transpose(0, 2, 1, 3)
    q_aug = jnp.concatenate([qb, aug[..., 3:], aug[..., :3], pad], axis=-1)
    qT = q_aug.transpose(0, 1, 3, 2)
    vb = v.astype(bf).transpose(0, 2, 1, 3)
    oT = pl.pallas_call(
        _fox_prompt_body,
        grid=(B, H, S // FOX_Q_TILE),
        in_specs=[pl.BlockSpec((None, None, LANES, FOX_Q_TILE), lambda b, h, i: (b, h, 0, i)),
                  pl.BlockSpec((None, None, S, LANES), lambda b, h, i: (b, h, 0, 0)),
                  pl.BlockSpec((None, None, S, dh), lambda b, h, i: (b, h, 0, 0))],
        out_specs=pl.BlockSpec((None, None, dh, FOX_Q_TILE), lambda b, h, i: (b, h, 0, i)),
        out_shape=jax.ShapeDtypeStruct((B, H, dh, S), jnp.float32),
        compiler_params=pltpu.CompilerParams(dimension_semantics=("parallel", "parallel", "arbitrary"),
                                             vmem_limit_bytes=VMEM_LIMIT_BYTES),
        name="fox_prompt_attention",
    )(qT, k_aug, vb)
    return oT.transpose(0, 3, 1, 2).reshape(B, S, H * dh)


def _fox_sample_body(pt_ref, qbd_ref, kp_ref, vp_ref, bp_ref, kn_ref, vn_ref, bn_ref, o_ref,
                     m_ref, l_ref, acc_ref, *, n_q):
    p = pl.program_id(1)
    width = qbd_ref.shape[0]
    rows = kn_ref.shape[0]

    @pl.when(p == 0)
    def _():
        m_ref[...] = jnp.full(m_ref.shape, NEG_INF, jnp.float32)
        l_ref[...] = jnp.zeros(l_ref.shape, jnp.float32)
        acc_ref[...] = jnp.zeros(acc_ref.shape, jnp.float32)

    def attend(k2, v2, bias_t):
        s_t = jnp.dot(qbd_ref[...], k2, preferred_element_type=jnp.float32).T + bias_t
        m = m_ref[...]
        m_new = jnp.maximum(m, jnp.max(s_t, axis=0, keepdims=True))
        alpha = jnp.exp(m - m_new)
        e = jnp.where(s_t > MASKED_BELOW, jnp.exp(s_t - m_new), 0.0)
        m_ref[...] = m_new
        l_ref[...] = alpha * l_ref[...] + jnp.sum(e, axis=0, keepdims=True)
        acc_ref[...] = alpha * acc_ref[...] + jnp.dot(v2, e.astype(jnp.bfloat16),
                                                      preferred_element_type=jnp.float32)

    page = lambda ref: ref[...].reshape(rows, PAGE_SIZE).astype(jnp.bfloat16)
    attend(page(kp_ref), page(vp_ref), bp_ref[...])

    @pl.when(p == pl.num_programs(1) - 1)
    def _():
        attend(kn_ref[...], vn_ref[...], bn_ref[...])
        out = acc_ref[...] / jnp.maximum(l_ref[...], TINY)
        r = lax.broadcasted_iota(jnp.int32, (rows, width), 0) // HEAD_DIM
        c = lax.broadcasted_iota(jnp.int32, (rows, width), 1) // n_q
        out = jnp.where(r == c, out, 0.0)
        la = lax.broadcasted_iota(jnp.int32, (width, width), 0) % n_q
        lb = lax.broadcasted_iota(jnp.int32, (width, width), 1)
        gather_q = jnp.where(la == lb, 1.0, 0.0).astype(jnp.bfloat16)
        o_ref[...] = _x_dot_01(out, gather_q)[:, :n_q]


def fox_sample_attention(q, k, v, logf, k_pool, v_pool, f_pool, page_table):
    DB, T, H, dh = q.shape
    n_pages = page_table.shape[1]
    past = n_pages * PAGE_SIZE
    bf = jnp.bfloat16
    width = H * T
    rows = H * dh
    f_all = jnp.concatenate([f_pool[page_table].reshape(DB, past, H), logf], axis=1)
    c = jnp.cumsum(f_all, axis=1)
    c_q = c[:, past:].transpose(0, 2, 1).reshape(DB, 1, 1, width)
    c_k = jnp.repeat(c[:, :past].reshape(DB, n_pages, PAGE_SIZE, H), T, axis=-1)
    bias_past = c_q - c_k
    tok = jnp.arange(PAGE_SIZE).reshape(1, PAGE_SIZE, 1)
    qi = (jnp.arange(width) % T).reshape(1, 1, width)
    c_new = jnp.pad(jnp.repeat(c[:, past:], T, axis=-1), ((0, 0), (0, PAGE_SIZE - T), (0, 0)))
    bias_new = jnp.where((tok <= qi) & (tok < T), c_q[:, 0] - c_new, NEG_INF)
    qh = (q * (dh ** -0.5)).transpose(0, 2, 1, 3)
    same_head = jnp.eye(H, dtype=qh.dtype).reshape(1, H, 1, H, 1)
    qbd = (qh[:, :, :, None, :] * same_head).astype(bf).reshape(DB, width, rows)
    new_t = lambda x: jnp.pad(x.astype(bf).transpose(0, 2, 3, 1),
                              ((0, 0), (0, 0), (0, 0), (0, PAGE_SIZE - T))).reshape(DB, rows, PAGE_SIZE)
    pool_t = lambda x: x.transpose(0, 2, 3, 1)
    body = functools.partial(_fox_sample_body, n_q=T)
    page_spec = pl.BlockSpec((None, H, dh, PAGE_SIZE), lambda b, p, pt: (pt[b, p], 0, 0, 0))
    per_b = lambda shape: pl.BlockSpec((None,) + shape, lambda b, p, pt: (b,) + (0,) * len(shape))
    o = pl.pallas_call(
        body,
        grid_spec=pltpu.PrefetchScalarGridSpec(
            num_scalar_prefetch=1,
            grid=(DB, n_pages),
            in_specs=[per_b((width, rows)), page_spec, page_spec,
                      pl.BlockSpec((None, None, PAGE_SIZE, width), lambda b, p, pt: (b, p, 0, 0)),
                      per_b((rows, PAGE_SIZE)), per_b((rows, PAGE_SIZE)), per_b((PAGE_SIZE, width))],
            out_specs=per_b((rows, T)),
            scratch_shapes=[pltpu.VMEM((1, width), jnp.float32), pltpu.VMEM((1, width), jnp.float32),
                            pltpu.VMEM((rows, width), jnp.float32)]),
        out_shape=jax.ShapeDtypeStruct((DB, rows, T), jnp.float32),
        compiler_params=pltpu.CompilerParams(dimension_semantics=("parallel", "arbitrary"),
                                             vmem_limit_bytes=VMEM_LIMIT_BYTES),
        name="fox_sample_attention",
    )(page_table, qbd, pool_t(k_pool), pool_t(v_pool), bias_past, new_t(k), new_t(v), bias_new)
    return o.reshape(DB, H, dh, T).transpose(0, 3, 1, 2).reshape(DB, T, H * dh)


def compress(x, pe, w1, w2):
    B, T, G, dh = x.shape
    r = CMP_LEN // CMP_STRIDE
    n_chunks = T // CMP_STRIDE
    nc = n_chunks - r + 1
    ch = x.reshape(B, n_chunks, CMP_STRIDE, G, dh)
    blk = jnp.concatenate([ch[:, m:m + nc] for m in range(r)], axis=2)
    blk = blk + pe[:, None, :]
    flat = blk.transpose(0, 1, 3, 2, 4).reshape(B, nc, G, CMP_LEN * dh)
    return jax.nn.gelu(flat @ w1, approximate=False) @ w2


def nsa_split(proj):
    B, T, _ = proj.shape
    nq = NSA_HEADS * HEAD_DIM
    nkv = NSA_KV_HEADS * HEAD_DIM
    q = proj[..., :nq].reshape(B, T, NSA_KV_HEADS, NSA_GROUP, HEAD_DIM)
    kv = proj[..., nq:nq + 6 * nkv].reshape(B, T, 6, NSA_KV_HEADS, HEAD_DIM)
    gl = proj[..., nq + 6 * nkv:].reshape(B, T, NSA_KV_HEADS, NSA_GROUP, 3)
    return q, gl, [kv[:, :, s] for s in range(6)]


def nsa_prompt(proj, cw, rel_bias):
    S = proj.shape[1]
    pe_k, w1_k, w2_k, pe_v, w1_v, w2_v = cw
    q, gl, (kc, vc, ks, vs, kw, vw) = nsa_split(proj)
    kcmp = compress(kc, pe_k, w1_k, w2_k)
    vcmp = compress(vc, pe_v, w1_v, w2_v)
    o = nsa_prompt_attention(q, gl, kcmp, vcmp, ks, vs, kw, vw, rel_bias)
    wb = min(WINDOW, S)
    return o, [kc, vc, ks, vs, kw[:, S - wb:], vw[:, S - wb:]]


def nsa_sample(proj, cw, rel_bias, ck_pool, cv_pool, sk_pool, sv_pool, wk_buf, wv_buf, page_table):
    DB, T, _ = proj.shape
    pe_k, w1_k, w2_k, pe_v, w1_v, w2_v = cw
    q, gl, (kc, vc, ks, vs, kw, vw) = nsa_split(proj)
    past = page_table.shape[1] * PAGE_SIZE
    L = past + T
    Lp = _round_up(L, SEL_BLOCK)

    def full(pool, new):
        old = pool[page_table].reshape(DB, past, NSA_KV_HEADS, HEAD_DIM)
        return jnp.pad(jnp.concatenate([old, new], axis=1), ((0, 0), (0, Lp - L), (0, 0), (0, 0)))

    kcmp = compress(full(ck_pool, kc), pe_k, w1_k, w2_k)
    vcmp = compress(full(cv_pool, vc), pe_v, w1_v, w2_v)
    kw_all = jnp.concatenate([wk_buf, kw], axis=1)
    vw_all = jnp.concatenate([wv_buf, vw], axis=1)
    o = nsa_sample_attention(q, gl, kcmp, vcmp, (ks, vs, kw, vw), sk_pool, sv_pool, wk_buf, wv_buf,
                             page_table, rel_bias)
    return o, [kc, vc, ks, vs, kw_all[:, T:], vw_all[:, T:]]


def kernel(x_prompt, x_sample, p_prompt, p_sample, cache_fox_k, cache_fox_v, cache_fox_logf,
           cache_nsa_cmp_k, cache_nsa_cmp_v, cache_nsa_sel_k, cache_nsa_sel_v,
           cache_nsa_win_k, cache_nsa_win_v, page_table,
           norm_mix, norm_ffn, norm_ple, norm_final,
           fox_w_in, fox_b_f, fox_w_out, nsa_w_in, nsa_w_out,
           cmp_pe_k, cmp_w1_k, cmp_w2_k, cmp_pe_v, cmp_w1_v, cmp_w2_v, rel_bias,
           peer_wq, peer_sub_k1, peer_sub_k2, peer_u, peer_v, ple_w_proj, ple_w_gate):
    B, S, d = x_prompt.shape
    DB, T, _ = x_sample.shape
    n_p, n_s = B * S, DB * T
    bf = jnp.bfloat16
    rows = lambda a, b: jnp.concatenate([a.reshape(n_p, -1), b.reshape(n_s, -1)], axis=0)
    x = rows(x_prompt, x_sample)
    fox_p, fox_s, nsa_p, nsa_s = [], [], [], []
    for i in range(DEPTH):
        j = i // 2
        if i % 2 == 0:
            h = rmsnorm_pallas(x, norm_mix[i], bf)
            proj = linear_pallas(h, fox_w_in[j])
            nh = FOX_HEADS * HEAD_DIM
            logf = jax.nn.log_sigmoid(proj[:, 3 * nh:] + fox_b_f[j])
            heads = lambda a, lead: a.reshape(lead + (FOX_HEADS, -1))
            qp, kp, vp = (heads(proj[:n_p, s * nh:(s + 1) * nh], (B, S)) for s in range(3))
            qs, ks_, vs_ = (heads(proj[n_p:, s * nh:(s + 1) * nh], (DB, T)) for s in range(3))
            fp, fs = logf[:n_p].reshape(B, S, FOX_HEADS), logf[n_p:].reshape(DB, T, FOX_HEADS)
            op = fox_prompt_attention(qp, kp, vp, fp)
            os_ = fox_sample_attention(qs, ks_, vs_, fs, cache_fox_k[j], cache_fox_v[j], cache_fox_logf[j],
                                       page_table)
            fox_p.append([kp, vp, fp])
            fox_s.append([ks_, vs_, fs])
            y = linear_pallas(rows(op, os_), fox_w_out[j])
        else:
            cw = (cmp_pe_k[j], cmp_w1_k[j], cmp_w2_k[j], cmp_pe_v[j], cmp_w1_v[j], cmp_w2_v[j])
            h = rmsnorm_pallas(x, norm_mix[i], bf)
            proj = linear_pallas(h, nsa_w_in[j])
            op, stp = nsa_prompt(proj[:n_p].reshape(B, S, -1), cw, rel_bias)
            os_, sts = nsa_sample(proj[n_p:].reshape(DB, T, -1), cw, rel_bias,
                                  cache_nsa_cmp_k[j], cache_nsa_cmp_v[j], cache_nsa_sel_k[j],
                                  cache_nsa_sel_v[j], cache_nsa_win_k[j], cache_nsa_win_v[j], page_table)
            nsa_p.append(stp)
            nsa_s.append(sts)
            y = linear_pallas(rows(op, os_), nsa_w_out[j])
        x = x + y
        h = rmsnorm_pallas(x, norm_ffn[i], bf)
        x = x + peer_pallas(h, peer_wq[i].T.astype(bf), peer_sub_k1[i].astype(bf), peer_sub_k2[i].astype(bf),
                            peer_u[i].astype(bf), peer_v[i].astype(bf))
        x = ple_pallas(x, rows(p_prompt[i], p_sample[i]), norm_ple[i], ple_w_proj[i], ple_w_gate[i])
    y = rmsnorm_pallas(x, norm_final, jnp.float32)
    st = lambda lst, k: jnp.stack([s[k] for s in lst])
    return (y[:n_p].reshape(B, S, d), y[n_p:].reshape(DB, T, d),
            st(fox_p, 0), st(fox_p, 1), st(fox_p, 2),
            st(fox_s, 0), st(fox_s, 1), st(fox_s, 2),
            st(nsa_p, 0), st(nsa_p, 1), st(nsa_p, 2), st(nsa_p, 3), st(nsa_p, 4), st(nsa_p, 5),
            st(nsa_s, 0), st(nsa_s, 1), st(nsa_s, 2), st(nsa_s, 3), st(nsa_s, 4), st(nsa_s, 5))
```

```python
import functools
import math

import jax
import jax.numpy as jnp
from jax import lax
from jax.experimental import pallas as pl
from jax.experimental.pallas import tpu as pltpu

D_MODEL = 1024
DEPTH = 2
PAGE_SIZE = 128
HEAD_DIM = 64
FOX_HEADS = D_MODEL // HEAD_DIM
NSA_HEADS = D_MODEL // HEAD_DIM
NSA_KV_HEADS = 4
NSA_GROUP = NSA_HEADS // NSA_KV_HEADS
CMP_LEN = 32
CMP_STRIDE = 16
SEL_BLOCK = 64
SEL_TOPK = 16
WINDOW = 512
N_BUCKETS = 32
MAX_DISTANCE = 128
PEER_HEADS = 8
PEER_TOPK = 16
N_KEYS = 128
N_EXPERTS = N_KEYS * N_KEYS
PEER_DKEY = 256
FORCE_SCORE = 1e4
RMS_EPS = 1e-6
NEG_INF = -1e30
TINY = 1e-30

VMEM_LIMIT_BYTES = 56 * 1024 * 1024
LANES = 128
ROW_TILE = 512
MAX_COL_TILE = 1024


def _round_up(x, m):
    return -(-x // m) * m


def _split3(x):
    hi = x.astype(jnp.bfloat16)
    r1 = x - hi.astype(jnp.float32)
    mid = r1.astype(jnp.bfloat16)
    lo = (r1 - mid.astype(jnp.float32)).astype(jnp.bfloat16)
    return hi, mid, lo


def _split3_outside_kernel(x):
    to_bf16 = lambda a: lax.reduce_precision(a, exponent_bits=8, mantissa_bits=7)
    hi = to_bf16(x)
    r1 = x - hi
    mid = to_bf16(r1)
    lo = to_bf16(r1 - mid)
    return hi.astype(jnp.bfloat16), mid.astype(jnp.bfloat16), lo.astype(jnp.bfloat16)


def _rms(x, g):
    return x * lax.rsqrt(jnp.mean(x * x, axis=-1, keepdims=True) + RMS_EPS) * g


def _rmsnorm_body(x_ref, g_ref, o_ref):
    o_ref[...] = _rms(x_ref[...], g_ref[...]).astype(o_ref.dtype)


def rmsnorm_pallas(x, g, out_dtype):
    n, d = x.shape
    return pl.pallas_call(
        _rmsnorm_body,
        grid=(n // ROW_TILE,),
        in_specs=[pl.BlockSpec((ROW_TILE, d), lambda i: (i, 0)),
                  pl.BlockSpec((1, d), lambda i: (0, 0))],
        out_specs=pl.BlockSpec((ROW_TILE, d), lambda i: (i, 0)),
        out_shape=jax.ShapeDtypeStruct((n, d), out_dtype),
        compiler_params=pltpu.CompilerParams(dimension_semantics=("parallel",)),
        name="rmsnorm",
    )(x, g.reshape(1, d))


def _linear_body(x_ref, w_ref, o_ref):
    o_ref[...] = jnp.dot(x_ref[...].astype(jnp.bfloat16), w_ref[...], preferred_element_type=jnp.float32)


def linear_pallas(x, w):
    n, k = x.shape
    m = w.shape[1]
    mp = _round_up(m, LANES)
    tn = max(t for t in range(LANES, MAX_COL_TILE + 1, LANES) if mp % t == 0)
    wb = jnp.pad(w.astype(jnp.bfloat16), ((0, 0), (0, mp - m)))
    return pl.pallas_call(
        _linear_body,
        grid=(n // ROW_TILE, mp // tn),
        in_specs=[pl.BlockSpec((ROW_TILE, k), lambda i, j: (i, 0)),
                  pl.BlockSpec((k, tn), lambda i, j: (0, j))],
        out_specs=pl.BlockSpec((ROW_TILE, tn), lambda i, j: (i, j)),
        out_shape=jax.ShapeDtypeStruct((n, mp), jnp.float32),
        compiler_params=pltpu.CompilerParams(dimension_semantics=("parallel", "parallel"),
                                             vmem_limit_bytes=VMEM_LIMIT_BYTES),
        name="linear",
    )(x, wb)


def _ple_body(x_ref, p_ref, g_ref, wg_ref, wp_ref, o_ref):
    x = x_ref[...]
    h = _rms(x, g_ref[...]).astype(jnp.bfloat16)
    gate = 1.0 / (1.0 + jnp.exp(-jnp.dot(h, wg_ref[...], preferred_element_type=jnp.float32)))
    proj = jnp.dot(p_ref[...].astype(jnp.bfloat16), wp_ref[...], preferred_element_type=jnp.float32)
    o_ref[...] = x + gate * proj


def ple_pallas(x, p, g, w_proj, w_gate):
    n, d = x.shape
    dp = p.shape[1]
    return pl.pallas_call(
        _ple_body,
        grid=(n // ROW_TILE,),
        in_specs=[pl.BlockSpec((ROW_TILE, d), lambda i: (i, 0)),
                  pl.BlockSpec((ROW_TILE, dp), lambda i: (i, 0)),
                  pl.BlockSpec((1, d), lambda i: (0, 0)),
                  pl.BlockSpec((d, d), lambda i: (0, 0)),
                  pl.BlockSpec((dp, d), lambda i: (0, 0))],
        out_specs=pl.BlockSpec((ROW_TILE, d), lambda i: (i, 0)),
        out_shape=jax.ShapeDtypeStruct((n, d), jnp.float32),
        compiler_params=pltpu.CompilerParams(dimension_semantics=("parallel",),
                                             vmem_limit_bytes=VMEM_LIMIT_BYTES),
        name="ple",
    )(x, p, g.reshape(1, d), w_gate.astype(jnp.bfloat16), w_proj.astype(jnp.bfloat16))


PEER_TOKEN_TILE = 512
PEER_EXPERT_TILE = 1024
SQRT_HALF = 0.7071067811865476


def _gelu_exact(x):
    return 0.5 * x * (1.0 + lax.erf(x * SQRT_HALF))


def _top_rows(x, k):
    vals = []
    for _ in range(k):
        m = jnp.max(x, axis=0, keepdims=True)
        vals.append(m)
        x = jnp.where(x == m, NEG_INF, x)
    return vals


def _peer_route_body(h_ref, wqT_ref, k1_ref, k2_ref, n1_ref, c1_ref, r2_ref, e2_ref):
    h = h_ref[...]
    half = PEER_DKEY // 2
    nt = (((1,), (1,)), ((), ()))
    for hd in range(PEER_HEADS):
        qv = lax.dot_general(wqT_ref[hd * PEER_DKEY:(hd + 1) * PEER_DKEY, :], h, nt,
                             preferred_element_type=jnp.float32)
        s1 = jnp.dot(k1_ref[hd], qv[:half].astype(jnp.bfloat16), preferred_element_type=jnp.float32)
        s2 = jnp.dot(k2_ref[hd], qv[half:].astype(jnp.bfloat16), preferred_element_type=jnp.float32)
        v1 = _top_rows(s1, PEER_TOPK)
        v2 = _top_rows(s2, PEER_TOPK)
        v2_stack = jnp.concatenate(v2, axis=0)
        blocks = []
        for p in range(PEER_TOPK):
            n_p = PEER_TOPK // (p + 1)
            rows = -(-n_p // 8) * 8
            blk = v1[p] + v2_stack[:rows]
            if n_p < rows:
                r = lax.broadcasted_iota(jnp.int32, blk.shape, 0)
                blk = jnp.where(r < n_p, blk, NEG_INF)
            blocks.append(blk)
        c = _top_rows(jnp.concatenate(blocks, axis=0), PEER_TOPK)
        z = jnp.ones_like(c[0])
        for kk in range(1, PEER_TOPK):
            z = z + jnp.exp(c[kk] - c[0])
        tau = c[PEER_TOPK - 1]
        n1 = jnp.zeros_like(s1)
        r2 = jnp.zeros_like(s2)
        for q in range(PEER_TOPK):
            n1 = n1 + jnp.where(s1 + v2[q] >= tau, 1.0, 0.0)
            r2 = r2 + jnp.where(v2[q] > s2, 1.0, 0.0)
        n1_ref[hd] = n1
        c1_ref[hd] = jnp.exp(s1 - v1[0]) / z
        r2_ref[hd] = r2.astype(jnp.bfloat16)
        e2_ref[hd] = jnp.exp(s2 - v2[0]).astype(jnp.bfloat16)


def _peer_dense_body(h_ref, u_ref, v_ref, n1_ref, c1_ref, r2_ref, e2_ref, y_ref):
    j = pl.program_id(1)

    @pl.when(j == 0)
    def _():
        y_ref[...] = jnp.zeros_like(y_ref)

    h = h_ref[...]
    act = lax.dot_general(u_ref[...], h, (((1,), (1,)), ((), ())),
                          preferred_element_type=jnp.float32)
    groups = PEER_EXPERT_TILE // N_KEYS
    parts = []
    for aa in range(groups):
        w = None
        for hd in range(PEER_HEADS):
            picked = r2_ref[hd] < n1_ref[hd, aa:aa + 1, :].astype(jnp.bfloat16)
            term = jnp.where(picked, e2_ref[hd], 0.0) * c1_ref[hd, aa:aa + 1, :].astype(jnp.bfloat16)
            w = term if w is None else w + term
        g = _gelu_exact(act[aa * N_KEYS:(aa + 1) * N_KEYS]).astype(jnp.bfloat16)
        parts.append(w * g)
    p = jnp.concatenate(parts, axis=0)
    y_ref[...] += lax.dot_general(p, v_ref[...], (((0,), (0,)), ((), ())),
                                  preferred_element_type=jnp.float32)


def peer_pallas(h, wqT, k1, k2, u, v):
    n, d = h.shape
    t = PEER_TOKEN_TILE
    nt = n // t
    hk = (PEER_HEADS, N_KEYS)
    route = lambda dt: jax.ShapeDtypeStruct(hk + (n,), dt)
    route_spec = pl.BlockSpec(hk + (t,), lambda i: (0, 0, i))
    n1, c1, r2, e2 = pl.pallas_call(
        _peer_route_body,
        grid=(nt,),
        in_specs=[pl.BlockSpec((t, d), lambda i: (i, 0)),
                  pl.BlockSpec(wqT.shape, lambda i: (0, 0)),
                  pl.BlockSpec(k1.shape, lambda i: (0, 0, 0)),
                  pl.BlockSpec(k2.shape, lambda i: (0, 0, 0))],
        out_specs=[route_spec] * 4,
        out_shape=[route(jnp.float32), route(jnp.float32), route(jnp.bfloat16), route(jnp.bfloat16)],
        compiler_params=pltpu.CompilerParams(dimension_semantics=("parallel",),
                                             vmem_limit_bytes=VMEM_LIMIT_BYTES),
        name="peer_route",
    )(h, wqT, k1, k2)

    e = PEER_EXPERT_TILE
    groups = e // N_KEYS
    row_spec = pl.BlockSpec((PEER_HEADS, groups, t), lambda i, j: (0, j, i))
    full_spec = pl.BlockSpec(hk + (t,), lambda i, j: (0, 0, i))
    return pl.pallas_call(
        _peer_dense_body,
        grid=(nt, N_EXPERTS // e),
        in_specs=[pl.BlockSpec((t, d), lambda i, j: (i, 0)),
                  pl.BlockSpec((e, d), lambda i, j: (j, 0)),
                  pl.BlockSpec((e, d), lambda i, j: (j, 0)),
                  row_spec, row_spec, full_spec, full_spec],
        out_specs=pl.BlockSpec((t, d), lambda i, j: (i, 0)),
        out_shape=jax.ShapeDtypeStruct((n, d), jnp.float32),
        compiler_params=pltpu.CompilerParams(dimension_semantics=("parallel", "arbitrary"),
                                             vmem_limit_bytes=VMEM_LIMIT_BYTES),
        name="peer_dense",
    )(h, u, v, n1, c1, r2, e2)


ATT_TILE = 128
MASKED_BELOW = -0.5e30
KEY_PAIR = 2


def t5_bucket(dist):
    n = jnp.maximum(dist, 0)
    max_exact = N_BUCKETS // 2
    nf = jnp.maximum(n, max_exact).astype(jnp.float32)
    large = max_exact + (jnp.log(nf / max_exact) / math.log(MAX_DISTANCE / max_exact)
                         * (N_BUCKETS - max_exact)).astype(jnp.int32)
    large = jnp.minimum(large, N_BUCKETS - 1)
    return jnp.where(n < max_exact, n, large)


def _softmax_cols(s):
    m = jnp.max(s, axis=0, keepdims=True)
    e = jnp.where(s > MASKED_BELOW, jnp.exp(s - m), 0.0)
    l = jnp.sum(e, axis=0, keepdims=True)
    return e / jnp.maximum(l, TINY)


def _dot_f32_by_01(mat01, x):
    return sum(jnp.dot(mat01, part, preferred_element_type=jnp.float32) for part in _split3(x))


def _x_dot_01(x, mat01):
    return sum(jnp.dot(part, mat01, preferred_element_type=jnp.float32) for part in _split3(x))


def _online_tiles(k_ref, v_ref, qT, lo, hi, bias_of, mask_of, tile):
    width = qT.shape[1]
    tn = (((0,), (0,)), ((), ()))

    def step(j, carry):
        m, l, acc = carry
        rows = pl.ds(pl.multiple_of(j * tile, tile), tile)
        s = jnp.dot(k_ref[rows, :], qT, preferred_element_type=jnp.float32) + bias_of(j)
        s = mask_of(j, s)
        m_new = jnp.maximum(m, jnp.max(s, axis=0, keepdims=True))
        alpha = jnp.exp(m - m_new)
        e = jnp.where(s > MASKED_BELOW, jnp.exp(s - m_new), 0.0)
        l = alpha * l + jnp.sum(e, axis=0, keepdims=True)
        pv = lax.dot_general(v_ref[rows, :], e.astype(jnp.bfloat16), tn,
                             preferred_element_type=jnp.float32)
        return m_new, l, alpha * acc + pv

    init = (jnp.full((1, width), NEG_INF, jnp.float32), jnp.zeros((1, width), jnp.float32),
            jnp.zeros((HEAD_DIM, width), jnp.float32))
    m, l, acc = lax.fori_loop(lo, hi, step, init)
    return acc / jnp.maximum(l, TINY)


def _nsa_core(qT, gate_logits, q_pos, kc_ref, vc_ref, ks_ref, vs_ref, kw_ref, vw_ref, cmp_bias, mask_ref,
              sel_range, sel_bias, win_range, win_bias, *, tq, n_sel, sel_tile, win_tile):
    width = qT.shape[1]
    tn = (((0,), (0,)), ((), ()))
    p_cmp = _softmax_cols(jnp.dot(kc_ref[...], qT, preferred_element_type=jnp.float32) + cmp_bias)
    o_cmp = lax.dot_general(vc_ref[...], p_cmp.astype(jnp.bfloat16), tn, preferred_element_type=jnp.float32)
    ncp = p_cmp.shape[0]
    n_keys = mask_ref.shape[0]
    n_blocks = n_keys // SEL_BLOCK
    nb = lax.broadcasted_iota(jnp.int32, (n_blocks, ncp), 0) * SEL_BLOCK
    cs = lax.broadcasted_iota(jnp.int32, (n_blocks, ncp), 1) * CMP_STRIDE
    overlap = jnp.where((cs < nb + SEL_BLOCK) & (cs + CMP_LEN > nb), 1.0, 0.0).astype(jnp.bfloat16)
    la = lax.broadcasted_iota(jnp.int32, (width, width), 0)
    lb = lax.broadcasted_iota(jnp.int32, (width, width), 1)
    same_query = jnp.where((la % tq) == (lb % tq), 1.0, 0.0).astype(jnp.bfloat16)
    imp = _x_dot_01(_dot_f32_by_01(overlap, p_cmp), same_query)
    blk = lax.broadcasted_iota(jnp.int32, (n_blocks, width), 0)
    cur = q_pos // SEL_BLOCK
    forced = (blk == 0) | (blk == cur) | (blk == cur - 1)
    score = jnp.where(blk <= cur, jnp.where(forced, FORCE_SCORE, imp), -1.0)
    rank = jnp.zeros((n_blocks, width), jnp.float32)
    for mrow in range(n_blocks):
        row = score[mrow:mrow + 1, :]
        ahead = (row > score) | ((row == score) & (blk > mrow))
        rank = rank + jnp.where(ahead, 1.0, 0.0)
    sel = jnp.where(rank < n_sel, 1.0, 0.0).astype(jnp.bfloat16)
    kb = lax.broadcasted_iota(jnp.int32, (n_keys, n_blocks), 0) // SEL_BLOCK
    nn = lax.broadcasted_iota(jnp.int32, (n_keys, n_blocks), 1)
    expand = jnp.where(kb == nn, 1.0, 0.0).astype(jnp.bfloat16)
    mask_ref[...] = jnp.dot(expand, sel, preferred_element_type=jnp.float32)

    def sel_mask(j, s):
        mk = mask_ref[pl.ds(pl.multiple_of(j * sel_tile, sel_tile), sel_tile), :]
        return jnp.where(mk > 0.5, s, NEG_INF)

    o_sel = _online_tiles(ks_ref, vs_ref, qT, sel_range[0], sel_range[1], sel_bias, sel_mask, sel_tile)
    o_win = _online_tiles(kw_ref, vw_ref, qT, win_range[0], win_range[1], win_bias, lambda j, s: s, win_tile)
    g = 1.0 / (1.0 + jnp.exp(-gate_logits))
    return g[0:1] * o_cmp + g[1:2] * o_sel + g[2:3] * o_win


def _nsa_prompt_body(qT_ref, gate_ref, kc_ref, vc_ref, ks_ref, vs_ref, kw_ref, vw_ref, cb_ref, tz_ref,
                     o_ref, mask_ref, *, n_sel):
    i = pl.program_id(2)
    tq = ATT_TILE
    width = qT_ref.shape[1]
    q_pos = i * tq + lax.broadcasted_iota(jnp.int32, (1, width), 1) % tq
    masked_tile = tz_ref.shape[0] - 1

    def pair_bias(max_delta):
        def bias(jj):
            tiles = []
            for j in (KEY_PAIR * jj + t for t in range(KEY_PAIR)):
                idx = jnp.where(j > i, masked_tile, jnp.minimum(i - j, max_delta))
                tiles.append(tz_ref[idx])
            return jnp.concatenate(tiles, axis=0)
        return bias

    first_win = jnp.maximum(i - WINDOW // ATT_TILE, 0)
    o_ref[...] = _nsa_core(
        qT_ref[...], gate_ref[...], q_pos, kc_ref, vc_ref, ks_ref, vs_ref, kw_ref, vw_ref, cb_ref[...], mask_ref,
        (0, i // KEY_PAIR + 1), pair_bias(2),
        (first_win // KEY_PAIR, i // KEY_PAIR + 1), pair_bias(masked_tile),
        tq=tq, n_sel=n_sel, sel_tile=KEY_PAIR * ATT_TILE, win_tile=KEY_PAIR * ATT_TILE)


def _bias_lanes(tbl, dist, ok):
    onehot = (t5_bucket(dist)[..., None] == jnp.arange(N_BUCKETS)).astype(jnp.float32)
    vals = jnp.dot(onehot.reshape(-1, N_BUCKETS), tbl.reshape(N_BUCKETS, -1),
                   precision=lax.Precision.HIGHEST).reshape(dist.shape + tbl.shape[1:])
    b = jnp.where(ok[..., None, None], vals, NEG_INF)
    nd = b.ndim
    b = jnp.moveaxis(b, (nd - 2, nd - 1), (0, nd - 2))
    return b.reshape(b.shape[:-2] + (b.shape[-2] * b.shape[-1],))


def nsa_prompt_attention(q, gate_logits, kcmp, vcmp, ks, vs, kw, vw, rel_bias):
    B, S, G, R, dh = q.shape
    t = ATT_TILE
    n_qt = S // t
    bf = jnp.bfloat16
    tbl = rel_bias.reshape(N_BUCKETS, G, R).astype(jnp.float32)
    qT = (q * (HEAD_DIM ** -0.5)).astype(bf).reshape(B, n_qt, t, G, R, dh)
    qT = qT.transpose(0, 3, 1, 5, 4, 2).reshape(B, G, n_qt, dh, R * t)
    gT = gate_logits.reshape(B, n_qt, t, G, R, 3).transpose(0, 3, 1, 5, 4, 2).reshape(B, G, n_qt, 3, R * t)
    n_cmp = kcmp.shape[1]
    ncp = _round_up(n_cmp, 8)
    padc = lambda x: jnp.pad(x.astype(bf).transpose(0, 2, 1, 3), ((0, 0), (0, 0), (0, ncp - n_cmp), (0, 0)))
    tr = lambda x: x.astype(bf).transpose(0, 2, 1, 3)
    qp = jnp.arange(S).reshape(n_qt, 1, t)
    c_idx = jnp.arange(ncp).reshape(1, ncp, 1)
    dist = qp - (c_idx * CMP_STRIDE + CMP_LEN - 1)
    cb = _bias_lanes(tbl, dist, (dist >= 0) & (c_idx < n_cmp))
    n_delta = WINDOW // t + 2
    d = (jnp.arange(n_delta).reshape(-1, 1, 1) * t + jnp.arange(t).reshape(1, 1, t)
         - jnp.arange(t).reshape(1, t, 1))
    tz = _bias_lanes(tbl, d, (d >= 0) & (d < WINDOW))
    body = functools.partial(_nsa_prompt_body, n_sel=min(SEL_TOPK, S // SEL_BLOCK))
    kv_spec = pl.BlockSpec((None, None, S, dh), lambda b, g, i: (b, g, 0, 0))
    cmp_spec = pl.BlockSpec((None, None, ncp, dh), lambda b, g, i: (b, g, 0, 0))
    oT = pl.pallas_call(
        body,
        grid=(B, G, n_qt),
        in_specs=[pl.BlockSpec((None, None, None, dh, R * t), lambda b, g, i: (b, g, i, 0, 0)),
                  pl.BlockSpec((None, None, None, 3, R * t), lambda b, g, i: (b, g, i, 0, 0)),
                  cmp_spec, cmp_spec, kv_spec, kv_spec, kv_spec, kv_spec,
                  pl.BlockSpec((None, None, ncp, R * t), lambda b, g, i: (g, i, 0, 0)),
                  pl.BlockSpec((None,) + tz.shape[1:], lambda b, g, i: (g, 0, 0, 0))],
        out_specs=pl.BlockSpec((None, None, None, dh, R * t), lambda b, g, i: (b, g, i, 0, 0)),
        out_shape=jax.ShapeDtypeStruct((B, G, n_qt, dh, R * t), jnp.float32),
        scratch_shapes=[pltpu.VMEM((S, R * t), jnp.float32)],
        compiler_params=pltpu.CompilerParams(dimension_semantics=("parallel", "parallel", "arbitrary"),
                                             vmem_limit_bytes=VMEM_LIMIT_BYTES),
        name="nsa_prompt_attention",
    )(qT, gT, padc(kcmp), padc(vcmp), tr(ks), tr(vs), tr(kw), tr(vw), cb, tz)
    o = oT.reshape(B, G, n_qt, dh, R, t).transpose(0, 2, 5, 1, 4, 3)
    return o.reshape(B, S, G * R * dh)


SAMPLE_PAGES_PER_STEP = 4


def _softmax_lanes(s):
    m = jnp.max(s, axis=1, keepdims=True)
    e = jnp.where(s > MASKED_BELOW, jnp.exp(s - m), 0.0)
    return e, jnp.sum(e, axis=1, keepdims=True)


def _attend_lanes(qbd, k2, v2, bias, mask=None):
    s = jnp.dot(qbd, k2, preferred_element_type=jnp.float32) + bias
    if mask is not None:
        s = jnp.where(mask > 0.5, s, NEG_INF)
    e, l = _softmax_lanes(s)
    o = lax.dot_general(e.astype(jnp.bfloat16), v2, (((1,), (1,)), ((), ())),
                        preferred_element_type=jnp.float32)
    return o / jnp.maximum(l, TINY)


def _nsa_sample_body(pt_ref, qbd_ref, gate_ref, kc_ref, vc_ref, new_ref, wk_ref, wv_ref, cb_ref, sb_ref, wb_ref,
                     *rest, n_sel, tq, past, n_blocks):
    pps = SAMPLE_PAGES_PER_STEP
    k_pages, v_pages = rest[:pps], rest[pps:2 * pps]
    o_ref, ks_ref, vs_ref = rest[2 * pps:]
    step = pl.program_id(1)
    rows = ks_ref.shape[0]
    width = qbd_ref.shape[0]
    for i in range(pps):
        cols = pl.ds(pl.multiple_of((step * pps + i) * PAGE_SIZE, PAGE_SIZE), PAGE_SIZE)
        ks_ref[:, cols] = k_pages[i][...].reshape(rows, PAGE_SIZE).astype(jnp.bfloat16)
        vs_ref[:, cols] = v_pages[i][...].reshape(rows, PAGE_SIZE).astype(jnp.bfloat16)

    @pl.when(step == pl.num_programs(1) - 1)
    def _():
        qbd = qbd_ref[...]
        n_keys = ks_ref.shape[1]
        ks_ref[:, past:n_keys] = new_ref[0]
        vs_ref[:, past:n_keys] = new_ref[1]
        s_c = lax.dot_general(qbd, kc_ref[...], (((1,), (1,)), ((), ())),
                              preferred_element_type=jnp.float32) + cb_ref[...]
        e_c, l_c = _softmax_lanes(s_c)
        p_cmp = e_c / jnp.maximum(l_c, TINY)
        o_cmp = jnp.dot(p_cmp.astype(jnp.bfloat16), vc_ref[...], preferred_element_type=jnp.float32)
        ncp = p_cmp.shape[1]
        cs = lax.broadcasted_iota(jnp.int32, (ncp, LANES), 0) * CMP_STRIDE
        nb = lax.broadcasted_iota(jnp.int32, (ncp, LANES), 1) * SEL_BLOCK
        overlap_t = jnp.where((cs < nb + SEL_BLOCK) & (cs + CMP_LEN > nb), 1.0, 0.0).astype(jnp.bfloat16)
        ra = lax.broadcasted_iota(jnp.int32, (width, width), 0)
        rb = lax.broadcasted_iota(jnp.int32, (width, width), 1)
        per_group = NSA_GROUP * tq
        same_query = jnp.where((ra // per_group == rb // per_group) & (ra % tq == rb % tq), 1.0, 0.0)
        imp = _dot_f32_by_01(same_query.astype(jnp.bfloat16), _x_dot_01(p_cmp, overlap_t))
        blk = lax.broadcasted_iota(jnp.int32, (width, LANES), 1)
        cur = (past + lax.broadcasted_iota(jnp.int32, (width, LANES), 0) % tq) // SEL_BLOCK
        forced = (blk == 0) | (blk == cur) | (blk == cur - 1)
        score = jnp.where(blk <= cur, jnp.where(forced, FORCE_SCORE, imp), -1.0)
        score = jnp.where(blk < n_blocks, score, -2.0)
        score_t = score.T
        bt = lax.broadcasted_iota(jnp.int32, (LANES, width), 0)
        rank = jnp.zeros((LANES, width), jnp.float32)
        for mrow in range(n_blocks):
            row = score_t[mrow:mrow + 1, :]
            ahead = (row > score_t) | ((row == score_t) & (bt > mrow))
            rank = rank + jnp.where(ahead, 1.0, 0.0)
        sel = jnp.where(rank < n_sel, 1.0, 0.0).T.astype(jnp.bfloat16)
        kb = lax.broadcasted_iota(jnp.int32, (LANES, n_keys), 1) // SEL_BLOCK
        nn = lax.broadcasted_iota(jnp.int32, (LANES, n_keys), 0)
        expand = jnp.where(kb == nn, 1.0, 0.0).astype(jnp.bfloat16)
        mask = jnp.dot(sel, expand, preferred_element_type=jnp.float32)
        o_sel = _attend_lanes(qbd, ks_ref[...], vs_ref[...], sb_ref[...], mask)
        window = lambda ref, new: jnp.concatenate(
            [ref[...].reshape(rows, ref.shape[-1]).astype(jnp.bfloat16), new], axis=1)
        o_win = _attend_lanes(qbd, window(wk_ref, new_ref[2]), window(wv_ref, new_ref[3]), wb_ref[...])
        g = 1.0 / (1.0 + jnp.exp(-gate_ref[...]))
        o = g[:, 0:1] * o_cmp + g[:, 1:2] * o_sel + g[:, 2:3] * o_win
        r = lax.broadcasted_iota(jnp.int32, (width, rows), 0) // per_group
        c = lax.broadcasted_iota(jnp.int32, (width, rows), 1) // HEAD_DIM
        o = jnp.where(r == c, o, 0.0)
        fa = lax.broadcasted_iota(jnp.int32, (rows, LANES), 0) % HEAD_DIM
        fb = lax.broadcasted_iota(jnp.int32, (rows, LANES), 1)
        fold = jnp.where(fa == fb, 1.0, 0.0).astype(jnp.bfloat16)
        o_ref[...] = _x_dot_01(o, fold)[:, :HEAD_DIM]


def _rows_table(table):
    g, k, w = table.shape
    return table.transpose(0, 2, 1).reshape(g * w, k)


def nsa_sample_attention(q, gate_logits, kcmp, vcmp, new_kv, sk_pool, sv_pool, wk_buf, wv_buf, page_table, rel_bias):
    DB, T, G, R, dh = q.shape
    bf = jnp.bfloat16
    width = G * R * T
    rows = G * dh
    n_pages = page_table.shape[1]
    past = n_pages * PAGE_SIZE
    n_keys = past + PAGE_SIZE
    n_blocks = _round_up(past + T, SEL_BLOCK) // SEL_BLOCK
    tbl = rel_bias.reshape(N_BUCKETS, G, R).astype(jnp.float32)
    q_pos = past + jnp.arange(T)
    qg = (q * (HEAD_DIM ** -0.5)).transpose(0, 2, 3, 1, 4)
    same_group = jnp.eye(G, dtype=qg.dtype).reshape(1, G, 1, 1, G, 1)
    qbd = (qg[:, :, :, :, None, :] * same_group).astype(bf).reshape(DB, width, rows)
    gates = jnp.pad(gate_logits.transpose(0, 2, 3, 1, 4).reshape(DB, width, 3), ((0, 0), (0, 0), (0, 5)))
    n_cmp = n_blocks * SEL_BLOCK // CMP_STRIDE - CMP_LEN // CMP_STRIDE + 1
    ncp = kcmp.shape[1]
    c_idx = jnp.arange(ncp).reshape(ncp, 1)
    dist = q_pos.reshape(1, T) - (c_idx * CMP_STRIDE + CMP_LEN - 1)
    cb = _rows_table(_bias_lanes(tbl, dist, (dist >= 0) & (c_idx < n_cmp)))
    k_idx = jnp.arange(n_keys).reshape(-1, 1)
    dist = q_pos.reshape(1, T) - k_idx
    sb = _rows_table(_bias_lanes(tbl, dist, dist >= 0))
    wbuf = wk_buf.shape[1]
    n_win = wbuf + PAGE_SIZE
    w_idx = jnp.arange(n_win).reshape(-1, 1)
    w_pos = past - wbuf + w_idx
    dist = q_pos.reshape(1, T) - w_pos
    wb = _rows_table(_bias_lanes(tbl, dist, (dist >= 0) & (dist < WINDOW) & (w_pos >= 0) & (w_idx < wbuf + T)))
    new_t = lambda x: jnp.pad(x.astype(bf).transpose(0, 2, 3, 1),
                              ((0, 0), (0, 0), (0, 0), (0, PAGE_SIZE - T))).reshape(DB, rows, PAGE_SIZE)
    new = jnp.stack([new_t(x) for x in new_kv], axis=1)
    native = lambda x: x.transpose(0, 2, 3, 1)
    pps = SAMPLE_PAGES_PER_STEP
    body = functools.partial(_nsa_sample_body, n_sel=min(SEL_TOPK, n_blocks), tq=T, past=past, n_blocks=n_blocks)
    per_b = lambda shape: pl.BlockSpec((None,) + shape, lambda b, s, pt: (b,) + (0,) * len(shape))
    const = lambda x: pl.BlockSpec(x.shape, lambda b, s, pt: (0,) * x.ndim)
    page = lambda i: pl.BlockSpec((None, G, dh, PAGE_SIZE), lambda b, s, pt: (pt[b, s * pps + i], 0, 0, 0))
    sk_t, sv_t = native(sk_pool), native(sv_pool)
    o = pl.pallas_call(
        body,
        grid_spec=pltpu.PrefetchScalarGridSpec(
            num_scalar_prefetch=1,
            grid=(DB, n_pages // pps),
            in_specs=[per_b((width, rows)), per_b((width, 8)), per_b((ncp, rows)), per_b((ncp, rows)),
                      per_b((4, rows, PAGE_SIZE)), per_b((G, dh, wbuf)), per_b((G, dh, wbuf)),
                      const(cb), const(sb), const(wb)]
                     + [page(i) for i in range(pps)] * 2,
            out_specs=per_b((width, dh)),
            scratch_shapes=[pltpu.VMEM((rows, n_keys), bf), pltpu.VMEM((rows, n_keys), bf)]),
        out_shape=jax.ShapeDtypeStruct((DB, width, dh), jnp.float32),
        compiler_params=pltpu.CompilerParams(dimension_semantics=("parallel", "arbitrary"),
                                             vmem_limit_bytes=VMEM_LIMIT_BYTES),
        name="nsa_sample_attention",
    )(page_table, qbd, gates, kcmp, vcmp, new, native(wk_buf), native(wv_buf), cb, sb, wb,
      *([sk_t] * pps), *([sv_t] * pps))
    return o.reshape(DB, G, R, T, dh).transpose(0, 3, 1, 2, 4).reshape(DB, T, G * R * dh)


FOX_Q_TILE = 512
FOX_K_TILE = 256


def _softmax_step(s, v, carry, masked):
    m, l, acc = carry
    m_new = jnp.maximum(m, jnp.max(s, axis=0, keepdims=True))
    alpha = jnp.exp(m - m_new)
    e = jnp.exp(s - m_new)
    if masked:
        e = jnp.where(s > MASKED_BELOW, e, 0.0)
    l = alpha * l + jnp.sum(e, axis=0, keepdims=True)
    pv = lax.dot_general(v, e.astype(jnp.bfloat16), (((0,), (0,)), ((), ())),
                         preferred_element_type=jnp.float32)
    return m_new, l, alpha * acc + pv


def _fox_prompt_body(qT_ref, k_ref, v_ref, o_ref):
    i = pl.program_id(2)
    qT = qT_ref[...]
    width = qT.shape[1]

    def tile(j):
        rows = pl.ds(pl.multiple_of(j * FOX_K_TILE, FOX_K_TILE), FOX_K_TILE)
        return jnp.dot(k_ref[rows, :], qT, preferred_element_type=jnp.float32), v_ref[rows, :]

    def full_step(j, carry):
        s, v = tile(j)
        return _softmax_step(s, v, carry, masked=False)

    carry = (jnp.full((1, width), NEG_INF, jnp.float32), jnp.zeros((1, width), jnp.float32),
             jnp.zeros((HEAD_DIM, width), jnp.float32))
    ratio = FOX_Q_TILE // FOX_K_TILE
    carry = lax.fori_loop(0, i * ratio, full_step, carry)
    q_pos = i * FOX_Q_TILE + lax.broadcasted_iota(jnp.int32, (FOX_K_TILE, width), 1)
    for dj in range(ratio):
        j = i * ratio + dj
        s, v = tile(j)
        k_pos = j * FOX_K_TILE + lax.broadcasted_iota(jnp.int32, (FOX_K_TILE, width), 0)
        s = jnp.where(k_pos <= q_pos, s, NEG_INF)
        carry = _softmax_step(s, v, carry, masked=True)
    m, l, acc = carry
    o_ref[...] = acc / jnp.maximum(l, TINY)


def _augment(c):
    hi, mid, lo = _split3_outside_kernel(c)
    one = jnp.ones_like(hi)
    return jnp.stack([hi, mid, lo, one, one, one], axis=-1)


def fox_prompt_attention(q, k, v, logf):
    B, S, H, dh = q.shape
    bf = jnp.bfloat16
    c = jnp.cumsum(logf, axis=1).transpose(0, 2, 1)
    aug = _augment(c)
    pad = jnp.zeros((B, H, S, LANES - dh - 6), bf)
    kb = k.astype(bf).transpose(0, 2, 1, 3)
    k_aug = jnp.concatenate([kb, -aug[..., :3], aug[..., 3:], pad], axis=-1)
    qb = (q * (dh ** -0.5)).astype(bf).transpose(0, 2, 1, 3)
    q_aug = jnp.concatenate([qb, aug[..., 3:], aug[..., :3], pad], axis=-1)
    qT = q_aug.transpose(0, 1, 3, 2)
    vb = v.astype(bf).transpose(0, 2, 1, 3)
    oT = pl.pallas_call(
        _fox_prompt_body,
        grid=(B, H, S // FOX_Q_TILE),
        in_specs=[pl.BlockSpec((None, None, LANES, FOX_Q_TILE), lambda b, h, i: (b, h, 0, i)),
                  pl.BlockSpec((None, None, S, LANES), lambda b, h, i: (b, h, 0, 0)),
                  pl.BlockSpec((None, None, S, dh), lambda b, h, i: (b, h, 0, 0))],
        out_specs=pl.BlockSpec((None, None, dh, FOX_Q_TILE), lambda b, h, i: (b, h, 0, i)),
        out_shape=jax.ShapeDtypeStruct((B, H, dh, S), jnp.float32),
        compiler_params=pltpu.CompilerParams(dimension_semantics=("parallel", "parallel", "arbitrary"),
                                             vmem_limit_bytes=VMEM_LIMIT_BYTES),
        name="fox_prompt_attention",
    )(qT, k_aug, vb)
    return oT.transpose(0, 3, 1, 2).reshape(B, S, H * dh)


def _fox_sample_body(pt_ref, qbd_ref, kp_ref, vp_ref, bp_ref, kn_ref, vn_ref, bn_ref, o_ref,
                     m_ref, l_ref, acc_ref, *, n_q):
    p = pl.program_id(1)
    width = qbd_ref.shape[0]
    rows = kn_ref.shape[0]

    @pl.when(p == 0)
    def _():
        m_ref[...] = jnp.full(m_ref.shape, NEG_INF, jnp.float32)
        l_ref[...] = jnp.zeros(l_ref.shape, jnp.float32)
        acc_ref[...] = jnp.zeros(acc_ref.shape, jnp.float32)

    def attend(k2, v2, bias_t):
        s_t = jnp.dot(qbd_ref[...], k2, preferred_element_type=jnp.float32).T + bias_t
        m = m_ref[...]
        m_new = jnp.maximum(m, jnp.max(s_t, axis=0, keepdims=True))
        alpha = jnp.exp(m - m_new)
        e = jnp.where(s_t > MASKED_BELOW, jnp.exp(s_t - m_new), 0.0)
        m_ref[...] = m_new
        l_ref[...] = alpha * l_ref[...] + jnp.sum(e, axis=0, keepdims=True)
        acc_ref[...] = alpha * acc_ref[...] + jnp.dot(v2, e.astype(jnp.bfloat16),
                                                      preferred_element_type=jnp.float32)

    page = lambda ref: ref[...].reshape(rows, PAGE_SIZE).astype(jnp.bfloat16)
    attend(page(kp_ref), page(vp_ref), bp_ref[...])

    @pl.when(p == pl.num_programs(1) - 1)
    def _():
        attend(kn_ref[...], vn_ref[...], bn_ref[...])
        out = acc_ref[...] / jnp.maximum(l_ref[...], TINY)
        r = lax.broadcasted_iota(jnp.int32, (rows, width), 0) // HEAD_DIM
        c = lax.broadcasted_iota(jnp.int32, (rows, width), 1) // n_q
        out = jnp.where(r == c, out, 0.0)
        la = lax.broadcasted_iota(jnp.int32, (width, width), 0) % n_q
        lb = lax.broadcasted_iota(jnp.int32, (width, width), 1)
        gather_q = jnp.where(la == lb, 1.0, 0.0).astype(jnp.bfloat16)
        o_ref[...] = _x_dot_01(out, gather_q)[:, :n_q]


def fox_sample_attention(q, k, v, logf, k_pool, v_pool, f_pool, page_table):
    DB, T, H, dh = q.shape
    n_pages = page_table.shape[1]
    past = n_pages * PAGE_SIZE
    bf = jnp.bfloat16
    width = H * T
    rows = H * dh
    f_all = jnp.concatenate([f_pool[page_table].reshape(DB, past, H), logf], axis=1)
    c = jnp.cumsum(f_all, axis=1)
    c_q = c[:, past:].transpose(0, 2, 1).reshape(DB, 1, 1, width)
    c_k = jnp.repeat(c[:, :past].reshape(DB, n_pages, PAGE_SIZE, H), T, axis=-1)
    bias_past = c_q - c_k
    tok = jnp.arange(PAGE_SIZE).reshape(1, PAGE_SIZE, 1)
    qi = (jnp.arange(width) % T).reshape(1, 1, width)
    c_new = jnp.pad(jnp.repeat(c[:, past:], T, axis=-1), ((0, 0), (0, PAGE_SIZE - T), (0, 0)))
    bias_new = jnp.where((tok <= qi) & (tok < T), c_q[:, 0] - c_new, NEG_INF)
    qh = (q * (dh ** -0.5)).transpose(0, 2, 1, 3)
    same_head = jnp.eye(H, dtype=qh.dtype).reshape(1, H, 1, H, 1)
    qbd = (qh[:, :, :, None, :] * same_head).astype(bf).reshape(DB, width, rows)
    new_t = lambda x: jnp.pad(x.astype(bf).transpose(0, 2, 3, 1),
                              ((0, 0), (0, 0), (0, 0), (0, PAGE_SIZE - T))).reshape(DB, rows, PAGE_SIZE)
    pool_t = lambda x: x.transpose(0, 2, 3, 1)
    body = functools.partial(_fox_sample_body, n_q=T)
    page_spec = pl.BlockSpec((None, H, dh, PAGE_SIZE), lambda b, p, pt: (pt[b, p], 0, 0, 0))
    per_b = lambda shape: pl.BlockSpec((None,) + shape, lambda b, p, pt: (b,) + (0,) * len(shape))
    o = pl.pallas_call(
        body,
        grid_spec=pltpu.PrefetchScalarGridSpec(
            num_scalar_prefetch=1,
            grid=(DB, n_pages),
            in_specs=[per_b((width, rows)), page_spec, page_spec,
                      pl.BlockSpec((None, None, PAGE_SIZE, width), lambda b, p, pt: (b, p, 0, 0)),
                      per_b((rows, PAGE_SIZE)), per_b((rows, PAGE_SIZE)), per_b((PAGE_SIZE, width))],
            out_specs=per_b((rows, T)),
            scratch_shapes=[pltpu.VMEM((1, width), jnp.float32), pltpu.VMEM((1, width), jnp.float32),
                            pltpu.VMEM((rows, width), jnp.float32)]),
        out_shape=jax.ShapeDtypeStruct((DB, rows, T), jnp.float32),
        compiler_params=pltpu.CompilerParams(dimension_semantics=("parallel", "arbitrary"),
                                             vmem_limit_bytes=VMEM_LIMIT_BYTES),
        name="fox_sample_attention",
    )(page_table, qbd, pool_t(k_pool), pool_t(v_pool), bias_past, new_t(k), new_t(v), bias_new)
    return o.reshape(DB, H, dh, T).transpose(0, 3, 1, 2).reshape(DB, T, H * dh)


def _compress_body(pt_ref, newk_ref, newv_ref, pe_ref, w1_ref, w2_ref, *rest, n_tok, n_out):
    pps = SAMPLE_PAGES_PER_STEP
    k_pages, v_pages = rest[:pps], rest[pps:2 * pps]
    ko_ref, vo_ref, xk_ref, xv_ref, flat_ref = rest[2 * pps:]
    step = pl.program_id(1)
    rows = newk_ref.shape[1]
    pairs = rows // LANES
    for i in range(pps):
        dst = pl.ds(pl.multiple_of((step * pps + i) * PAGE_SIZE, PAGE_SIZE), PAGE_SIZE)
        for x_ref, pages in ((xk_ref, k_pages), (xv_ref, v_pages)):
            page_t = pages[i][...].reshape(rows, PAGE_SIZE).T
            for gp in range(pairs):
                x_ref[gp, dst, :] = page_t[:, gp * LANES:(gp + 1) * LANES]

    @pl.when(step == pl.num_programs(1) - 1)
    def _():
        n_new = newk_ref.shape[0]
        tail = xk_ref.shape[1] - n_tok - n_new
        for which, (x_ref, new_ref, o_ref) in enumerate(((xk_ref, newk_ref, ko_ref), (xv_ref, newv_ref, vo_ref))):
            for gp in range(pairs):
                x_ref[gp, n_tok:n_tok + n_new, :] = new_ref[:, gp * LANES:(gp + 1) * LANES]
                x_ref[gp, n_tok + n_new:, :] = jnp.zeros((tail, LANES), jnp.float32)
            for l in range(CMP_LEN):
                for gp in range(pairs):
                    blk = x_ref[gp, pl.ds(l, n_out, stride=CMP_STRIDE), :]
                    blk = blk + pe_ref[which, l:l + 1, :]
                    flat_ref[gp * n_out:(gp + 1) * n_out, l * LANES:(l + 1) * LANES] = blk.astype(jnp.bfloat16)
            hidden = _gelu_exact(jnp.dot(flat_ref[...], w1_ref[which], preferred_element_type=jnp.float32))
            out = jnp.dot(hidden.astype(jnp.bfloat16), w2_ref[which], preferred_element_type=jnp.float32)
            o_ref[...] = jnp.concatenate([out[gp * n_out:(gp + 1) * n_out] for gp in range(pairs)],
                                         axis=1).astype(o_ref.dtype)


def _pair_weights(w1, w2, pe):
    dh = w2.shape[0]
    eye2 = jnp.eye(2, dtype=w1.dtype)
    w1p = jnp.einsum('lde,pq->lpdqe', w1.reshape(CMP_LEN, dh, dh), eye2).reshape(CMP_LEN * 2 * dh, 2 * dh)
    w2p = jnp.einsum('de,pq->pdqe', w2, eye2).reshape(2 * dh, 2 * dh)
    pep = jnp.concatenate([pe, pe], axis=1)
    return w1p, w2p, pep


def compress_paged(k_pool, v_pool, new_k, new_v, page_table, weights_k, weights_v):
    DB, T, G, dh = new_k.shape
    bf = jnp.bfloat16
    rows = G * dh
    n_pages = page_table.shape[1]
    past = n_pages * PAGE_SIZE
    lp = _round_up(past + T, SEL_BLOCK)
    n_cmp = lp // CMP_STRIDE - CMP_LEN // CMP_STRIDE + 1
    n_out = _round_up(n_cmp, 8)
    x_rows = _round_up(CMP_LEN + CMP_STRIDE * (n_out - 1), PAGE_SIZE)
    wk = _pair_weights(weights_k[1], weights_k[2], weights_k[0])
    wv = _pair_weights(weights_v[1], weights_v[2], weights_v[0])
    w1 = jnp.stack([wk[0], wv[0]]).astype(bf)
    w2 = jnp.stack([wk[1], wv[1]]).astype(bf)
    pe = jnp.stack([wk[2], wv[2]])
    native = lambda x: x.transpose(0, 2, 3, 1)
    pps = SAMPLE_PAGES_PER_STEP
    body = functools.partial(_compress_body, n_tok=past, n_out=n_out)
    per_b = lambda shape: pl.BlockSpec((None,) + shape, lambda b, s, pt: (b,) + (0,) * len(shape))
    const = lambda x: pl.BlockSpec(x.shape, lambda b, s, pt: (0,) * x.ndim)
    page = lambda i: pl.BlockSpec((None, G, dh, PAGE_SIZE), lambda b, s, pt: (pt[b, s * pps + i], 0, 0, 0))
    kt, vt = native(k_pool), native(v_pool)
    out_shape = jax.ShapeDtypeStruct((DB, n_out, rows), bf)
    return pl.pallas_call(
        body,
        grid_spec=pltpu.PrefetchScalarGridSpec(
            num_scalar_prefetch=1,
            grid=(DB, n_pages // pps),
            in_specs=[per_b((T, rows)), per_b((T, rows)), const(pe), const(w1), const(w2)]
                     + [page(i) for i in range(pps)] * 2,
            out_specs=[per_b((n_out, rows)), per_b((n_out, rows))],
            scratch_shapes=[pltpu.VMEM((rows // LANES, x_rows, LANES), jnp.float32),
                            pltpu.VMEM((rows // LANES, x_rows, LANES), jnp.float32),
                            pltpu.VMEM((rows // LANES * n_out, CMP_LEN * LANES), bf)]),
        out_shape=[out_shape, out_shape],
        compiler_params=pltpu.CompilerParams(dimension_semantics=("parallel", "arbitrary"),
                                             vmem_limit_bytes=VMEM_LIMIT_BYTES),
        name="nsa_compress_paged",
    )(page_table, new_k.reshape(DB, T, rows), new_v.reshape(DB, T, rows), pe, w1, w2, *([kt] * pps), *([vt] * pps))


def compress(x, pe, w1, w2):
    B, T, G, dh = x.shape
    r = CMP_LEN // CMP_STRIDE
    n_chunks = T // CMP_STRIDE
    nc = n_chunks - r + 1
    ch = x.reshape(B, n_chunks, CMP_STRIDE, G, dh)
    blk = jnp.concatenate([ch[:, m:m + nc] for m in range(r)], axis=2)
    blk = blk + pe[:, None, :]
    flat = blk.transpose(0, 1, 3, 2, 4).reshape(B, nc, G, CMP_LEN * dh)
    return jax.nn.gelu(flat @ w1, approximate=False) @ w2


def nsa_split(proj):
    B, T, _ = proj.shape
    nq = NSA_HEADS * HEAD_DIM
    nkv = NSA_KV_HEADS * HEAD_DIM
    q = proj[..., :nq].reshape(B, T, NSA_KV_HEADS, NSA_GROUP, HEAD_DIM)
    kv = proj[..., nq:nq + 6 * nkv].reshape(B, T, 6, NSA_KV_HEADS, HEAD_DIM)
    gl = proj[..., nq + 6 * nkv:nq + 6 * nkv + 3 * NSA_HEADS].reshape(B, T, NSA_KV_HEADS, NSA_GROUP, 3)
    return q, gl, [kv[:, :, s] for s in range(6)]


def nsa_prompt(proj, cw, rel_bias):
    S = proj.shape[1]
    pe_k, w1_k, w2_k, pe_v, w1_v, w2_v = cw
    q, gl, (kc, vc, ks, vs, kw, vw) = nsa_split(proj)
    kcmp = compress(kc, pe_k, w1_k, w2_k)
    vcmp = compress(vc, pe_v, w1_v, w2_v)
    o = nsa_prompt_attention(q, gl, kcmp, vcmp, ks, vs, kw, vw, rel_bias)
    wb = min(WINDOW, S)
    return o, [kc, vc, ks, vs, kw[:, S - wb:], vw[:, S - wb:]]


def nsa_sample(proj, cw, rel_bias, ck_pool, cv_pool, sk_pool, sv_pool, wk_buf, wv_buf, page_table):
    DB, T, _ = proj.shape
    pe_k, w1_k, w2_k, pe_v, w1_v, w2_v = cw
    q, gl, (kc, vc, ks, vs, kw, vw) = nsa_split(proj)
    kcmp, vcmp = compress_paged(ck_pool, cv_pool, kc, vc, page_table, (pe_k, w1_k, w2_k), (pe_v, w1_v, w2_v))
    kw_all = jnp.concatenate([wk_buf, kw], axis=1)
    vw_all = jnp.concatenate([wv_buf, vw], axis=1)
    o = nsa_sample_attention(q, gl, kcmp, vcmp, (ks, vs, kw, vw), sk_pool, sv_pool, wk_buf, wv_buf,
                             page_table, rel_bias)
    return o, [kc, vc, ks, vs, kw_all[:, T:], vw_all[:, T:]]


def kernel(x_prompt, x_sample, p_prompt, p_sample, cache_fox_k, cache_fox_v, cache_fox_logf,
           cache_nsa_cmp_k, cache_nsa_cmp_v, cache_nsa_sel_k, cache_nsa_sel_v,
           cache_nsa_win_k, cache_nsa_win_v, page_table,
           norm_mix, norm_ffn, norm_ple, norm_final,
           fox_w_in, fox_b_f, fox_w_out, nsa_w_in, nsa_w_out,
           cmp_pe_k, cmp_w1_k, cmp_w2_k, cmp_pe_v, cmp_w1_v, cmp_w2_v, rel_bias,
           peer_wq, peer_sub_k1, peer_sub_k2, peer_u, peer_v, ple_w_proj, ple_w_gate):
    B, S, d = x_prompt.shape
    DB, T, _ = x_sample.shape
    n_p, n_s = B * S, DB * T
    bf = jnp.bfloat16
    rows = lambda a, b: jnp.concatenate([a.reshape(n_p, -1), b.reshape(n_s, -1)], axis=0)
    x = rows(x_prompt, x_sample)
    fox_p, fox_s, nsa_p, nsa_s = [], [], [], []
    for i in range(DEPTH):
        j = i // 2
        if i % 2 == 0:
            h = rmsnorm_pallas(x, norm_mix[i], bf)
            proj = linear_pallas(h, fox_w_in[j])
            nh = FOX_HEADS * HEAD_DIM
            logf = jax.nn.log_sigmoid(proj[:, 3 * nh:3 * nh + FOX_HEADS] + fox_b_f[j])
            heads = lambda a, lead: a.reshape(lead + (FOX_HEADS, -1))
            qp, kp, vp = (heads(proj[:n_p, s * nh:(s + 1) * nh], (B, S)) for s in range(3))
            qs, ks_, vs_ = (heads(proj[n_p:, s * nh:(s + 1) * nh], (DB, T)) for s in range(3))
            fp, fs = logf[:n_p].reshape(B, S, FOX_HEADS), logf[n_p:].reshape(DB, T, FOX_HEADS)
            op = fox_prompt_attention(qp, kp, vp, fp)
            os_ = fox_sample_attention(qs, ks_, vs_, fs, cache_fox_k[j], cache_fox_v[j], cache_fox_logf[j],
                                       page_table)
            fox_p.append([kp, vp, fp])
            fox_s.append([ks_, vs_, fs])
            y = linear_pallas(rows(op, os_), fox_w_out[j])
        else:
            cw = (cmp_pe_k[j], cmp_w1_k[j], cmp_w2_k[j], cmp_pe_v[j], cmp_w1_v[j], cmp_w2_v[j])
            h = rmsnorm_pallas(x, norm_mix[i], bf)
            proj = linear_pallas(h, nsa_w_in[j])
            op, stp = nsa_prompt(proj[:n_p].reshape(B, S, -1), cw, rel_bias)
            os_, sts = nsa_sample(proj[n_p:].reshape(DB, T, -1), cw, rel_bias,
                                  cache_nsa_cmp_k[j], cache_nsa_cmp_v[j], cache_nsa_sel_k[j],
                                  cache_nsa_sel_v[j], cache_nsa_win_k[j], cache_nsa_win_v[j], page_table)
            nsa_p.append(stp)
            nsa_s.append(sts)
            y = linear_pallas(rows(op, os_), nsa_w_out[j])
        x = x + y
        h = rmsnorm_pallas(x, norm_ffn[i], bf)
        x = x + peer_pallas(h, peer_wq[i].T.astype(bf), peer_sub_k1[i].astype(bf), peer_sub_k2[i].astype(bf),
                            peer_u[i].astype(bf), peer_v[i].astype(bf))
        x = ple_pallas(x, rows(p_prompt[i], p_sample[i]), norm_ple[i], ple_w_proj[i], ple_w_gate[i])
    y = rmsnorm_pallas(x, norm_final, jnp.float32)
    st = lambda lst, k: jnp.stack([s[k] for s in lst])
    return (y[:n_p].reshape(B, S, d), y[n_p:].reshape(DB, T, d),
            st(fox_p, 0), st(fox_p, 1), st(fox_p, 2),
            st(fox_s, 0), st(fox_s, 1), st(fox_s, 2),
            st(nsa_p, 0), st(nsa_p, 1), st(nsa_p, 2), st(nsa_p, 3), st(nsa_p, 4), st(nsa_p, 5),
            st(nsa_s, 0), st(nsa_s, 1), st(nsa_s, 2), st(nsa_s, 3), st(nsa_s, 4), st(nsa_s, 5))
```

```python
import functools
import math

import jax
import jax.numpy as jnp
from jax import lax
from jax.experimental import pallas as pl
from jax.experimental.pallas import tpu as pltpu

D_MODEL = 1024
DEPTH = 2
PAGE_SIZE = 128
HEAD_DIM = 64
FOX_HEADS = D_MODEL // HEAD_DIM
NSA_HEADS = D_MODEL // HEAD_DIM
NSA_KV_HEADS = 4
NSA_GROUP = NSA_HEADS // NSA_KV_HEADS
CMP_LEN = 32
CMP_STRIDE = 16
SEL_BLOCK = 64
SEL_TOPK = 16
WINDOW = 512
N_BUCKETS = 32
MAX_DISTANCE = 128
PEER_HEADS = 8
PEER_TOPK = 16
N_KEYS = 128
N_EXPERTS = N_KEYS * N_KEYS
PEER_DKEY = 256
FORCE_SCORE = 1e4
RMS_EPS = 1e-6
NEG_INF = -1e30
TINY = 1e-30

VMEM_LIMIT_BYTES = 56 * 1024 * 1024
LANES = 128
ROW_TILE = 512
MAX_COL_TILE = 1024


def _round_up(x, m):
    return -(-x // m) * m


def _split3(x):
    hi = x.astype(jnp.bfloat16)
    r1 = x - hi.astype(jnp.float32)
    mid = r1.astype(jnp.bfloat16)
    lo = (r1 - mid.astype(jnp.float32)).astype(jnp.bfloat16)
    return hi, mid, lo


def _split3_outside_kernel(x):
    to_bf16 = lambda a: lax.reduce_precision(a, exponent_bits=8, mantissa_bits=7)
    hi = to_bf16(x)
    r1 = x - hi
    mid = to_bf16(r1)
    lo = to_bf16(r1 - mid)
    return hi.astype(jnp.bfloat16), mid.astype(jnp.bfloat16), lo.astype(jnp.bfloat16)


def _rms(x, g):
    return x * lax.rsqrt(jnp.mean(x * x, axis=-1, keepdims=True) + RMS_EPS) * g


def _rmsnorm_body(x_ref, g_ref, o_ref):
    o_ref[...] = _rms(x_ref[...], g_ref[...]).astype(o_ref.dtype)


def rmsnorm_pallas(x, g, out_dtype):
    n, d = x.shape
    return pl.pallas_call(
        _rmsnorm_body,
        grid=(n // ROW_TILE,),
        in_specs=[pl.BlockSpec((ROW_TILE, d), lambda i: (i, 0)),
                  pl.BlockSpec((1, d), lambda i: (0, 0))],
        out_specs=pl.BlockSpec((ROW_TILE, d), lambda i: (i, 0)),
        out_shape=jax.ShapeDtypeStruct((n, d), out_dtype),
        compiler_params=pltpu.CompilerParams(dimension_semantics=("parallel",)),
        name="rmsnorm",
    )(x, g.reshape(1, d))


def _linear_body(x_ref, w_ref, o_ref):
    o_ref[...] = jnp.dot(x_ref[...].astype(jnp.bfloat16), w_ref[...], preferred_element_type=jnp.float32)


def linear_pallas(x, w):
    n, k = x.shape
    m = w.shape[1]
    mp = _round_up(m, LANES)
    tn = max(t for t in range(LANES, MAX_COL_TILE + 1, LANES) if mp % t == 0)
    wb = jnp.pad(w.astype(jnp.bfloat16), ((0, 0), (0, mp - m)))
    return pl.pallas_call(
        _linear_body,
        grid=(n // ROW_TILE, mp // tn),
        in_specs=[pl.BlockSpec((ROW_TILE, k), lambda i, j: (i, 0)),
                  pl.BlockSpec((k, tn), lambda i, j: (0, j))],
        out_specs=pl.BlockSpec((ROW_TILE, tn), lambda i, j: (i, j)),
        out_shape=jax.ShapeDtypeStruct((n, mp), jnp.float32),
        compiler_params=pltpu.CompilerParams(dimension_semantics=("parallel", "parallel"),
                                             vmem_limit_bytes=VMEM_LIMIT_BYTES),
        name="linear",
    )(x, wb)


def _ple_body(x_ref, p_ref, g_ref, wg_ref, wp_ref, o_ref):
    x = x_ref[...]
    h = _rms(x, g_ref[...]).astype(jnp.bfloat16)
    gate = 1.0 / (1.0 + jnp.exp(-jnp.dot(h, wg_ref[...], preferred_element_type=jnp.float32)))
    proj = jnp.dot(p_ref[...].astype(jnp.bfloat16), wp_ref[...], preferred_element_type=jnp.float32)
    o_ref[...] = x + gate * proj


def ple_pallas(x, p, g, w_proj, w_gate):
    n, d = x.shape
    dp = p.shape[1]
    return pl.pallas_call(
        _ple_body,
        grid=(n // ROW_TILE,),
        in_specs=[pl.BlockSpec((ROW_TILE, d), lambda i: (i, 0)),
                  pl.BlockSpec((ROW_TILE, dp), lambda i: (i, 0)),
                  pl.BlockSpec((1, d), lambda i: (0, 0)),
                  pl.BlockSpec((d, d), lambda i: (0, 0)),
                  pl.BlockSpec((dp, d), lambda i: (0, 0))],
        out_specs=pl.BlockSpec((ROW_TILE, d), lambda i: (i, 0)),
        out_shape=jax.ShapeDtypeStruct((n, d), jnp.float32),
        compiler_params=pltpu.CompilerParams(dimension_semantics=("parallel",),
                                             vmem_limit_bytes=VMEM_LIMIT_BYTES),
        name="ple",
    )(x, p, g.reshape(1, d), w_gate.astype(jnp.bfloat16), w_proj.astype(jnp.bfloat16))


PEER_TOKEN_TILE = 512
PEER_EXPERT_TILE = 1024
SQRT_HALF = 0.7071067811865476


def _gelu_exact(x):
    return 0.5 * x * (1.0 + lax.erf(x * SQRT_HALF))


def _top_rows(x, k):
    vals = []
    for _ in range(k):
        m = jnp.max(x, axis=0, keepdims=True)
        vals.append(m)
        x = jnp.where(x == m, NEG_INF, x)
    return vals


def _peer_route_body(h_ref, wqT_ref, k1_ref, k2_ref, n1_ref, c1_ref, r2_ref, e2_ref):
    h = h_ref[...]
    half = PEER_DKEY // 2
    nt = (((1,), (1,)), ((), ()))
    for hd in range(PEER_HEADS):
        qv = lax.dot_general(wqT_ref[hd * PEER_DKEY:(hd + 1) * PEER_DKEY, :], h, nt,
                             preferred_element_type=jnp.float32)
        s1 = jnp.dot(k1_ref[hd], qv[:half].astype(jnp.bfloat16), preferred_element_type=jnp.float32)
        s2 = jnp.dot(k2_ref[hd], qv[half:].astype(jnp.bfloat16), preferred_element_type=jnp.float32)
        v1 = _top_rows(s1, PEER_TOPK)
        v2 = _top_rows(s2, PEER_TOPK)
        v2_stack = jnp.concatenate(v2, axis=0)
        blocks = []
        for p in range(PEER_TOPK):
            n_p = PEER_TOPK // (p + 1)
            rows = -(-n_p // 8) * 8
            blk = v1[p] + v2_stack[:rows]
            if n_p < rows:
                r = lax.broadcasted_iota(jnp.int32, blk.shape, 0)
                blk = jnp.where(r < n_p, blk, NEG_INF)
            blocks.append(blk)
        c = _top_rows(jnp.concatenate(blocks, axis=0), PEER_TOPK)
        z = jnp.ones_like(c[0])
        for kk in range(1, PEER_TOPK):
            z = z + jnp.exp(c[kk] - c[0])
        tau = c[PEER_TOPK - 1]
        n1 = jnp.zeros_like(s1)
        r2 = jnp.zeros_like(s2)
        for q in range(PEER_TOPK):
            n1 = n1 + jnp.where(s1 + v2[q] >= tau, 1.0, 0.0)
            r2 = r2 + jnp.where(v2[q] > s2, 1.0, 0.0)
        n1_ref[hd] = n1
        c1_ref[hd] = jnp.exp(s1 - v1[0]) / z
        r2_ref[hd] = r2.astype(jnp.bfloat16)
        e2_ref[hd] = jnp.exp(s2 - v2[0]).astype(jnp.bfloat16)


def _peer_dense_body(h_ref, u_ref, v_ref, n1_ref, c1_ref, r2_ref, e2_ref, y_ref):
    j = pl.program_id(1)

    @pl.when(j == 0)
    def _():
        y_ref[...] = jnp.zeros_like(y_ref)

    h = h_ref[...]
    act = lax.dot_general(u_ref[...], h, (((1,), (1,)), ((), ())),
                          preferred_element_type=jnp.float32)
    groups = PEER_EXPERT_TILE // N_KEYS
    parts = []
    for aa in range(groups):
        w = None
        for hd in range(PEER_HEADS):
            picked = r2_ref[hd] < n1_ref[hd, aa:aa + 1, :].astype(jnp.bfloat16)
            term = jnp.where(picked, e2_ref[hd], 0.0) * c1_ref[hd, aa:aa + 1, :].astype(jnp.bfloat16)
            w = term if w is None else w + term
        g = _gelu_exact(act[aa * N_KEYS:(aa + 1) * N_KEYS]).astype(jnp.bfloat16)
        parts.append(w * g)
    p = jnp.concatenate(parts, axis=0)
    y_ref[...] += lax.dot_general(p, v_ref[...], (((0,), (0,)), ((), ())),
                                  preferred_element_type=jnp.float32)


def peer_pallas(h, wqT, k1, k2, u, v):
    n, d = h.shape
    t = PEER_TOKEN_TILE
    nt = n // t
    hk = (PEER_HEADS, N_KEYS)
    route = lambda dt: jax.ShapeDtypeStruct(hk + (n,), dt)
    route_spec = pl.BlockSpec(hk + (t,), lambda i: (0, 0, i))
    n1, c1, r2, e2 = pl.pallas_call(
        _peer_route_body,
        grid=(nt,),
        in_specs=[pl.BlockSpec((t, d), lambda i: (i, 0)),
                  pl.BlockSpec(wqT.shape, lambda i: (0, 0)),
                  pl.BlockSpec(k1.shape, lambda i: (0, 0, 0)),
                  pl.BlockSpec(k2.shape, lambda i: (0, 0, 0))],
        out_specs=[route_spec] * 4,
        out_shape=[route(jnp.float32), route(jnp.float32), route(jnp.bfloat16), route(jnp.bfloat16)],
        compiler_params=pltpu.CompilerParams(dimension_semantics=("parallel",),
                                             vmem_limit_bytes=VMEM_LIMIT_BYTES),
        name="peer_route",
    )(h, wqT, k1, k2)

    e = PEER_EXPERT_TILE
    groups = e // N_KEYS
    row_spec = pl.BlockSpec((PEER_HEADS, groups, t), lambda i, j: (0, j, i))
    full_spec = pl.BlockSpec(hk + (t,), lambda i, j: (0, 0, i))
    return pl.pallas_call(
        _peer_dense_body,
        grid=(nt, N_EXPERTS // e),
        in_specs=[pl.BlockSpec((t, d), lambda i, j: (i, 0)),
                  pl.BlockSpec((e, d), lambda i, j: (j, 0)),
                  pl.BlockSpec((e, d), lambda i, j: (j, 0)),
                  row_spec, row_spec, full_spec, full_spec],
        out_specs=pl.BlockSpec((t, d), lambda i, j: (i, 0)),
        out_shape=jax.ShapeDtypeStruct((n, d), jnp.float32),
        compiler_params=pltpu.CompilerParams(dimension_semantics=("parallel", "arbitrary"),
                                             vmem_limit_bytes=VMEM_LIMIT_BYTES),
        name="peer_dense",
    )(h, u, v, n1, c1, r2, e2)


ATT_TILE = 128
MASKED_BELOW = -0.5e30
KEY_PAIR = 2


def t5_bucket(dist):
    n = jnp.maximum(dist, 0)
    max_exact = N_BUCKETS // 2
    nf = jnp.maximum(n, max_exact).astype(jnp.float32)
    large = max_exact + (jnp.log(nf / max_exact) / math.log(MAX_DISTANCE / max_exact)
                         * (N_BUCKETS - max_exact)).astype(jnp.int32)
    large = jnp.minimum(large, N_BUCKETS - 1)
    return jnp.where(n < max_exact, n, large)


def _softmax_cols(s):
    m = jnp.max(s, axis=0, keepdims=True)
    e = jnp.where(s > MASKED_BELOW, jnp.exp(s - m), 0.0)
    l = jnp.sum(e, axis=0, keepdims=True)
    return e / jnp.maximum(l, TINY)


def _dot_f32_by_01(mat01, x):
    return sum(jnp.dot(mat01, part, preferred_element_type=jnp.float32) for part in _split3(x))


def _x_dot_01(x, mat01):
    return sum(jnp.dot(part, mat01, preferred_element_type=jnp.float32) for part in _split3(x))


def _online_tiles(k_ref, v_ref, qT, lo, hi, bias_of, mask_of, tile):
    width = qT.shape[1]
    tn = (((0,), (0,)), ((), ()))

    def step(j, carry):
        m, l, acc = carry
        rows = pl.ds(pl.multiple_of(j * tile, tile), tile)
        s = jnp.dot(k_ref[rows, :], qT, preferred_element_type=jnp.float32) + bias_of(j)
        s = mask_of(j, s)
        m_new = jnp.maximum(m, jnp.max(s, axis=0, keepdims=True))
        alpha = jnp.exp(m - m_new)
        e = jnp.where(s > MASKED_BELOW, jnp.exp(s - m_new), 0.0)
        l = alpha * l + jnp.sum(e, axis=0, keepdims=True)
        pv = lax.dot_general(v_ref[rows, :], e.astype(jnp.bfloat16), tn,
                             preferred_element_type=jnp.float32)
        return m_new, l, alpha * acc + pv

    init = (jnp.full((1, width), NEG_INF, jnp.float32), jnp.zeros((1, width), jnp.float32),
            jnp.zeros((HEAD_DIM, width), jnp.float32))
    m, l, acc = lax.fori_loop(lo, hi, step, init)
    return acc / jnp.maximum(l, TINY)


def _nsa_core(qT, gate_logits, q_pos, kc_ref, vc_ref, ks_ref, vs_ref, kw_ref, vw_ref, cmp_bias,
              sel_range, sel_bias, win_range, win_bias, *, tq, n_sel, n_blocks, sel_tile, win_tile):
    tn = (((0,), (0,)), ((), ()))
    p_cmp = _softmax_cols(jnp.dot(kc_ref[...], qT, preferred_element_type=jnp.float32) + cmp_bias)
    o_cmp = lax.dot_general(vc_ref[...], p_cmp.astype(jnp.bfloat16), tn, preferred_element_type=jnp.float32)
    ncp = p_cmp.shape[0]
    nb = lax.broadcasted_iota(jnp.int32, (n_blocks, ncp), 0) * SEL_BLOCK
    cs = lax.broadcasted_iota(jnp.int32, (n_blocks, ncp), 1) * CMP_STRIDE
    overlap = jnp.where((cs < nb + SEL_BLOCK) & (cs + CMP_LEN > nb), 1.0, 0.0).astype(jnp.bfloat16)
    per_head = _dot_f32_by_01(overlap, p_cmp)
    imp = sum(per_head[:, r * tq:(r + 1) * tq] for r in range(NSA_GROUP))
    blk = lax.broadcasted_iota(jnp.int32, (n_blocks, tq), 0)
    cur = q_pos // SEL_BLOCK
    forced = (blk == 0) | (blk == cur) | (blk == cur - 1)
    score = jnp.where(blk <= cur, jnp.where(forced, FORCE_SCORE, imp), -1.0)
    rank = jnp.zeros((n_blocks, tq), jnp.float32)
    for mrow in range(n_blocks):
        row = score[mrow:mrow + 1, :]
        ahead = (row > score) | ((row == score) & (blk > mrow))
        rank = rank + jnp.where(ahead, 1.0, 0.0)
    sel = jnp.where(rank < n_sel, 1.0, 0.0).astype(jnp.bfloat16)
    sel = jnp.concatenate([sel] * NSA_GROUP, axis=1)

    def sel_mask(j, s):
        kb = (j * sel_tile + lax.broadcasted_iota(jnp.int32, (sel_tile, n_blocks), 0)) // SEL_BLOCK
        nn = lax.broadcasted_iota(jnp.int32, (sel_tile, n_blocks), 1)
        expand = jnp.where(kb == nn, 1.0, 0.0).astype(jnp.bfloat16)
        mk = jnp.dot(expand, sel, preferred_element_type=jnp.float32)
        return jnp.where(mk > 0.5, s, NEG_INF)

    o_sel = _online_tiles(ks_ref, vs_ref, qT, sel_range[0], sel_range[1], sel_bias, sel_mask, sel_tile)
    o_win = _online_tiles(kw_ref, vw_ref, qT, win_range[0], win_range[1], win_bias, lambda j, s: s, win_tile)
    g = 1.0 / (1.0 + jnp.exp(-gate_logits))
    return g[0:1] * o_cmp + g[1:2] * o_sel + g[2:3] * o_win


def _nsa_prompt_body(qT_ref, gate_ref, kc_ref, vc_ref, ks_ref, vs_ref, kw_ref, vw_ref, cb_ref, tz_ref,
                     o_ref, *, n_sel):
    i = pl.program_id(2)
    tq = ATT_TILE
    q_pos = i * tq + lax.broadcasted_iota(jnp.int32, (1, tq), 1)
    masked_tile = tz_ref.shape[0] - 1

    def pair_bias(max_delta):
        def bias(jj):
            tiles = []
            for j in (KEY_PAIR * jj + t for t in range(KEY_PAIR)):
                idx = jnp.where(j > i, masked_tile, jnp.minimum(i - j, max_delta))
                tiles.append(tz_ref[idx])
            return jnp.concatenate(tiles, axis=0)
        return bias

    first_win = jnp.maximum(i - WINDOW // ATT_TILE, 0)
    o_ref[...] = _nsa_core(
        qT_ref[...], gate_ref[...], q_pos, kc_ref, vc_ref, ks_ref, vs_ref, kw_ref, vw_ref, cb_ref[...],
        (0, i // KEY_PAIR + 1), pair_bias(2),
        (first_win // KEY_PAIR, i // KEY_PAIR + 1), pair_bias(masked_tile),
        tq=tq, n_sel=n_sel, n_blocks=ks_ref.shape[0] // SEL_BLOCK,
        sel_tile=KEY_PAIR * ATT_TILE, win_tile=KEY_PAIR * ATT_TILE)


def _bias_lanes(tbl, dist, ok):
    onehot = (t5_bucket(dist)[..., None] == jnp.arange(N_BUCKETS)).astype(jnp.float32)
    vals = jnp.dot(onehot.reshape(-1, N_BUCKETS), tbl.reshape(N_BUCKETS, -1),
                   precision=lax.Precision.HIGHEST).reshape(dist.shape + tbl.shape[1:])
    b = jnp.where(ok[..., None, None], vals, NEG_INF)
    nd = b.ndim
    b = jnp.moveaxis(b, (nd - 2, nd - 1), (0, nd - 2))
    return b.reshape(b.shape[:-2] + (b.shape[-2] * b.shape[-1],))


def nsa_prompt_attention(q, gate_logits, kcmp, vcmp, ks, vs, kw, vw, rel_bias):
    B, S, G, R, dh = q.shape
    t = ATT_TILE
    n_qt = S // t
    bf = jnp.bfloat16
    tbl = rel_bias.reshape(N_BUCKETS, G, R).astype(jnp.float32)
    qT = (q * (HEAD_DIM ** -0.5)).astype(bf).reshape(B, n_qt, t, G, R, dh)
    qT = qT.transpose(0, 3, 1, 5, 4, 2).reshape(B, G, n_qt, dh, R * t)
    gT = gate_logits.reshape(B, n_qt, t, G, R, 3).transpose(0, 3, 1, 5, 4, 2).reshape(B, G, n_qt, 3, R * t)
    n_cmp = kcmp.shape[1]
    ncp = _round_up(n_cmp, 8)
    padc = lambda x: jnp.pad(x.astype(bf).transpose(0, 2, 1, 3), ((0, 0), (0, 0), (0, ncp - n_cmp), (0, 0)))
    tr = lambda x: x.astype(bf).transpose(0, 2, 1, 3)
    qp = jnp.arange(S).reshape(n_qt, 1, t)
    c_idx = jnp.arange(ncp).reshape(1, ncp, 1)
    dist = qp - (c_idx * CMP_STRIDE + CMP_LEN - 1)
    cb = _bias_lanes(tbl, dist, (dist >= 0) & (c_idx < n_cmp))
    n_delta = WINDOW // t + 2
    d = (jnp.arange(n_delta).reshape(-1, 1, 1) * t + jnp.arange(t).reshape(1, 1, t)
         - jnp.arange(t).reshape(1, t, 1))
    tz = _bias_lanes(tbl, d, (d >= 0) & (d < WINDOW))
    body = functools.partial(_nsa_prompt_body, n_sel=min(SEL_TOPK, S // SEL_BLOCK))
    kv_spec = pl.BlockSpec((None, None, S, dh), lambda b, g, i: (b, g, 0, 0))
    cmp_spec = pl.BlockSpec((None, None, ncp, dh), lambda b, g, i: (b, g, 0, 0))
    oT = pl.pallas_call(
        body,
        grid=(B, G, n_qt),
        in_specs=[pl.BlockSpec((None, None, None, dh, R * t), lambda b, g, i: (b, g, i, 0, 0)),
                  pl.BlockSpec((None, None, None, 3, R * t), lambda b, g, i: (b, g, i, 0, 0)),
                  cmp_spec, cmp_spec, kv_spec, kv_spec, kv_spec, kv_spec,
                  pl.BlockSpec((None, None, ncp, R * t), lambda b, g, i: (g, i, 0, 0)),
                  pl.BlockSpec((None,) + tz.shape[1:], lambda b, g, i: (g, 0, 0, 0))],
        out_specs=pl.BlockSpec((None, None, None, dh, R * t), lambda b, g, i: (b, g, i, 0, 0)),
        out_shape=jax.ShapeDtypeStruct((B, G, n_qt, dh, R * t), jnp.float32),
        compiler_params=pltpu.CompilerParams(dimension_semantics=("parallel", "parallel", "arbitrary"),
                                             vmem_limit_bytes=VMEM_LIMIT_BYTES),
        name="nsa_prompt_attention",
    )(qT, gT, padc(kcmp), padc(vcmp), tr(ks), tr(vs), tr(kw), tr(vw), cb, tz)
    o = oT.reshape(B, G, n_qt, dh, R, t).transpose(0, 2, 5, 1, 4, 3)
    return o.reshape(B, S, G * R * dh)


SAMPLE_PAGES_PER_STEP = 4


def _softmax_lanes(s):
    m = jnp.max(s, axis=1, keepdims=True)
    e = jnp.where(s > MASKED_BELOW, jnp.exp(s - m), 0.0)
    return e, jnp.sum(e, axis=1, keepdims=True)


def _attend_lanes(qbd, k2, v2, bias, mask=None):
    s = jnp.dot(qbd, k2, preferred_element_type=jnp.float32) + bias
    if mask is not None:
        s = jnp.where(mask > 0.5, s, NEG_INF)
    e, l = _softmax_lanes(s)
    o = lax.dot_general(e.astype(jnp.bfloat16), v2, (((1,), (1,)), ((), ())),
                        preferred_element_type=jnp.float32)
    return o / jnp.maximum(l, TINY)


def _nsa_sample_body(pt_ref, qbd_ref, gate_ref, kc_ref, vc_ref, new_ref, wk_ref, wv_ref, cb_ref, sb_ref, wb_ref,
                     *rest, n_sel, tq, past, n_blocks):
    pps = SAMPLE_PAGES_PER_STEP
    k_pages, v_pages = rest[:pps], rest[pps:2 * pps]
    o_ref, ks_ref, vs_ref = rest[2 * pps:]
    step = pl.program_id(1)
    rows = ks_ref.shape[0]
    width = qbd_ref.shape[0]
    for i in range(pps):
        cols = pl.ds(pl.multiple_of((step * pps + i) * PAGE_SIZE, PAGE_SIZE), PAGE_SIZE)
        ks_ref[:, cols] = k_pages[i][...].reshape(rows, PAGE_SIZE).astype(jnp.bfloat16)
        vs_ref[:, cols] = v_pages[i][...].reshape(rows, PAGE_SIZE).astype(jnp.bfloat16)

    @pl.when(step == pl.num_programs(1) - 1)
    def _():
        qbd = qbd_ref[...]
        n_keys = ks_ref.shape[1]
        ks_ref[:, past:n_keys] = new_ref[0]
        vs_ref[:, past:n_keys] = new_ref[1]
        s_c = lax.dot_general(qbd, kc_ref[...], (((1,), (1,)), ((), ())),
                              preferred_element_type=jnp.float32) + cb_ref[...]
        e_c, l_c = _softmax_lanes(s_c)
        p_cmp = e_c / jnp.maximum(l_c, TINY)
        o_cmp = jnp.dot(p_cmp.astype(jnp.bfloat16), vc_ref[...], preferred_element_type=jnp.float32)
        ncp = p_cmp.shape[1]
        cs = lax.broadcasted_iota(jnp.int32, (ncp, LANES), 0) * CMP_STRIDE
        nb = lax.broadcasted_iota(jnp.int32, (ncp, LANES), 1) * SEL_BLOCK
        overlap_t = jnp.where((cs < nb + SEL_BLOCK) & (cs + CMP_LEN > nb), 1.0, 0.0).astype(jnp.bfloat16)
        ra = lax.broadcasted_iota(jnp.int32, (width, width), 0)
        rb = lax.broadcasted_iota(jnp.int32, (width, width), 1)
        per_group = NSA_GROUP * tq
        same_query = jnp.where((ra // per_group == rb // per_group) & (ra % tq == rb % tq), 1.0, 0.0)
        imp = _dot_f32_by_01(same_query.astype(jnp.bfloat16), _x_dot_01(p_cmp, overlap_t))
        blk = lax.broadcasted_iota(jnp.int32, (width, LANES), 1)
        cur = (past + lax.broadcasted_iota(jnp.int32, (width, LANES), 0) % tq) // SEL_BLOCK
        forced = (blk == 0) | (blk == cur) | (blk == cur - 1)
        score = jnp.where(blk <= cur, jnp.where(forced, FORCE_SCORE, imp), -1.0)
        score = jnp.where(blk < n_blocks, score, -2.0)
        score_t = score.T
        bt = lax.broadcasted_iota(jnp.int32, (LANES, width), 0)
        rank = jnp.zeros((LANES, width), jnp.float32)
        for mrow in range(n_blocks):
            row = score_t[mrow:mrow + 1, :]
            ahead = (row > score_t) | ((row == score_t) & (bt > mrow))
            rank = rank + jnp.where(ahead, 1.0, 0.0)
        sel = jnp.where(rank < n_sel, 1.0, 0.0).T.astype(jnp.bfloat16)
        kb = lax.broadcasted_iota(jnp.int32, (LANES, n_keys), 1) // SEL_BLOCK
        nn = lax.broadcasted_iota(jnp.int32, (LANES, n_keys), 0)
        expand = jnp.where(kb == nn, 1.0, 0.0).astype(jnp.bfloat16)
        mask = jnp.dot(sel, expand, preferred_element_type=jnp.float32)
        o_sel = _attend_lanes(qbd, ks_ref[...], vs_ref[...], sb_ref[...], mask)
        window = lambda ref, new: jnp.concatenate(
            [ref[...].reshape(rows, ref.shape[-1]).astype(jnp.bfloat16), new], axis=1)
        o_win = _attend_lanes(qbd, window(wk_ref, new_ref[2]), window(wv_ref, new_ref[3]), wb_ref[...])
        g = 1.0 / (1.0 + jnp.exp(-gate_ref[...]))
        o = g[:, 0:1] * o_cmp + g[:, 1:2] * o_sel + g[:, 2:3] * o_win
        r = lax.broadcasted_iota(jnp.int32, (width, rows), 0) // per_group
        c = lax.broadcasted_iota(jnp.int32, (width, rows), 1) // HEAD_DIM
        o = jnp.where(r == c, o, 0.0)
        fa = lax.broadcasted_iota(jnp.int32, (rows, LANES), 0) % HEAD_DIM
        fb = lax.broadcasted_iota(jnp.int32, (rows, LANES), 1)
        fold = jnp.where(fa == fb, 1.0, 0.0).astype(jnp.bfloat16)
        o_ref[...] = _x_dot_01(o, fold)[:, :HEAD_DIM]


def _rows_table(table):
    g, k, w = table.shape
    return table.transpose(0, 2, 1).reshape(g * w, k)


def nsa_sample_attention(q, gate_logits, kcmp, vcmp, new_kv, sk_pool, sv_pool, wk_buf, wv_buf, page_table, rel_bias):
    DB, T, G, R, dh = q.shape
    bf = jnp.bfloat16
    width = G * R * T
    rows = G * dh
    n_pages = page_table.shape[1]
    past = n_pages * PAGE_SIZE
    n_keys = past + PAGE_SIZE
    n_blocks = _round_up(past + T, SEL_BLOCK) // SEL_BLOCK
    tbl = rel_bias.reshape(N_BUCKETS, G, R).astype(jnp.float32)
    q_pos = past + jnp.arange(T)
    qg = (q * (HEAD_DIM ** -0.5)).transpose(0, 2, 3, 1, 4)
    same_group = jnp.eye(G, dtype=qg.dtype).reshape(1, G, 1, 1, G, 1)
    qbd = (qg[:, :, :, :, None, :] * same_group).astype(bf).reshape(DB, width, rows)
    gates = jnp.pad(gate_logits.transpose(0, 2, 3, 1, 4).reshape(DB, width, 3), ((0, 0), (0, 0), (0, 5)))
    n_cmp = n_blocks * SEL_BLOCK // CMP_STRIDE - CMP_LEN // CMP_STRIDE + 1
    ncp = kcmp.shape[1]
    c_idx = jnp.arange(ncp).reshape(ncp, 1)
    dist = q_pos.reshape(1, T) - (c_idx * CMP_STRIDE + CMP_LEN - 1)
    cb = _rows_table(_bias_lanes(tbl, dist, (dist >= 0) & (c_idx < n_cmp)))
    k_idx = jnp.arange(n_keys).reshape(-1, 1)
    dist = q_pos.reshape(1, T) - k_idx
    sb = _rows_table(_bias_lanes(tbl, dist, dist >= 0))
    wbuf = wk_buf.shape[1]
    n_win = wbuf + PAGE_SIZE
    w_idx = jnp.arange(n_win).reshape(-1, 1)
    w_pos = past - wbuf + w_idx
    dist = q_pos.reshape(1, T) - w_pos
    wb = _rows_table(_bias_lanes(tbl, dist, (dist >= 0) & (dist < WINDOW) & (w_pos >= 0) & (w_idx < wbuf + T)))
    new_t = lambda x: jnp.pad(x.astype(bf).transpose(0, 2, 3, 1),
                              ((0, 0), (0, 0), (0, 0), (0, PAGE_SIZE - T))).reshape(DB, rows, PAGE_SIZE)
    new = jnp.stack([new_t(x) for x in new_kv], axis=1)
    native = lambda x: x.transpose(0, 2, 3, 1)
    pps = SAMPLE_PAGES_PER_STEP
    body = functools.partial(_nsa_sample_body, n_sel=min(SEL_TOPK, n_blocks), tq=T, past=past, n_blocks=n_blocks)
    per_b = lambda shape: pl.BlockSpec((None,) + shape, lambda b, s, pt: (b,) + (0,) * len(shape))
    const = lambda x: pl.BlockSpec(x.shape, lambda b, s, pt: (0,) * x.ndim)
    page = lambda i: pl.BlockSpec((None, G, dh, PAGE_SIZE), lambda b, s, pt: (pt[b, s * pps + i], 0, 0, 0))
    sk_t, sv_t = native(sk_pool), native(sv_pool)
    o = pl.pallas_call(
        body,
        grid_spec=pltpu.PrefetchScalarGridSpec(
            num_scalar_prefetch=1,
            grid=(DB, n_pages // pps),
            in_specs=[per_b((width, rows)), per_b((width, 8)), per_b((ncp, rows)), per_b((ncp, rows)),
                      per_b((4, rows, PAGE_SIZE)), per_b((G, dh, wbuf)), per_b((G, dh, wbuf)),
                      const(cb), const(sb), const(wb)]
                     + [page(i) for i in range(pps)] * 2,
            out_specs=per_b((width, dh)),
            scratch_shapes=[pltpu.VMEM((rows, n_keys), bf), pltpu.VMEM((rows, n_keys), bf)]),
        out_shape=jax.ShapeDtypeStruct((DB, width, dh), jnp.float32),
        compiler_params=pltpu.CompilerParams(dimension_semantics=("parallel", "arbitrary"),
                                             vmem_limit_bytes=VMEM_LIMIT_BYTES),
        name="nsa_sample_attention",
    )(page_table, qbd, gates, kcmp, vcmp, new, native(wk_buf), native(wv_buf), cb, sb, wb,
      *([sk_t] * pps), *([sv_t] * pps))
    return o.reshape(DB, G, R, T, dh).transpose(0, 3, 1, 2, 4).reshape(DB, T, G * R * dh)


FOX_Q_TILE = 512
FOX_K_TILE = 256


def _softmax_step(s, v, carry, masked):
    m, l, acc = carry
    m_new = jnp.maximum(m, jnp.max(s, axis=0, keepdims=True))
    alpha = jnp.exp(m - m_new)
    e = jnp.exp(s - m_new)
    if masked:
        e = jnp.where(s > MASKED_BELOW, e, 0.0)
    l = alpha * l + jnp.sum(e, axis=0, keepdims=True)
    pv = lax.dot_general(v, e.astype(jnp.bfloat16), (((0,), (0,)), ((), ())),
                         preferred_element_type=jnp.float32)
    return m_new, l, alpha * acc + pv


def _fox_prompt_body(qT_ref, qd_ref, k_ref, kd_ref, v_ref, o_ref):
    i = pl.program_id(2)
    qT = qT_ref[...]
    q_decay = qd_ref[...]
    width = qT.shape[1]

    def tile(j):
        rows = pl.ds(pl.multiple_of(j * FOX_K_TILE, FOX_K_TILE), FOX_K_TILE)
        s = jnp.dot(k_ref[rows, :], qT, preferred_element_type=jnp.float32)
        s = s + jnp.dot(kd_ref[rows, :], q_decay, preferred_element_type=jnp.float32)
        return s, v_ref[rows, :]

    def full_step(j, carry):
        s, v = tile(j)
        return _softmax_step(s, v, carry, masked=False)

    carry = (jnp.full((1, width), NEG_INF, jnp.float32), jnp.zeros((1, width), jnp.float32),
             jnp.zeros((HEAD_DIM, width), jnp.float32))
    ratio = FOX_Q_TILE // FOX_K_TILE
    carry = lax.fori_loop(0, i * ratio, full_step, carry)
    q_pos = i * FOX_Q_TILE + lax.broadcasted_iota(jnp.int32, (FOX_K_TILE, width), 1)
    for dj in range(ratio):
        j = i * ratio + dj
        s, v = tile(j)
        k_pos = j * FOX_K_TILE + lax.broadcasted_iota(jnp.int32, (FOX_K_TILE, width), 0)
        s = jnp.where(k_pos <= q_pos, s, NEG_INF)
        carry = _softmax_step(s, v, carry, masked=True)
    m, l, acc = carry
    o_ref[...] = acc / jnp.maximum(l, TINY)


DECAY_COLS = 8


def fox_prompt_attention(q, k, v, logf):
    B, S, H, dh = q.shape
    bf = jnp.bfloat16
    c = jnp.cumsum(logf, axis=1).transpose(0, 2, 1)
    hi, mid, lo = _split3_outside_kernel(c)
    one, zero = jnp.ones_like(hi), jnp.zeros_like(hi)
    k_decay = jnp.stack([-hi, -mid, -lo, one, one, one, zero, zero], axis=-1)
    q_decay = jnp.stack([one, one, one, hi, mid, lo, zero, zero], axis=2)
    kb = k.astype(bf).transpose(0, 2, 1, 3)
    qT = (q * (dh ** -0.5)).astype(bf).transpose(0, 2, 3, 1)
    vb = v.astype(bf).transpose(0, 2, 1, 3)
    per_head = lambda rows: pl.BlockSpec((None, None, S, rows), lambda b, h, i: (b, h, 0, 0))
    per_tile = lambda rows: pl.BlockSpec((None, None, rows, FOX_Q_TILE), lambda b, h, i: (b, h, 0, i))
    oT = pl.pallas_call(
        _fox_prompt_body,
        grid=(B, H, S // FOX_Q_TILE),
        in_specs=[per_tile(dh), per_tile(DECAY_COLS), per_head(dh), per_head(DECAY_COLS), per_head(dh)],
        out_specs=per_tile(dh),
        out_shape=jax.ShapeDtypeStruct((B, H, dh, S), jnp.float32),
        compiler_params=pltpu.CompilerParams(dimension_semantics=("parallel", "parallel", "arbitrary"),
                                             vmem_limit_bytes=VMEM_LIMIT_BYTES),
        name="fox_prompt_attention",
    )(qT, q_decay, kb, k_decay, vb)
    return oT.transpose(0, 3, 1, 2).reshape(B, S, H * dh)


FOX_PAGES_PER_STEP = 2


def _fox_sample_body(pt_ref, qbd_ref, bp_ref, kn_ref, vn_ref, bn_ref, *rest, n_q):
    pps = FOX_PAGES_PER_STEP
    k_pages, v_pages = rest[:pps], rest[pps:2 * pps]
    o_ref, m_ref, l_ref, acc_ref = rest[2 * pps:]
    p = pl.program_id(1)
    width = qbd_ref.shape[0]
    rows = kn_ref.shape[0]

    @pl.when(p == 0)
    def _():
        m_ref[...] = jnp.full(m_ref.shape, NEG_INF, jnp.float32)
        l_ref[...] = jnp.zeros(l_ref.shape, jnp.float32)
        acc_ref[...] = jnp.zeros(acc_ref.shape, jnp.float32)

    def attend(k2, v2, bias_t):
        s_t = jnp.dot(qbd_ref[...], k2, preferred_element_type=jnp.float32).T + bias_t
        m = m_ref[...]
        m_new = jnp.maximum(m, jnp.max(s_t, axis=0, keepdims=True))
        alpha = jnp.exp(m - m_new)
        e = jnp.where(s_t > MASKED_BELOW, jnp.exp(s_t - m_new), 0.0)
        m_ref[...] = m_new
        l_ref[...] = alpha * l_ref[...] + jnp.sum(e, axis=0, keepdims=True)
        acc_ref[...] = alpha * acc_ref[...] + jnp.dot(v2, e.astype(jnp.bfloat16),
                                                      preferred_element_type=jnp.float32)

    pages = lambda refs: jnp.concatenate(
        [ref[...].reshape(rows, PAGE_SIZE).astype(jnp.bfloat16) for ref in refs], axis=1)
    attend(pages(k_pages), pages(v_pages), bp_ref[...].reshape(pps * PAGE_SIZE, width))

    @pl.when(p == pl.num_programs(1) - 1)
    def _():
        attend(kn_ref[...], vn_ref[...], bn_ref[...])
        out = acc_ref[...] / jnp.maximum(l_ref[...], TINY)
        r = lax.broadcasted_iota(jnp.int32, (rows, width), 0) // HEAD_DIM
        c = lax.broadcasted_iota(jnp.int32, (rows, width), 1) // n_q
        out = jnp.where(r == c, out, 0.0)
        la = lax.broadcasted_iota(jnp.int32, (width, width), 0) % n_q
        lb = lax.broadcasted_iota(jnp.int32, (width, width), 1)
        gather_q = jnp.where(la == lb, 1.0, 0.0).astype(jnp.bfloat16)
        o_ref[...] = _x_dot_01(out, gather_q)[:, :n_q]


def fox_sample_attention(q, k, v, logf, k_pool, v_pool, f_pool, page_table):
    DB, T, H, dh = q.shape
    n_pages = page_table.shape[1]
    past = n_pages * PAGE_SIZE
    bf = jnp.bfloat16
    width = H * T
    rows = H * dh
    f_all = jnp.concatenate([f_pool[page_table].reshape(DB, past, H), logf], axis=1)
    c = jnp.cumsum(f_all, axis=1)
    c_q = c[:, past:].transpose(0, 2, 1).reshape(DB, 1, 1, width)
    c_k = jnp.repeat(c[:, :past].reshape(DB, n_pages, PAGE_SIZE, H), T, axis=-1)
    bias_past = c_q - c_k
    tok = jnp.arange(PAGE_SIZE).reshape(1, PAGE_SIZE, 1)
    qi = (jnp.arange(width) % T).reshape(1, 1, width)
    c_new = jnp.pad(jnp.repeat(c[:, past:], T, axis=-1), ((0, 0), (0, PAGE_SIZE - T), (0, 0)))
    bias_new = jnp.where((tok <= qi) & (tok < T), c_q[:, 0] - c_new, NEG_INF)
    qh = (q * (dh ** -0.5)).transpose(0, 2, 1, 3)
    same_head = jnp.eye(H, dtype=qh.dtype).reshape(1, H, 1, H, 1)
    qbd = (qh[:, :, :, None, :] * same_head).astype(bf).reshape(DB, width, rows)
    new_t = lambda x: jnp.pad(x.astype(bf).transpose(0, 2, 3, 1),
                              ((0, 0), (0, 0), (0, 0), (0, PAGE_SIZE - T))).reshape(DB, rows, PAGE_SIZE)
    pool_t = lambda x: x.transpose(0, 2, 3, 1)
    body = functools.partial(_fox_sample_body, n_q=T)
    pps = FOX_PAGES_PER_STEP
    page = lambda i: pl.BlockSpec((None, H, dh, PAGE_SIZE), lambda b, p, pt: (pt[b, p * pps + i], 0, 0, 0))
    per_b = lambda shape: pl.BlockSpec((None,) + shape, lambda b, p, pt: (b,) + (0,) * len(shape))
    kt, vt = pool_t(k_pool), pool_t(v_pool)
    o = pl.pallas_call(
        body,
        grid_spec=pltpu.PrefetchScalarGridSpec(
            num_scalar_prefetch=1,
            grid=(DB, n_pages // pps),
            in_specs=[per_b((width, rows)),
                      pl.BlockSpec((None, pps, PAGE_SIZE, width), lambda b, p, pt: (b, p, 0, 0)),
                      per_b((rows, PAGE_SIZE)), per_b((rows, PAGE_SIZE)), per_b((PAGE_SIZE, width))]
                     + [page(i) for i in range(pps)] * 2,
            out_specs=per_b((rows, T)),
            scratch_shapes=[pltpu.VMEM((1, width), jnp.float32), pltpu.VMEM((1, width), jnp.float32),
                            pltpu.VMEM((rows, width), jnp.float32)]),
        out_shape=jax.ShapeDtypeStruct((DB, rows, T), jnp.float32),
        compiler_params=pltpu.CompilerParams(dimension_semantics=("parallel", "arbitrary"),
                                             vmem_limit_bytes=VMEM_LIMIT_BYTES),
        name="fox_sample_attention",
    )(page_table, qbd, bias_past, new_t(k), new_t(v), bias_new, *([kt] * pps), *([vt] * pps))
    return o.reshape(DB, H, dh, T).transpose(0, 3, 1, 2).reshape(DB, T, H * dh)


def _compress_body(pt_ref, newk_ref, newv_ref, pe_ref, w1_ref, w2_ref, *rest, n_tok, n_out):
    pps = SAMPLE_PAGES_PER_STEP
    k_pages, v_pages = rest[:pps], rest[pps:2 * pps]
    ko_ref, vo_ref, xk_ref, xv_ref, flat_ref = rest[2 * pps:]
    step = pl.program_id(1)
    rows = newk_ref.shape[1]
    pairs = rows // LANES
    for i in range(pps):
        dst = pl.ds(pl.multiple_of((step * pps + i) * PAGE_SIZE, PAGE_SIZE), PAGE_SIZE)
        for x_ref, pages in ((xk_ref, k_pages), (xv_ref, v_pages)):
            page_t = pages[i][...].reshape(rows, PAGE_SIZE).T
            for gp in range(pairs):
                x_ref[gp, dst, :] = page_t[:, gp * LANES:(gp + 1) * LANES]

    @pl.when(step == pl.num_programs(1) - 1)
    def _():
        n_new = newk_ref.shape[0]
        tail = xk_ref.shape[1] - n_tok - n_new
        for which, (x_ref, new_ref, o_ref) in enumerate(((xk_ref, newk_ref, ko_ref), (xv_ref, newv_ref, vo_ref))):
            for gp in range(pairs):
                x_ref[gp, n_tok:n_tok + n_new, :] = new_ref[:, gp * LANES:(gp + 1) * LANES]
                x_ref[gp, n_tok + n_new:, :] = jnp.zeros((tail, LANES), jnp.float32)
            for l in range(CMP_LEN):
                for gp in range(pairs):
                    blk = x_ref[gp, pl.ds(l, n_out, stride=CMP_STRIDE), :]
                    blk = blk + pe_ref[which, l:l + 1, :]
                    flat_ref[gp * n_out:(gp + 1) * n_out, l * LANES:(l + 1) * LANES] = blk.astype(jnp.bfloat16)
            hidden = _gelu_exact(jnp.dot(flat_ref[...], w1_ref[which], preferred_element_type=jnp.float32))
            out = jnp.dot(hidden.astype(jnp.bfloat16), w2_ref[which], preferred_element_type=jnp.float32)
            o_ref[...] = jnp.concatenate([out[gp * n_out:(gp + 1) * n_out] for gp in range(pairs)],
                                         axis=1).astype(o_ref.dtype)


def _pair_weights(w1, w2, pe):
    dh = w2.shape[0]
    eye2 = jnp.eye(2, dtype=w1.dtype)
    w1p = jnp.einsum('lde,pq->lpdqe', w1.reshape(CMP_LEN, dh, dh), eye2).reshape(CMP_LEN * 2 * dh, 2 * dh)
    w2p = jnp.einsum('de,pq->pdqe', w2, eye2).reshape(2 * dh, 2 * dh)
    pep = jnp.concatenate([pe, pe], axis=1)
    return w1p, w2p, pep


def compress_paged(k_pool, v_pool, new_k, new_v, page_table, weights_k, weights_v):
    DB, T, G, dh = new_k.shape
    bf = jnp.bfloat16
    rows = G * dh
    n_pages = page_table.shape[1]
    past = n_pages * PAGE_SIZE
    lp = _round_up(past + T, SEL_BLOCK)
    n_cmp = lp // CMP_STRIDE - CMP_LEN // CMP_STRIDE + 1
    n_out = _round_up(n_cmp, 8)
    x_rows = _round_up(CMP_LEN + CMP_STRIDE * (n_out - 1), PAGE_SIZE)
    wk = _pair_weights(weights_k[1], weights_k[2], weights_k[0])
    wv = _pair_weights(weights_v[1], weights_v[2], weights_v[0])
    w1 = jnp.stack([wk[0], wv[0]]).astype(bf)
    w2 = jnp.stack([wk[1], wv[1]]).astype(bf)
    pe = jnp.stack([wk[2], wv[2]])
    native = lambda x: x.transpose(0, 2, 3, 1)
    pps = SAMPLE_PAGES_PER_STEP
    body = functools.partial(_compress_body, n_tok=past, n_out=n_out)
    per_b = lambda shape: pl.BlockSpec((None,) + shape, lambda b, s, pt: (b,) + (0,) * len(shape))
    const = lambda x: pl.BlockSpec(x.shape, lambda b, s, pt: (0,) * x.ndim)
    page = lambda i: pl.BlockSpec((None, G, dh, PAGE_SIZE), lambda b, s, pt: (pt[b, s * pps + i], 0, 0, 0))
    kt, vt = native(k_pool), native(v_pool)
    out_shape = jax.ShapeDtypeStruct((DB, n_out, rows), bf)
    return pl.pallas_call(
        body,
        grid_spec=pltpu.PrefetchScalarGridSpec(
            num_scalar_prefetch=1,
            grid=(DB, n_pages // pps),
            in_specs=[per_b((T, rows)), per_b((T, rows)), const(pe), const(w1), const(w2)]
                     + [page(i) for i in range(pps)] * 2,
            out_specs=[per_b((n_out, rows)), per_b((n_out, rows))],
            scratch_shapes=[pltpu.VMEM((rows // LANES, x_rows, LANES), jnp.float32),
                            pltpu.VMEM((rows // LANES, x_rows, LANES), jnp.float32),
                            pltpu.VMEM((rows // LANES * n_out, CMP_LEN * LANES), bf)]),
        out_shape=[out_shape, out_shape],
        compiler_params=pltpu.CompilerParams(dimension_semantics=("parallel", "arbitrary"),
                                             vmem_limit_bytes=VMEM_LIMIT_BYTES),
        name="nsa_compress_paged",
    )(page_table, new_k.reshape(DB, T, rows), new_v.reshape(DB, T, rows), pe, w1, w2, *([kt] * pps), *([vt] * pps))


def compress(x, pe, w1, w2):
    B, T, G, dh = x.shape
    r = CMP_LEN // CMP_STRIDE
    n_chunks = T // CMP_STRIDE
    nc = n_chunks - r + 1
    ch = x.reshape(B, n_chunks, CMP_STRIDE, G, dh)
    blk = jnp.concatenate([ch[:, m:m + nc] for m in range(r)], axis=2)
    blk = blk + pe[:, None, :]
    flat = blk.transpose(0, 1, 3, 2, 4).reshape(B, nc, G, CMP_LEN * dh)
    return jax.nn.gelu(flat @ w1, approximate=False) @ w2


def nsa_split(proj):
    B, T, _ = proj.shape
    nq = NSA_HEADS * HEAD_DIM
    nkv = NSA_KV_HEADS * HEAD_DIM
    q = proj[..., :nq].reshape(B, T, NSA_KV_HEADS, NSA_GROUP, HEAD_DIM)
    kv = proj[..., nq:nq + 6 * nkv].reshape(B, T, 6, NSA_KV_HEADS, HEAD_DIM)
    gl = proj[..., nq + 6 * nkv:nq + 6 * nkv + 3 * NSA_HEADS].reshape(B, T, NSA_KV_HEADS, NSA_GROUP, 3)
    return q, gl, [kv[:, :, s] for s in range(6)]


def nsa_prompt(proj, cw, rel_bias):
    S = proj.shape[1]
    pe_k, w1_k, w2_k, pe_v, w1_v, w2_v = cw
    q, gl, (kc, vc, ks, vs, kw, vw) = nsa_split(proj)
    kcmp = compress(kc, pe_k, w1_k, w2_k)
    vcmp = compress(vc, pe_v, w1_v, w2_v)
    o = nsa_prompt_attention(q, gl, kcmp, vcmp, ks, vs, kw, vw, rel_bias)
    wb = min(WINDOW, S)
    return o, [kc, vc, ks, vs, kw[:, S - wb:], vw[:, S - wb:]]


def nsa_sample(proj, cw, rel_bias, ck_pool, cv_pool, sk_pool, sv_pool, wk_buf, wv_buf, page_table):
    DB, T, _ = proj.shape
    pe_k, w1_k, w2_k, pe_v, w1_v, w2_v = cw
    q, gl, (kc, vc, ks, vs, kw, vw) = nsa_split(proj)
    kcmp, vcmp = compress_paged(ck_pool, cv_pool, kc, vc, page_table, (pe_k, w1_k, w2_k), (pe_v, w1_v, w2_v))
    kw_all = jnp.concatenate([wk_buf, kw], axis=1)
    vw_all = jnp.concatenate([wv_buf, vw], axis=1)
    o = nsa_sample_attention(q, gl, kcmp, vcmp, (ks, vs, kw, vw), sk_pool, sv_pool, wk_buf, wv_buf,
                             page_table, rel_bias)
    return o, [kc, vc, ks, vs, kw_all[:, T:], vw_all[:, T:]]


def kernel(x_prompt, x_sample, p_prompt, p_sample, cache_fox_k, cache_fox_v, cache_fox_logf,
           cache_nsa_cmp_k, cache_nsa_cmp_v, cache_nsa_sel_k, cache_nsa_sel_v,
           cache_nsa_win_k, cache_nsa_win_v, page_table,
           norm_mix, norm_ffn, norm_ple, norm_final,
           fox_w_in, fox_b_f, fox_w_out, nsa_w_in, nsa_w_out,
           cmp_pe_k, cmp_w1_k, cmp_w2_k, cmp_pe_v, cmp_w1_v, cmp_w2_v, rel_bias,
           peer_wq, peer_sub_k1, peer_sub_k2, peer_u, peer_v, ple_w_proj, ple_w_gate):
    B, S, d = x_prompt.shape
    DB, T, _ = x_sample.shape
    n_p, n_s = B * S, DB * T
    bf = jnp.bfloat16
    rows = lambda a, b: jnp.concatenate([a.reshape(n_p, -1), b.reshape(n_s, -1)], axis=0)
    x = rows(x_prompt, x_sample)
    fox_p, fox_s, nsa_p, nsa_s = [], [], [], []
    for i in range(DEPTH):
        j = i // 2
        if i % 2 == 0:
            h = rmsnorm_pallas(x, norm_mix[i], bf)
            proj = linear_pallas(h, fox_w_in[j])
            nh = FOX_HEADS * HEAD_DIM
            logf = jax.nn.log_sigmoid(proj[:, 3 * nh:3 * nh + FOX_HEADS] + fox_b_f[j])
            heads = lambda a, lead: a.reshape(lead + (FOX_HEADS, -1))
            qp, kp, vp = (heads(proj[:n_p, s * nh:(s + 1) * nh], (B, S)) for s in range(3))
            qs, ks_, vs_ = (heads(proj[n_p:, s * nh:(s + 1) * nh], (DB, T)) for s in range(3))
            fp, fs = logf[:n_p].reshape(B, S, FOX_HEADS), logf[n_p:].reshape(DB, T, FOX_HEADS)
            op = fox_prompt_attention(qp, kp, vp, fp)
            os_ = fox_sample_attention(qs, ks_, vs_, fs, cache_fox_k[j], cache_fox_v[j], cache_fox_logf[j],
                                       page_table)
            fox_p.append([kp, vp, fp])
            fox_s.append([ks_, vs_, fs])
            y = linear_pallas(rows(op, os_), fox_w_out[j])
        else:
            cw = (cmp_pe_k[j], cmp_w1_k[j], cmp_w2_k[j], cmp_pe_v[j], cmp_w1_v[j], cmp_w2_v[j])
            h = rmsnorm_pallas(x, norm_mix[i], bf)
            proj = linear_pallas(h, nsa_w_in[j])
            op, stp = nsa_prompt(proj[:n_p].reshape(B, S, -1), cw, rel_bias)
            os_, sts = nsa_sample(proj[n_p:].reshape(DB, T, -1), cw, rel_bias,
                                  cache_nsa_cmp_k[j], cache_nsa_cmp_v[j], cache_nsa_sel_k[j],
                                  cache_nsa_sel_v[j], cache_nsa_win_k[j], cache_nsa_win_v[j], page_table)
            nsa_p.append(stp)
            nsa_s.append(sts)
            y = linear_pallas(rows(op, os_), nsa_w_out[j])
        x = x + y
        h = rmsnorm_pallas(x, norm_ffn[i], bf)
        x = x + peer_pallas(h, peer_wq[i].T.astype(bf), peer_sub_k1[i].astype(bf), peer_sub_k2[i].astype(bf),
                            peer_u[i].astype(bf), peer_v[i].astype(bf))
        x = ple_pallas(x, rows(p_prompt[i], p_sample[i]), norm_ple[i], ple_w_proj[i], ple_w_gate[i])
    y = rmsnorm_pallas(x, norm_final, jnp.float32)
    st = lambda lst, k: jnp.stack([s[k] for s in lst])
    return (y[:n_p].reshape(B, S, d), y[n_p:].reshape(DB, T, d),
            st(fox_p, 0), st(fox_p, 1), st(fox_p, 2),
            st(fox_s, 0), st(fox_s, 1), st(fox_s, 2),
            st(nsa_p, 0), st(nsa_p, 1), st(nsa_p, 2), st(nsa_p, 3), st(nsa_p, 4), st(nsa_p, 5),
            st(nsa_s, 0), st(nsa_s, 1), st(nsa_s, 2), st(nsa_s, 3), st(nsa_s, 4), st(nsa_s, 5))
```

```python
import functools
import math

import jax
import jax.numpy as jnp
from jax import lax
from jax.experimental import pallas as pl
from jax.experimental.pallas import tpu as pltpu

D_MODEL = 1024
DEPTH = 2
PAGE_SIZE = 128
HEAD_DIM = 64
FOX_HEADS = D_MODEL // HEAD_DIM
NSA_HEADS = D_MODEL // HEAD_DIM
NSA_KV_HEADS = 4
NSA_GROUP = NSA_HEADS // NSA_KV_HEADS
CMP_LEN = 32
CMP_STRIDE = 16
SEL_BLOCK = 64
SEL_TOPK = 16
WINDOW = 512
N_BUCKETS = 32
MAX_DISTANCE = 128
PEER_HEADS = 8
PEER_TOPK = 16
N_KEYS = 128
N_EXPERTS = N_KEYS * N_KEYS
PEER_DKEY = 256
FORCE_SCORE = 1e4
RMS_EPS = 1e-6
NEG_INF = -1e30
TINY = 1e-30

VMEM_LIMIT_BYTES = 56 * 1024 * 1024
LANES = 128
ROW_TILE = 512
MAX_COL_TILE = 1024


def _round_up(x, m):
    return -(-x // m) * m


def _split3(x):
    hi = x.astype(jnp.bfloat16)
    r1 = x - hi.astype(jnp.float32)
    mid = r1.astype(jnp.bfloat16)
    lo = (r1 - mid.astype(jnp.float32)).astype(jnp.bfloat16)
    return hi, mid, lo


def _split3_outside_kernel(x):
    to_bf16 = lambda a: lax.reduce_precision(a, exponent_bits=8, mantissa_bits=7)
    hi = to_bf16(x)
    r1 = x - hi
    mid = to_bf16(r1)
    lo = to_bf16(r1 - mid)
    return hi.astype(jnp.bfloat16), mid.astype(jnp.bfloat16), lo.astype(jnp.bfloat16)


def _rms(x, g):
    return x * lax.rsqrt(jnp.mean(x * x, axis=-1, keepdims=True) + RMS_EPS) * g


def _rmsnorm_body(x_ref, g_ref, o_ref):
    o_ref[...] = _rms(x_ref[...], g_ref[...]).astype(o_ref.dtype)


def rmsnorm_pallas(x, g, out_dtype):
    n, d = x.shape
    return pl.pallas_call(
        _rmsnorm_body,
        grid=(n // ROW_TILE,),
        in_specs=[pl.BlockSpec((ROW_TILE, d), lambda i: (i, 0)),
                  pl.BlockSpec((1, d), lambda i: (0, 0))],
        out_specs=pl.BlockSpec((ROW_TILE, d), lambda i: (i, 0)),
        out_shape=jax.ShapeDtypeStruct((n, d), out_dtype),
        compiler_params=pltpu.CompilerParams(dimension_semantics=("parallel",)),
        name="rmsnorm",
    )(x, g.reshape(1, d))


def _linear_body(x_ref, w_ref, o_ref):
    o_ref[...] = jnp.dot(x_ref[...].astype(jnp.bfloat16), w_ref[...], preferred_element_type=jnp.float32)


def linear_pallas(x, w):
    n, k = x.shape
    m = w.shape[1]
    mp = _round_up(m, LANES)
    tn = max(t for t in range(LANES, MAX_COL_TILE + 1, LANES) if mp % t == 0)
    wb = jnp.pad(w.astype(jnp.bfloat16), ((0, 0), (0, mp - m)))
    return pl.pallas_call(
        _linear_body,
        grid=(n // ROW_TILE, mp // tn),
        in_specs=[pl.BlockSpec((ROW_TILE, k), lambda i, j: (i, 0)),
                  pl.BlockSpec((k, tn), lambda i, j: (0, j))],
        out_specs=pl.BlockSpec((ROW_TILE, tn), lambda i, j: (i, j)),
        out_shape=jax.ShapeDtypeStruct((n, mp), jnp.float32),
        compiler_params=pltpu.CompilerParams(dimension_semantics=("parallel", "parallel"),
                                             vmem_limit_bytes=VMEM_LIMIT_BYTES),
        name="linear",
    )(x, wb)


def _ple_body(x_ref, p_ref, g_ref, wg_ref, wp_ref, o_ref):
    x = x_ref[...]
    h = _rms(x, g_ref[...]).astype(jnp.bfloat16)
    gate = 1.0 / (1.0 + jnp.exp(-jnp.dot(h, wg_ref[...], preferred_element_type=jnp.float32)))
    proj = jnp.dot(p_ref[...].astype(jnp.bfloat16), wp_ref[...], preferred_element_type=jnp.float32)
    o_ref[...] = x + gate * proj


def ple_pallas(x, p, g, w_proj, w_gate):
    n, d = x.shape
    dp = p.shape[1]
    return pl.pallas_call(
        _ple_body,
        grid=(n // ROW_TILE,),
        in_specs=[pl.BlockSpec((ROW_TILE, d), lambda i: (i, 0)),
                  pl.BlockSpec((ROW_TILE, dp), lambda i: (i, 0)),
                  pl.BlockSpec((1, d), lambda i: (0, 0)),
                  pl.BlockSpec((d, d), lambda i: (0, 0)),
                  pl.BlockSpec((dp, d), lambda i: (0, 0))],
        out_specs=pl.BlockSpec((ROW_TILE, d), lambda i: (i, 0)),
        out_shape=jax.ShapeDtypeStruct((n, d), jnp.float32),
        compiler_params=pltpu.CompilerParams(dimension_semantics=("parallel",),
                                             vmem_limit_bytes=VMEM_LIMIT_BYTES),
        name="ple",
    )(x, p, g.reshape(1, d), w_gate.astype(jnp.bfloat16), w_proj.astype(jnp.bfloat16))


PEER_TOKEN_TILE = 512
PEER_EXPERT_TILE = 1024
SQRT_HALF = 0.7071067811865476


def _gelu_exact(x):
    return 0.5 * x * (1.0 + lax.erf(x * SQRT_HALF))


def _top_rows(x, k):
    vals = []
    for _ in range(k):
        m = jnp.max(x, axis=0, keepdims=True)
        vals.append(m)
        x = jnp.where(x == m, NEG_INF, x)
    return vals


def _peer_route_body(h_ref, wqT_ref, k1_ref, k2_ref, n1_ref, c1_ref, r2_ref, e2_ref):
    h = h_ref[...]
    half = PEER_DKEY // 2
    nt = (((1,), (1,)), ((), ()))
    for hd in range(PEER_HEADS):
        qv = lax.dot_general(wqT_ref[hd * PEER_DKEY:(hd + 1) * PEER_DKEY, :], h, nt,
                             preferred_element_type=jnp.float32)
        s1 = jnp.dot(k1_ref[hd], qv[:half].astype(jnp.bfloat16), preferred_element_type=jnp.float32)
        s2 = jnp.dot(k2_ref[hd], qv[half:].astype(jnp.bfloat16), preferred_element_type=jnp.float32)
        v1 = _top_rows(s1, PEER_TOPK)
        v2 = _top_rows(s2, PEER_TOPK)
        v2_stack = jnp.concatenate(v2, axis=0)
        blocks = []
        for p in range(PEER_TOPK):
            n_p = PEER_TOPK // (p + 1)
            rows = -(-n_p // 8) * 8
            blk = v1[p] + v2_stack[:rows]
            if n_p < rows:
                r = lax.broadcasted_iota(jnp.int32, blk.shape, 0)
                blk = jnp.where(r < n_p, blk, NEG_INF)
            blocks.append(blk)
        c = _top_rows(jnp.concatenate(blocks, axis=0), PEER_TOPK)
        z = jnp.ones_like(c[0])
        for kk in range(1, PEER_TOPK):
            z = z + jnp.exp(c[kk] - c[0])
        tau = c[PEER_TOPK - 1]
        n1 = jnp.zeros_like(s1)
        r2 = jnp.zeros_like(s2)
        for q in range(PEER_TOPK):
            n1 = n1 + jnp.where(s1 + v2[q] >= tau, 1.0, 0.0)
            r2 = r2 + jnp.where(v2[q] > s2, 1.0, 0.0)
        n1_ref[hd] = n1
        c1_ref[hd] = jnp.exp(s1 - v1[0]) / z
        r2_ref[hd] = r2.astype(jnp.bfloat16)
        e2_ref[hd] = jnp.exp(s2 - v2[0]).astype(jnp.bfloat16)


def _peer_dense_body(h_ref, u_ref, v_ref, n1_ref, c1_ref, r2_ref, e2_ref, y_ref):
    j = pl.program_id(1)

    @pl.when(j == 0)
    def _():
        y_ref[...] = jnp.zeros_like(y_ref)

    h = h_ref[...]
    act = lax.dot_general(u_ref[...], h, (((1,), (1,)), ((), ())),
                          preferred_element_type=jnp.float32)
    groups = PEER_EXPERT_TILE // N_KEYS
    parts = []
    for aa in range(groups):
        w = None
        for hd in range(PEER_HEADS):
            picked = r2_ref[hd] < n1_ref[hd, aa:aa + 1, :].astype(jnp.bfloat16)
            term = jnp.where(picked, e2_ref[hd], 0.0) * c1_ref[hd, aa:aa + 1, :].astype(jnp.bfloat16)
            w = term if w is None else w + term
        g = _gelu_exact(act[aa * N_KEYS:(aa + 1) * N_KEYS]).astype(jnp.bfloat16)
        parts.append(w * g)
    p = jnp.concatenate(parts, axis=0)
    y_ref[...] += lax.dot_general(p, v_ref[...], (((0,), (0,)), ((), ())),
                                  preferred_element_type=jnp.float32)


def peer_pallas(h, wqT, k1, k2, u, v):
    n, d = h.shape
    t = PEER_TOKEN_TILE
    nt = n // t
    hk = (PEER_HEADS, N_KEYS)
    route = lambda dt: jax.ShapeDtypeStruct(hk + (n,), dt)
    route_spec = pl.BlockSpec(hk + (t,), lambda i: (0, 0, i))
    n1, c1, r2, e2 = pl.pallas_call(
        _peer_route_body,
        grid=(nt,),
        in_specs=[pl.BlockSpec((t, d), lambda i: (i, 0)),
                  pl.BlockSpec(wqT.shape, lambda i: (0, 0)),
                  pl.BlockSpec(k1.shape, lambda i: (0, 0, 0)),
                  pl.BlockSpec(k2.shape, lambda i: (0, 0, 0))],
        out_specs=[route_spec] * 4,
        out_shape=[route(jnp.float32), route(jnp.float32), route(jnp.bfloat16), route(jnp.bfloat16)],
        compiler_params=pltpu.CompilerParams(dimension_semantics=("parallel",),
                                             vmem_limit_bytes=VMEM_LIMIT_BYTES),
        name="peer_route",
    )(h, wqT, k1, k2)

    e = PEER_EXPERT_TILE
    groups = e // N_KEYS
    row_spec = pl.BlockSpec((PEER_HEADS, groups, t), lambda i, j: (0, j, i))
    full_spec = pl.BlockSpec(hk + (t,), lambda i, j: (0, 0, i))
    return pl.pallas_call(
        _peer_dense_body,
        grid=(nt, N_EXPERTS // e),
        in_specs=[pl.BlockSpec((t, d), lambda i, j: (i, 0)),
                  pl.BlockSpec((e, d), lambda i, j: (j, 0)),
                  pl.BlockSpec((e, d), lambda i, j: (j, 0)),
                  row_spec, row_spec, full_spec, full_spec],
        out_specs=pl.BlockSpec((t, d), lambda i, j: (i, 0)),
        out_shape=jax.ShapeDtypeStruct((n, d), jnp.float32),
        compiler_params=pltpu.CompilerParams(dimension_semantics=("parallel", "arbitrary"),
                                             vmem_limit_bytes=VMEM_LIMIT_BYTES),
        name="peer_dense",
    )(h, u, v, n1, c1, r2, e2)


ATT_TILE = 128
MASKED_BELOW = -0.5e30
KEY_PAIR = 4


def t5_bucket(dist):
    n = jnp.maximum(dist, 0)
    max_exact = N_BUCKETS // 2
    nf = jnp.maximum(n, max_exact).astype(jnp.float32)
    large = max_exact + (jnp.log(nf / max_exact) / math.log(MAX_DISTANCE / max_exact)
                         * (N_BUCKETS - max_exact)).astype(jnp.int32)
    large = jnp.minimum(large, N_BUCKETS - 1)
    return jnp.where(n < max_exact, n, large)


def _softmax_cols(s):
    m = jnp.max(s, axis=0, keepdims=True)
    e = jnp.where(s > MASKED_BELOW, jnp.exp(s - m), 0.0)
    l = jnp.sum(e, axis=0, keepdims=True)
    return e / jnp.maximum(l, TINY)


def _dot_f32_by_01(mat01, x):
    return sum(jnp.dot(mat01, part, preferred_element_type=jnp.float32) for part in _split3(x))


def _x_dot_01(x, mat01):
    return sum(jnp.dot(part, mat01, preferred_element_type=jnp.float32) for part in _split3(x))


def _online_tiles(k_ref, v_ref, qT, lo, hi, bias_of, mask_of, tile):
    width = qT.shape[1]
    tn = (((0,), (0,)), ((), ()))

    def step(j, carry):
        m, l, acc = carry
        rows = pl.ds(pl.multiple_of(j * tile, tile), tile)
        s = jnp.dot(k_ref[rows, :], qT, preferred_element_type=jnp.float32) + bias_of(j)
        s = mask_of(j, s)
        m_new = jnp.maximum(m, jnp.max(s, axis=0, keepdims=True))
        alpha = jnp.exp(m - m_new)
        e = jnp.where(s > MASKED_BELOW, jnp.exp(s - m_new), 0.0)
        l = alpha * l + jnp.sum(e, axis=0, keepdims=True)
        pv = lax.dot_general(v_ref[rows, :], e.astype(jnp.bfloat16), tn,
                             preferred_element_type=jnp.float32)
        return m_new, l, alpha * acc + pv

    init = (jnp.full((1, width), NEG_INF, jnp.float32), jnp.zeros((1, width), jnp.float32),
            jnp.zeros((HEAD_DIM, width), jnp.float32))
    m, l, acc = lax.fori_loop(lo, hi, step, init)
    return acc / jnp.maximum(l, TINY)


def _nsa_core(qT, gate_logits, q_pos, kc_ref, vc_ref, ks_ref, vs_ref, kw_ref, vw_ref, cmp_bias,
              sel_range, sel_bias, win_range, win_bias, *, tq, n_sel, n_blocks, sel_tile, win_tile):
    tn = (((0,), (0,)), ((), ()))
    p_cmp = _softmax_cols(jnp.dot(kc_ref[...], qT, preferred_element_type=jnp.float32) + cmp_bias)
    o_cmp = lax.dot_general(vc_ref[...], p_cmp.astype(jnp.bfloat16), tn, preferred_element_type=jnp.float32)
    ncp = p_cmp.shape[0]
    nb = lax.broadcasted_iota(jnp.int32, (n_blocks, ncp), 0) * SEL_BLOCK
    cs = lax.broadcasted_iota(jnp.int32, (n_blocks, ncp), 1) * CMP_STRIDE
    overlap = jnp.where((cs < nb + SEL_BLOCK) & (cs + CMP_LEN > nb), 1.0, 0.0).astype(jnp.bfloat16)
    per_head = _dot_f32_by_01(overlap, p_cmp)
    imp = sum(per_head[:, r * tq:(r + 1) * tq] for r in range(NSA_GROUP))
    blk = lax.broadcasted_iota(jnp.int32, (n_blocks, tq), 0)
    cur = q_pos // SEL_BLOCK
    forced = (blk == 0) | (blk == cur) | (blk == cur - 1)
    score = jnp.where(blk <= cur, jnp.where(forced, FORCE_SCORE, imp), -1.0)
    rank = jnp.zeros((n_blocks, tq), jnp.float32)
    for mrow in range(n_blocks):
        row = score[mrow:mrow + 1, :]
        ahead = (row > score) | ((row == score) & (blk > mrow))
        rank = rank + jnp.where(ahead, 1.0, 0.0)
    sel = jnp.where(rank < n_sel, 1.0, 0.0).astype(jnp.bfloat16)
    sel = jnp.concatenate([sel] * NSA_GROUP, axis=1)

    def sel_mask(j, s):
        kb = (j * sel_tile + lax.broadcasted_iota(jnp.int32, (sel_tile, n_blocks), 0)) // SEL_BLOCK
        nn = lax.broadcasted_iota(jnp.int32, (sel_tile, n_blocks), 1)
        expand = jnp.where(kb == nn, 1.0, 0.0).astype(jnp.bfloat16)
        mk = jnp.dot(expand, sel, preferred_element_type=jnp.float32)
        return jnp.where(mk > 0.5, s, NEG_INF)

    o_sel = _online_tiles(ks_ref, vs_ref, qT, sel_range[0], sel_range[1], sel_bias, sel_mask, sel_tile)
    o_win = _online_tiles(kw_ref, vw_ref, qT, win_range[0], win_range[1], win_bias, lambda j, s: s, win_tile)
    g = 1.0 / (1.0 + jnp.exp(-gate_logits))
    return g[0:1] * o_cmp + g[1:2] * o_sel + g[2:3] * o_win


def _nsa_prompt_body(qT_ref, gate_ref, kc_ref, vc_ref, ks_ref, vs_ref, kw_ref, vw_ref, cb_ref, tz_ref,
                     o_ref, *, n_sel):
    i = pl.program_id(2)
    tq = ATT_TILE
    q_pos = i * tq + lax.broadcasted_iota(jnp.int32, (1, tq), 1)
    masked_tile = tz_ref.shape[0] - 1

    def pair_bias(max_delta):
        def bias(jj):
            tiles = []
            for j in (KEY_PAIR * jj + t for t in range(KEY_PAIR)):
                idx = jnp.where(j > i, masked_tile, jnp.minimum(i - j, max_delta))
                tiles.append(tz_ref[idx])
            return jnp.concatenate(tiles, axis=0)
        return bias

    first_win = jnp.maximum(i - WINDOW // ATT_TILE, 0)
    o_ref[...] = _nsa_core(
        qT_ref[...], gate_ref[...], q_pos, kc_ref, vc_ref, ks_ref, vs_ref, kw_ref, vw_ref, cb_ref[...],
        (0, i // KEY_PAIR + 1), pair_bias(2),
        (first_win // KEY_PAIR, i // KEY_PAIR + 1), pair_bias(masked_tile),
        tq=tq, n_sel=n_sel, n_blocks=ks_ref.shape[0] // SEL_BLOCK,
        sel_tile=KEY_PAIR * ATT_TILE, win_tile=KEY_PAIR * ATT_TILE)


def _bias_lanes(tbl, dist, ok):
    onehot = (t5_bucket(dist)[..., None] == jnp.arange(N_BUCKETS)).astype(jnp.float32)
    vals = jnp.dot(onehot.reshape(-1, N_BUCKETS), tbl.reshape(N_BUCKETS, -1),
                   precision=lax.Precision.HIGHEST).reshape(dist.shape + tbl.shape[1:])
    b = jnp.where(ok[..., None, None], vals, NEG_INF)
    nd = b.ndim
    b = jnp.moveaxis(b, (nd - 2, nd - 1), (0, nd - 2))
    return b.reshape(b.shape[:-2] + (b.shape[-2] * b.shape[-1],))


def nsa_prompt_attention(q, gate_logits, kcmp, vcmp, ks, vs, kw, vw, rel_bias):
    B, S, G, R, dh = q.shape
    t = ATT_TILE
    n_qt = S // t
    bf = jnp.bfloat16
    tbl = rel_bias.reshape(N_BUCKETS, G, R).astype(jnp.float32)
    qT = (q * (HEAD_DIM ** -0.5)).astype(bf).reshape(B, n_qt, t, G, R, dh)
    qT = qT.transpose(0, 3, 1, 5, 4, 2).reshape(B, G, n_qt, dh, R * t)
    gT = gate_logits.reshape(B, n_qt, t, G, R, 3).transpose(0, 3, 1, 5, 4, 2).reshape(B, G, n_qt, 3, R * t)
    n_cmp = kcmp.shape[1]
    ncp = _round_up(n_cmp, 8)
    padc = lambda x: jnp.pad(x.astype(bf).transpose(0, 2, 1, 3), ((0, 0), (0, 0), (0, ncp - n_cmp), (0, 0)))
    tr = lambda x: x.astype(bf).transpose(0, 2, 1, 3)
    qp = jnp.arange(S).reshape(n_qt, 1, t)
    c_idx = jnp.arange(ncp).reshape(1, ncp, 1)
    dist = qp - (c_idx * CMP_STRIDE + CMP_LEN - 1)
    cb = _bias_lanes(tbl, dist, (dist >= 0) & (c_idx < n_cmp))
    n_delta = WINDOW // t + 2
    d = (jnp.arange(n_delta).reshape(-1, 1, 1) * t + jnp.arange(t).reshape(1, 1, t)
         - jnp.arange(t).reshape(1, t, 1))
    tz = _bias_lanes(tbl, d, (d >= 0) & (d < WINDOW))
    body = functools.partial(_nsa_prompt_body, n_sel=min(SEL_TOPK, S // SEL_BLOCK))
    kv_spec = pl.BlockSpec((None, None, S, dh), lambda b, g, i: (b, g, 0, 0))
    cmp_spec = pl.BlockSpec((None, None, ncp, dh), lambda b, g, i: (b, g, 0, 0))
    oT = pl.pallas_call(
        body,
        grid=(B, G, n_qt),
        in_specs=[pl.BlockSpec((None, None, None, dh, R * t), lambda b, g, i: (b, g, i, 0, 0)),
                  pl.BlockSpec((None, None, None, 3, R * t), lambda b, g, i: (b, g, i, 0, 0)),
                  cmp_spec, cmp_spec, kv_spec, kv_spec, kv_spec, kv_spec,
                  pl.BlockSpec((None, None, ncp, R * t), lambda b, g, i: (g, i, 0, 0)),
                  pl.BlockSpec((None,) + tz.shape[1:], lambda b, g, i: (g, 0, 0, 0))],
        out_specs=pl.BlockSpec((None, None, None, dh, R * t), lambda b, g, i: (b, g, i, 0, 0)),
        out_shape=jax.ShapeDtypeStruct((B, G, n_qt, dh, R * t), jnp.float32),
        compiler_params=pltpu.CompilerParams(dimension_semantics=("parallel", "parallel", "arbitrary"),
                                             vmem_limit_bytes=VMEM_LIMIT_BYTES),
        name="nsa_prompt_attention",
    )(qT, gT, padc(kcmp), padc(vcmp), tr(ks), tr(vs), tr(kw), tr(vw), cb, tz)
    o = oT.reshape(B, G, n_qt, dh, R, t).transpose(0, 2, 5, 1, 4, 3)
    return o.reshape(B, S, G * R * dh)


SAMPLE_PAGES_PER_STEP = 8


def _softmax_lanes(s):
    m = jnp.max(s, axis=1, keepdims=True)
    e = jnp.where(s > MASKED_BELOW, jnp.exp(s - m), 0.0)
    return e, jnp.sum(e, axis=1, keepdims=True)


def _attend_lanes(qbd, k2, v2, bias, mask=None):
    s = jnp.dot(qbd, k2, preferred_element_type=jnp.float32) + bias
    if mask is not None:
        s = jnp.where(mask > 0.5, s, NEG_INF)
    e, l = _softmax_lanes(s)
    o = lax.dot_general(e.astype(jnp.bfloat16), v2, (((1,), (1,)), ((), ())),
                        preferred_element_type=jnp.float32)
    return o / jnp.maximum(l, TINY)


def _nsa_sample_body(pt_ref, qbd_ref, gate_ref, kc_ref, vc_ref, new_ref, wk_ref, wv_ref, cb_ref, sb_ref, wb_ref,
                     *rest, n_sel, tq, past, n_blocks):
    pps = SAMPLE_PAGES_PER_STEP
    k_pages, v_pages = rest[:pps], rest[pps:2 * pps]
    o_ref, ks_ref, vs_ref = rest[2 * pps:]
    step = pl.program_id(1)
    rows = ks_ref.shape[0]
    width = qbd_ref.shape[0]
    for i in range(pps):
        cols = pl.ds(pl.multiple_of((step * pps + i) * PAGE_SIZE, PAGE_SIZE), PAGE_SIZE)
        ks_ref[:, cols] = k_pages[i][...].reshape(rows, PAGE_SIZE).astype(jnp.bfloat16)
        vs_ref[:, cols] = v_pages[i][...].reshape(rows, PAGE_SIZE).astype(jnp.bfloat16)

    @pl.when(step == pl.num_programs(1) - 1)
    def _():
        qbd = qbd_ref[...]
        n_keys = ks_ref.shape[1]
        ks_ref[:, past:n_keys] = new_ref[0]
        vs_ref[:, past:n_keys] = new_ref[1]
        s_c = lax.dot_general(qbd, kc_ref[...], (((1,), (1,)), ((), ())),
                              preferred_element_type=jnp.float32) + cb_ref[...]
        e_c, l_c = _softmax_lanes(s_c)
        p_cmp = e_c / jnp.maximum(l_c, TINY)
        o_cmp = jnp.dot(p_cmp.astype(jnp.bfloat16), vc_ref[...], preferred_element_type=jnp.float32)
        ncp = p_cmp.shape[1]
        cs = lax.broadcasted_iota(jnp.int32, (ncp, LANES), 0) * CMP_STRIDE
        nb = lax.broadcasted_iota(jnp.int32, (ncp, LANES), 1) * SEL_BLOCK
        overlap_t = jnp.where((cs < nb + SEL_BLOCK) & (cs + CMP_LEN > nb), 1.0, 0.0).astype(jnp.bfloat16)
        ra = lax.broadcasted_iota(jnp.int32, (width, width), 0)
        rb = lax.broadcasted_iota(jnp.int32, (width, width), 1)
        per_group = NSA_GROUP * tq
        same_query = jnp.where((ra // per_group == rb // per_group) & (ra % tq == rb % tq), 1.0, 0.0)
        imp = _dot_f32_by_01(same_query.astype(jnp.bfloat16), _x_dot_01(p_cmp, overlap_t))
        blk = lax.broadcasted_iota(jnp.int32, (width, LANES), 1)
        cur = (past + lax.broadcasted_iota(jnp.int32, (width, LANES), 0) % tq) // SEL_BLOCK
        forced = (blk == 0) | (blk == cur) | (blk == cur - 1)
        score = jnp.where(blk <= cur, jnp.where(forced, FORCE_SCORE, imp), -1.0)
        score = jnp.where(blk < n_blocks, score, -2.0)
        score_t = score.T
        bt = lax.broadcasted_iota(jnp.int32, (LANES, width), 0)
        rank = jnp.zeros((LANES, width), jnp.float32)
        for mrow in range(n_blocks):
            row = score_t[mrow:mrow + 1, :]
            ahead = (row > score_t) | ((row == score_t) & (bt > mrow))
            rank = rank + jnp.where(ahead, 1.0, 0.0)
        sel = jnp.where(rank < n_sel, 1.0, 0.0).T.astype(jnp.bfloat16)
        kb = lax.broadcasted_iota(jnp.int32, (LANES, n_keys), 1) // SEL_BLOCK
        nn = lax.broadcasted_iota(jnp.int32, (LANES, n_keys), 0)
        expand = jnp.where(kb == nn, 1.0, 0.0).astype(jnp.bfloat16)
        mask = jnp.dot(sel, expand, preferred_element_type=jnp.float32)
        o_sel = _attend_lanes(qbd, ks_ref[...], vs_ref[...], sb_ref[...], mask)
        window = lambda ref, new: jnp.concatenate(
            [ref[...].reshape(rows, ref.shape[-1]).astype(jnp.bfloat16), new], axis=1)
        o_win = _attend_lanes(qbd, window(wk_ref, new_ref[2]), window(wv_ref, new_ref[3]), wb_ref[...])
        g = 1.0 / (1.0 + jnp.exp(-gate_ref[...]))
        o = g[:, 0:1] * o_cmp + g[:, 1:2] * o_sel + g[:, 2:3] * o_win
        r = lax.broadcasted_iota(jnp.int32, (width, rows), 0) // per_group
        c = lax.broadcasted_iota(jnp.int32, (width, rows), 1) // HEAD_DIM
        o = jnp.where(r == c, o, 0.0)
        fa = lax.broadcasted_iota(jnp.int32, (rows, LANES), 0) % HEAD_DIM
        fb = lax.broadcasted_iota(jnp.int32, (rows, LANES), 1)
        fold = jnp.where(fa == fb, 1.0, 0.0).astype(jnp.bfloat16)
        o_ref[...] = _x_dot_01(o, fold)[:, :HEAD_DIM]


def _rows_table(table):
    g, k, w = table.shape
    return table.transpose(0, 2, 1).reshape(g * w, k)


def nsa_sample_attention(q, gate_logits, kcmp, vcmp, new_kv, sk_pool, sv_pool, wk_buf, wv_buf, page_table, rel_bias):
    DB, T, G, R, dh = q.shape
    bf = jnp.bfloat16
    width = G * R * T
    rows = G * dh
    n_pages = page_table.shape[1]
    past = n_pages * PAGE_SIZE
    n_keys = past + PAGE_SIZE
    n_blocks = _round_up(past + T, SEL_BLOCK) // SEL_BLOCK
    tbl = rel_bias.reshape(N_BUCKETS, G, R).astype(jnp.float32)
    q_pos = past + jnp.arange(T)
    qg = (q * (HEAD_DIM ** -0.5)).transpose(0, 2, 3, 1, 4)
    same_group = jnp.eye(G, dtype=qg.dtype).reshape(1, G, 1, 1, G, 1)
    qbd = (qg[:, :, :, :, None, :] * same_group).astype(bf).reshape(DB, width, rows)
    gates = jnp.pad(gate_logits.transpose(0, 2, 3, 1, 4).reshape(DB, width, 3), ((0, 0), (0, 0), (0, 5)))
    n_cmp = n_blocks * SEL_BLOCK // CMP_STRIDE - CMP_LEN // CMP_STRIDE + 1
    ncp = kcmp.shape[1]
    c_idx = jnp.arange(ncp).reshape(ncp, 1)
    dist = q_pos.reshape(1, T) - (c_idx * CMP_STRIDE + CMP_LEN - 1)
    cb = _rows_table(_bias_lanes(tbl, dist, (dist >= 0) & (c_idx < n_cmp)))
    k_idx = jnp.arange(n_keys).reshape(-1, 1)
    dist = q_pos.reshape(1, T) - k_idx
    sb = _rows_table(_bias_lanes(tbl, dist, dist >= 0))
    wbuf = wk_buf.shape[1]
    n_win = wbuf + PAGE_SIZE
    w_idx = jnp.arange(n_win).reshape(-1, 1)
    w_pos = past - wbuf + w_idx
    dist = q_pos.reshape(1, T) - w_pos
    wb = _rows_table(_bias_lanes(tbl, dist, (dist >= 0) & (dist < WINDOW) & (w_pos >= 0) & (w_idx < wbuf + T)))
    new_t = lambda x: jnp.pad(x.astype(bf).transpose(0, 2, 3, 1),
                              ((0, 0), (0, 0), (0, 0), (0, PAGE_SIZE - T))).reshape(DB, rows, PAGE_SIZE)
    new = jnp.stack([new_t(x) for x in new_kv], axis=1)
    native = lambda x: x.transpose(0, 2, 3, 1)
    pps = SAMPLE_PAGES_PER_STEP
    body = functools.partial(_nsa_sample_body, n_sel=min(SEL_TOPK, n_blocks), tq=T, past=past, n_blocks=n_blocks)
    per_b = lambda shape: pl.BlockSpec((None,) + shape, lambda b, s, pt: (b,) + (0,) * len(shape))
    const = lambda x: pl.BlockSpec(x.shape, lambda b, s, pt: (0,) * x.ndim)
    page = lambda i: pl.BlockSpec((None, G, dh, PAGE_SIZE), lambda b, s, pt: (pt[b, s * pps + i], 0, 0, 0))
    sk_t, sv_t = native(sk_pool), native(sv_pool)
    o = pl.pallas_call(
        body,
        grid_spec=pltpu.PrefetchScalarGridSpec(
            num_scalar_prefetch=1,
            grid=(DB, n_pages // pps),
            in_specs=[per_b((width, rows)), per_b((width, 8)), per_b((ncp, rows)), per_b((ncp, rows)),
                      per_b((4, rows, PAGE_SIZE)), per_b((G, dh, wbuf)), per_b((G, dh, wbuf)),
                      const(cb), const(sb), const(wb)]
                     + [page(i) for i in range(pps)] * 2,
            out_specs=per_b((width, dh)),
            scratch_shapes=[pltpu.VMEM((rows, n_keys), bf), pltpu.VMEM((rows, n_keys), bf)]),
        out_shape=jax.ShapeDtypeStruct((DB, width, dh), jnp.float32),
        compiler_params=pltpu.CompilerParams(dimension_semantics=("parallel", "arbitrary"),
                                             vmem_limit_bytes=VMEM_LIMIT_BYTES),
        name="nsa_sample_attention",
    )(page_table, qbd, gates, kcmp, vcmp, new, native(wk_buf), native(wv_buf), cb, sb, wb,
      *([sk_t] * pps), *([sv_t] * pps))
    return o.reshape(DB, G, R, T, dh).transpose(0, 3, 1, 2, 4).reshape(DB, T, G * R * dh)


FOX_Q_TILE = 512
FOX_K_TILE = 256


def _softmax_step(s, v, carry, masked):
    m, l, acc = carry
    m_new = jnp.maximum(m, jnp.max(s, axis=0, keepdims=True))
    alpha = jnp.exp(m - m_new)
    e = jnp.exp(s - m_new)
    if masked:
        e = jnp.where(s > MASKED_BELOW, e, 0.0)
    l = alpha * l + jnp.sum(e, axis=0, keepdims=True)
    pv = lax.dot_general(v, e.astype(jnp.bfloat16), (((0,), (0,)), ((), ())),
                         preferred_element_type=jnp.float32)
    return m_new, l, alpha * acc + pv


def _fox_prompt_body(qT_ref, qd_ref, k_ref, kd_ref, v_ref, o_ref):
    i = pl.program_id(2)
    qT = qT_ref[...]
    q_decay = qd_ref[...]
    width = qT.shape[1]

    def tile(j):
        rows = pl.ds(pl.multiple_of(j * FOX_K_TILE, FOX_K_TILE), FOX_K_TILE)
        s = jnp.dot(k_ref[rows, :], qT, preferred_element_type=jnp.float32)
        s = s + jnp.dot(kd_ref[rows, :], q_decay, preferred_element_type=jnp.float32)
        return s, v_ref[rows, :]

    def full_step(j, carry):
        s, v = tile(j)
        return _softmax_step(s, v, carry, masked=False)

    carry = (jnp.full((1, width), NEG_INF, jnp.float32), jnp.zeros((1, width), jnp.float32),
             jnp.zeros((HEAD_DIM, width), jnp.float32))
    ratio = FOX_Q_TILE // FOX_K_TILE
    carry = lax.fori_loop(0, i * ratio, full_step, carry)
    q_pos = i * FOX_Q_TILE + lax.broadcasted_iota(jnp.int32, (FOX_K_TILE, width), 1)
    for dj in range(ratio):
        j = i * ratio + dj
        s, v = tile(j)
        k_pos = j * FOX_K_TILE + lax.broadcasted_iota(jnp.int32, (FOX_K_TILE, width), 0)
        s = jnp.where(k_pos <= q_pos, s, NEG_INF)
        carry = _softmax_step(s, v, carry, masked=True)
    m, l, acc = carry
    o_ref[...] = acc / jnp.maximum(l, TINY)


DECAY_COLS = 8


def fox_prompt_attention(q, k, v, logf):
    B, S, H, dh = q.shape
    bf = jnp.bfloat16
    c = jnp.cumsum(logf, axis=1).transpose(0, 2, 1)
    hi, mid, lo = _split3_outside_kernel(c)
    one, zero = jnp.ones_like(hi), jnp.zeros_like(hi)
    k_decay = jnp.stack([-hi, -mid, -lo, one, one, one, zero, zero], axis=-1)
    q_decay = jnp.stack([one, one, one, hi, mid, lo, zero, zero], axis=2)
    kb = k.astype(bf).transpose(0, 2, 1, 3)
    qT = (q * (dh ** -0.5)).astype(bf).transpose(0, 2, 3, 1)
    vb = v.astype(bf).transpose(0, 2, 1, 3)
    per_head = lambda rows: pl.BlockSpec((None, None, S, rows), lambda b, h, i: (b, h, 0, 0))
    per_tile = lambda rows: pl.BlockSpec((None, None, rows, FOX_Q_TILE), lambda b, h, i: (b, h, 0, i))
    oT = pl.pallas_call(
        _fox_prompt_body,
        grid=(B, H, S // FOX_Q_TILE),
        in_specs=[per_tile(dh), per_tile(DECAY_COLS), per_head(dh), per_head(DECAY_COLS), per_head(dh)],
        out_specs=per_tile(dh),
        out_shape=jax.ShapeDtypeStruct((B, H, dh, S), jnp.float32),
        compiler_params=pltpu.CompilerParams(dimension_semantics=("parallel", "parallel", "arbitrary"),
                                             vmem_limit_bytes=VMEM_LIMIT_BYTES),
        name="fox_prompt_attention",
    )(qT, q_decay, kb, k_decay, vb)
    return oT.transpose(0, 3, 1, 2).reshape(B, S, H * dh)


FOX_PAGES_PER_STEP = 4


def _fox_sample_body(pt_ref, qbd_ref, bp_ref, kn_ref, vn_ref, bn_ref, *rest, n_q):
    pps = FOX_PAGES_PER_STEP
    k_pages, v_pages = rest[:pps], rest[pps:2 * pps]
    o_ref, m_ref, l_ref, acc_ref = rest[2 * pps:]
    p = pl.program_id(1)
    width = qbd_ref.shape[0]
    rows = kn_ref.shape[0]

    @pl.when(p == 0)
    def _():
        m_ref[...] = jnp.full(m_ref.shape, NEG_INF, jnp.float32)
        l_ref[...] = jnp.zeros(l_ref.shape, jnp.float32)
        acc_ref[...] = jnp.zeros(acc_ref.shape, jnp.float32)

    def attend(k2, v2, bias_t):
        s_t = jnp.dot(qbd_ref[...], k2, preferred_element_type=jnp.float32).T + bias_t
        m = m_ref[...]
        m_new = jnp.maximum(m, jnp.max(s_t, axis=0, keepdims=True))
        alpha = jnp.exp(m - m_new)
        e = jnp.where(s_t > MASKED_BELOW, jnp.exp(s_t - m_new), 0.0)
        m_ref[...] = m_new
        l_ref[...] = alpha * l_ref[...] + jnp.sum(e, axis=0, keepdims=True)
        acc_ref[...] = alpha * acc_ref[...] + jnp.dot(v2, e.astype(jnp.bfloat16),
                                                      preferred_element_type=jnp.float32)

    pages = lambda refs: jnp.concatenate(
        [ref[...].reshape(rows, PAGE_SIZE).astype(jnp.bfloat16) for ref in refs], axis=1)
    attend(pages(k_pages), pages(v_pages), bp_ref[...].reshape(pps * PAGE_SIZE, width))

    @pl.when(p == pl.num_programs(1) - 1)
    def _():
        attend(kn_ref[...], vn_ref[...], bn_ref[...])
        out = acc_ref[...] / jnp.maximum(l_ref[...], TINY)
        r = lax.broadcasted_iota(jnp.int32, (rows, width), 0) // HEAD_DIM
        c = lax.broadcasted_iota(jnp.int32, (rows, width), 1) // n_q
        out = jnp.where(r == c, out, 0.0)
        la = lax.broadcasted_iota(jnp.int32, (width, width), 0) % n_q
        lb = lax.broadcasted_iota(jnp.int32, (width, width), 1)
        gather_q = jnp.where(la == lb, 1.0, 0.0).astype(jnp.bfloat16)
        o_ref[...] = _x_dot_01(out, gather_q)[:, :n_q]


def fox_sample_attention(q, k, v, logf, k_pool, v_pool, f_pool, page_table):
    DB, T, H, dh = q.shape
    n_pages = page_table.shape[1]
    past = n_pages * PAGE_SIZE
    bf = jnp.bfloat16
    width = H * T
    rows = H * dh
    f_all = jnp.concatenate([f_pool[page_table].reshape(DB, past, H), logf], axis=1)
    c = jnp.cumsum(f_all, axis=1)
    c_q = c[:, past:].transpose(0, 2, 1).reshape(DB, 1, 1, width)
    c_k = jnp.repeat(c[:, :past].reshape(DB, n_pages, PAGE_SIZE, H), T, axis=-1)
    bias_past = c_q - c_k
    tok = jnp.arange(PAGE_SIZE).reshape(1, PAGE_SIZE, 1)
    qi = (jnp.arange(width) % T).reshape(1, 1, width)
    c_new = jnp.pad(jnp.repeat(c[:, past:], T, axis=-1), ((0, 0), (0, PAGE_SIZE - T), (0, 0)))
    bias_new = jnp.where((tok <= qi) & (tok < T), c_q[:, 0] - c_new, NEG_INF)
    qh = (q * (dh ** -0.5)).transpose(0, 2, 1, 3)
    same_head = jnp.eye(H, dtype=qh.dtype).reshape(1, H, 1, H, 1)
    qbd = (qh[:, :, :, None, :] * same_head).astype(bf).reshape(DB, width, rows)
    new_t = lambda x: jnp.pad(x.astype(bf).transpose(0, 2, 3, 1),
                              ((0, 0), (0, 0), (0, 0), (0, PAGE_SIZE - T))).reshape(DB, rows, PAGE_SIZE)
    pool_t = lambda x: x.transpose(0, 2, 3, 1)
    body = functools.partial(_fox_sample_body, n_q=T)
    pps = FOX_PAGES_PER_STEP
    page = lambda i: pl.BlockSpec((None, H, dh, PAGE_SIZE), lambda b, p, pt: (pt[b, p * pps + i], 0, 0, 0))
    per_b = lambda shape: pl.BlockSpec((None,) + shape, lambda b, p, pt: (b,) + (0,) * len(shape))
    kt, vt = pool_t(k_pool), pool_t(v_pool)
    o = pl.pallas_call(
        body,
        grid_spec=pltpu.PrefetchScalarGridSpec(
            num_scalar_prefetch=1,
            grid=(DB, n_pages // pps),
            in_specs=[per_b((width, rows)),
                      pl.BlockSpec((None, pps, PAGE_SIZE, width), lambda b, p, pt: (b, p, 0, 0)),
                      per_b((rows, PAGE_SIZE)), per_b((rows, PAGE_SIZE)), per_b((PAGE_SIZE, width))]
                     + [page(i) for i in range(pps)] * 2,
            out_specs=per_b((rows, T)),
            scratch_shapes=[pltpu.VMEM((1, width), jnp.float32), pltpu.VMEM((1, width), jnp.float32),
                            pltpu.VMEM((rows, width), jnp.float32)]),
        out_shape=jax.ShapeDtypeStruct((DB, rows, T), jnp.float32),
        compiler_params=pltpu.CompilerParams(dimension_semantics=("parallel", "arbitrary"),
                                             vmem_limit_bytes=VMEM_LIMIT_BYTES),
        name="fox_sample_attention",
    )(page_table, qbd, bias_past, new_t(k), new_t(v), bias_new, *([kt] * pps), *([vt] * pps))
    return o.reshape(DB, H, dh, T).transpose(0, 3, 1, 2).reshape(DB, T, H * dh)


def _compress_body(pt_ref, newk_ref, newv_ref, pe_ref, w1_ref, w2_ref, *rest, n_tok, n_out):
    pps = SAMPLE_PAGES_PER_STEP
    k_pages, v_pages = rest[:pps], rest[pps:2 * pps]
    ko_ref, vo_ref, xk_ref, xv_ref, flat_ref = rest[2 * pps:]
    step = pl.program_id(1)
    rows = newk_ref.shape[1]
    pairs = rows // LANES
    for i in range(pps):
        dst = pl.ds(pl.multiple_of((step * pps + i) * PAGE_SIZE, PAGE_SIZE), PAGE_SIZE)
        for x_ref, pages in ((xk_ref, k_pages), (xv_ref, v_pages)):
            page_t = pages[i][...].reshape(rows, PAGE_SIZE).T
            for gp in range(pairs):
                x_ref[gp, dst, :] = page_t[:, gp * LANES:(gp + 1) * LANES]

    @pl.when(step == pl.num_programs(1) - 1)
    def _():
        n_new = newk_ref.shape[0]
        tail = xk_ref.shape[1] - n_tok - n_new
        for which, (x_ref, new_ref, o_ref) in enumerate(((xk_ref, newk_ref, ko_ref), (xv_ref, newv_ref, vo_ref))):
            for gp in range(pairs):
                x_ref[gp, n_tok:n_tok + n_new, :] = new_ref[:, gp * LANES:(gp + 1) * LANES]
                x_ref[gp, n_tok + n_new:, :] = jnp.zeros((tail, LANES), jnp.float32)
            for l in range(CMP_LEN):
                for gp in range(pairs):
                    blk = x_ref[gp, pl.ds(l, n_out, stride=CMP_STRIDE), :]
                    blk = blk + pe_ref[which, l:l + 1, :]
                    flat_ref[gp * n_out:(gp + 1) * n_out, l * LANES:(l + 1) * LANES] = blk.astype(jnp.bfloat16)
            hidden = _gelu_exact(jnp.dot(flat_ref[...], w1_ref[which], preferred_element_type=jnp.float32))
            out = jnp.dot(hidden.astype(jnp.bfloat16), w2_ref[which], preferred_element_type=jnp.float32)
            o_ref[...] = jnp.concatenate([out[gp * n_out:(gp + 1) * n_out] for gp in range(pairs)],
                                         axis=1).astype(o_ref.dtype)


def _pair_weights(w1, w2, pe):
    dh = w2.shape[0]
    eye2 = jnp.eye(2, dtype=w1.dtype)
    w1p = jnp.einsum('lde,pq->lpdqe', w1.reshape(CMP_LEN, dh, dh), eye2).reshape(CMP_LEN * 2 * dh, 2 * dh)
    w2p = jnp.einsum('de,pq->pdqe', w2, eye2).reshape(2 * dh, 2 * dh)
    pep = jnp.concatenate([pe, pe], axis=1)
    return w1p, w2p, pep


def compress_paged(k_pool, v_pool, new_k, new_v, page_table, weights_k, weights_v):
    DB, T, G, dh = new_k.shape
    bf = jnp.bfloat16
    rows = G * dh
    n_pages = page_table.shape[1]
    past = n_pages * PAGE_SIZE
    lp = _round_up(past + T, SEL_BLOCK)
    n_cmp = lp // CMP_STRIDE - CMP_LEN // CMP_STRIDE + 1
    n_out = _round_up(n_cmp, 8)
    x_rows = _round_up(CMP_LEN + CMP_STRIDE * (n_out - 1), PAGE_SIZE)
    wk = _pair_weights(weights_k[1], weights_k[2], weights_k[0])
    wv = _pair_weights(weights_v[1], weights_v[2], weights_v[0])
    w1 = jnp.stack([wk[0], wv[0]]).astype(bf)
    w2 = jnp.stack([wk[1], wv[1]]).astype(bf)
    pe = jnp.stack([wk[2], wv[2]])
    native = lambda x: x.transpose(0, 2, 3, 1)
    pps = SAMPLE_PAGES_PER_STEP
    body = functools.partial(_compress_body, n_tok=past, n_out=n_out)
    per_b = lambda shape: pl.BlockSpec((None,) + shape, lambda b, s, pt: (b,) + (0,) * len(shape))
    const = lambda x: pl.BlockSpec(x.shape, lambda b, s, pt: (0,) * x.ndim)
    page = lambda i: pl.BlockSpec((None, G, dh, PAGE_SIZE), lambda b, s, pt: (pt[b, s * pps + i], 0, 0, 0))
    kt, vt = native(k_pool), native(v_pool)
    out_shape = jax.ShapeDtypeStruct((DB, n_out, rows), bf)
    return pl.pallas_call(
        body,
        grid_spec=pltpu.PrefetchScalarGridSpec(
            num_scalar_prefetch=1,
            grid=(DB, n_pages // pps),
            in_specs=[per_b((T, rows)), per_b((T, rows)), const(pe), const(w1), const(w2)]
                     + [page(i) for i in range(pps)] * 2,
            out_specs=[per_b((n_out, rows)), per_b((n_out, rows))],
            scratch_shapes=[pltpu.VMEM((rows // LANES, x_rows, LANES), jnp.float32),
                            pltpu.VMEM((rows // LANES, x_rows, LANES), jnp.float32),
                            pltpu.VMEM((rows // LANES * n_out, CMP_LEN * LANES), bf)]),
        out_shape=[out_shape, out_shape],
        compiler_params=pltpu.CompilerParams(dimension_semantics=("parallel", "arbitrary"),
                                             vmem_limit_bytes=VMEM_LIMIT_BYTES),
        name="nsa_compress_paged",
    )(page_table, new_k.reshape(DB, T, rows), new_v.reshape(DB, T, rows), pe, w1, w2, *([kt] * pps), *([vt] * pps))


def compress(x, pe, w1, w2):
    B, T, G, dh = x.shape
    r = CMP_LEN // CMP_STRIDE
    n_chunks = T // CMP_STRIDE
    nc = n_chunks - r + 1
    ch = x.reshape(B, n_chunks, CMP_STRIDE, G, dh)
    blk = jnp.concatenate([ch[:, m:m + nc] for m in range(r)], axis=2)
    blk = blk + pe[:, None, :]
    flat = blk.transpose(0, 1, 3, 2, 4).reshape(B, nc, G, CMP_LEN * dh)
    return jax.nn.gelu(flat @ w1, approximate=False) @ w2


def nsa_split(proj):
    B, T, _ = proj.shape
    nq = NSA_HEADS * HEAD_DIM
    nkv = NSA_KV_HEADS * HEAD_DIM
    q = proj[..., :nq].reshape(B, T, NSA_KV_HEADS, NSA_GROUP, HEAD_DIM)
    kv = proj[..., nq:nq + 6 * nkv].reshape(B, T, 6, NSA_KV_HEADS, HEAD_DIM)
    gl = proj[..., nq + 6 * nkv:nq + 6 * nkv + 3 * NSA_HEADS].reshape(B, T, NSA_KV_HEADS, NSA_GROUP, 3)
    return q, gl, [kv[:, :, s] for s in range(6)]


def nsa_prompt(proj, cw, rel_bias):
    S = proj.shape[1]
    pe_k, w1_k, w2_k, pe_v, w1_v, w2_v = cw
    q, gl, (kc, vc, ks, vs, kw, vw) = nsa_split(proj)
    kcmp = compress(kc, pe_k, w1_k, w2_k)
    vcmp = compress(vc, pe_v, w1_v, w2_v)
    o = nsa_prompt_attention(q, gl, kcmp, vcmp, ks, vs, kw, vw, rel_bias)
    wb = min(WINDOW, S)
    return o, [kc, vc, ks, vs, kw[:, S - wb:], vw[:, S - wb:]]


def nsa_sample(proj, cw, rel_bias, ck_pool, cv_pool, sk_pool, sv_pool, wk_buf, wv_buf, page_table):
    DB, T, _ = proj.shape
    pe_k, w1_k, w2_k, pe_v, w1_v, w2_v = cw
    q, gl, (kc, vc, ks, vs, kw, vw) = nsa_split(proj)
    kcmp, vcmp = compress_paged(ck_pool, cv_pool, kc, vc, page_table, (pe_k, w1_k, w2_k), (pe_v, w1_v, w2_v))
    kw_all = jnp.concatenate([wk_buf, kw], axis=1)
    vw_all = jnp.concatenate([wv_buf, vw], axis=1)
    o = nsa_sample_attention(q, gl, kcmp, vcmp, (ks, vs, kw, vw), sk_pool, sv_pool, wk_buf, wv_buf,
                             page_table, rel_bias)
    return o, [kc, vc, ks, vs, kw_all[:, T:], vw_all[:, T:]]


def kernel(x_prompt, x_sample, p_prompt, p_sample, cache_fox_k, cache_fox_v, cache_fox_logf,
           cache_nsa_cmp_k, cache_nsa_cmp_v, cache_nsa_sel_k, cache_nsa_sel_v,
           cache_nsa_win_k, cache_nsa_win_v, page_table,
           norm_mix, norm_ffn, norm_ple, norm_final,
           fox_w_in, fox_b_f, fox_w_out, nsa_w_in, nsa_w_out,
           cmp_pe_k, cmp_w1_k, cmp_w2_k, cmp_pe_v, cmp_w1_v, cmp_w2_v, rel_bias,
           peer_wq, peer_sub_k1, peer_sub_k2, peer_u, peer_v, ple_w_proj, ple_w_gate):
    B, S, d = x_prompt.shape
    DB, T, _ = x_sample.shape
    n_p, n_s = B * S, DB * T
    bf = jnp.bfloat16
    rows = lambda a, b: jnp.concatenate([a.reshape(n_p, -1), b.reshape(n_s, -1)], axis=0)
    x = rows(x_prompt, x_sample)
    fox_p, fox_s, nsa_p, nsa_s = [], [], [], []
    for i in range(DEPTH):
        j = i // 2
        if i % 2 == 0:
            h = rmsnorm_pallas(x, norm_mix[i], bf)
            proj = linear_pallas(h, fox_w_in[j])
            nh = FOX_HEADS * HEAD_DIM
            logf = jax.nn.log_sigmoid(proj[:, 3 * nh:3 * nh + FOX_HEADS] + fox_b_f[j])
            heads = lambda a, lead: a.reshape(lead + (FOX_HEADS, -1))
            qp, kp, vp = (heads(proj[:n_p, s * nh:(s + 1) * nh], (B, S)) for s in range(3))
            qs, ks_, vs_ = (heads(proj[n_p:, s * nh:(s + 1) * nh], (DB, T)) for s in range(3))
            fp, fs = logf[:n_p].reshape(B, S, FOX_HEADS), logf[n_p:].reshape(DB, T, FOX_HEADS)
            op = fox_prompt_attention(qp, kp, vp, fp)
            os_ = fox_sample_attention(qs, ks_, vs_, fs, cache_fox_k[j], cache_fox_v[j], cache_fox_logf[j],
                                       page_table)
            fox_p.append([kp, vp, fp])
            fox_s.append([ks_, vs_, fs])
            y = linear_pallas(rows(op, os_), fox_w_out[j])
        else:
            cw = (cmp_pe_k[j], cmp_w1_k[j], cmp_w2_k[j], cmp_pe_v[j], cmp_w1_v[j], cmp_w2_v[j])
            h = rmsnorm_pallas(x, norm_mix[i], bf)
            proj = linear_pallas(h, nsa_w_in[j])
            op, stp = nsa_prompt(proj[:n_p].reshape(B, S, -1), cw, rel_bias)
            os_, sts = nsa_sample(proj[n_p:].reshape(DB, T, -1), cw, rel_bias,
                                  cache_nsa_cmp_k[j], cache_nsa_cmp_v[j], cache_nsa_sel_k[j],
                                  cache_nsa_sel_v[j], cache_nsa_win_k[j], cache_nsa_win_v[j], page_table)
            nsa_p.append(stp)
            nsa_s.append(sts)
            y = linear_pallas(rows(op, os_), nsa_w_out[j])
        x = x + y
        h = rmsnorm_pallas(x, norm_ffn[i], bf)
        x = x + peer_pallas(h, peer_wq[i].T.astype(bf), peer_sub_k1[i].astype(bf), peer_sub_k2[i].astype(bf),
                            peer_u[i].astype(bf), peer_v[i].astype(bf))
        x = ple_pallas(x, rows(p_prompt[i], p_sample[i]), norm_ple[i], ple_w_proj[i], ple_w_gate[i])
    y = rmsnorm_pallas(x, norm_final, jnp.float32)
    st = lambda lst, k: jnp.stack([s[k] for s in lst])
    return (y[:n_p].reshape(B, S, d), y[n_p:].reshape(DB, T, d),
            st(fox_p, 0), st(fox_p, 1), st(fox_p, 2),
            st(fox_s, 0), st(fox_s, 1), st(fox_s, 2),
            st(nsa_p, 0), st(nsa_p, 1), st(nsa_p, 2), st(nsa_p, 3), st(nsa_p, 4), st(nsa_p, 5),
            st(nsa_s, 0), st(nsa_s, 1), st(nsa_s, 2), st(nsa_s, 3), st(nsa_s, 4), st(nsa_s, 5))
```

```python
import functools
import math

import jax
import jax.numpy as jnp
from jax import lax
from jax.experimental import pallas as pl
from jax.experimental.pallas import tpu as pltpu

D_MODEL = 1024
DEPTH = 2
PAGE_SIZE = 128
HEAD_DIM = 64
FOX_HEADS = D_MODEL // HEAD_DIM
NSA_HEADS = D_MODEL // HEAD_DIM
NSA_KV_HEADS = 4
NSA_GROUP = NSA_HEADS // NSA_KV_HEADS
CMP_LEN = 32
CMP_STRIDE = 16
SEL_BLOCK = 64
SEL_TOPK = 16
WINDOW = 512
N_BUCKETS = 32
MAX_DISTANCE = 128
PEER_HEADS = 8
PEER_TOPK = 16
N_KEYS = 128
N_EXPERTS = N_KEYS * N_KEYS
PEER_DKEY = 256
FORCE_SCORE = 1e4
RMS_EPS = 1e-6
NEG_INF = -1e30
TINY = 1e-30

VMEM_LIMIT_BYTES = 56 * 1024 * 1024
LANES = 128
ROW_TILE = 512
MAX_COL_TILE = 1024


def _round_up(x, m):
    return -(-x // m) * m


def _split3(x):
    hi = x.astype(jnp.bfloat16)
    r1 = x - hi.astype(jnp.float32)
    mid = r1.astype(jnp.bfloat16)
    lo = (r1 - mid.astype(jnp.float32)).astype(jnp.bfloat16)
    return hi, mid, lo


def _split3_outside_kernel(x):
    to_bf16 = lambda a: lax.reduce_precision(a, exponent_bits=8, mantissa_bits=7)
    hi = to_bf16(x)
    r1 = x - hi
    mid = to_bf16(r1)
    lo = to_bf16(r1 - mid)
    return hi.astype(jnp.bfloat16), mid.astype(jnp.bfloat16), lo.astype(jnp.bfloat16)


def _rms(x, g):
    return x * lax.rsqrt(jnp.mean(x * x, axis=-1, keepdims=True) + RMS_EPS) * g


def _rmsnorm_body(x_ref, g_ref, o_ref):
    o_ref[...] = _rms(x_ref[...], g_ref[...]).astype(o_ref.dtype)


def rmsnorm_pallas(x, g, out_dtype):
    n, d = x.shape
    return pl.pallas_call(
        _rmsnorm_body,
        grid=(n // ROW_TILE,),
        in_specs=[pl.BlockSpec((ROW_TILE, d), lambda i: (i, 0)),
                  pl.BlockSpec((1, d), lambda i: (0, 0))],
        out_specs=pl.BlockSpec((ROW_TILE, d), lambda i: (i, 0)),
        out_shape=jax.ShapeDtypeStruct((n, d), out_dtype),
        compiler_params=pltpu.CompilerParams(dimension_semantics=("parallel",)),
        name="rmsnorm",
    )(x, g.reshape(1, d))


def _linear_body(x_ref, w_ref, o_ref):
    o_ref[...] = jnp.dot(x_ref[...].astype(jnp.bfloat16), w_ref[...], preferred_element_type=jnp.float32)


def linear_pallas(x, w):
    n, k = x.shape
    m = w.shape[1]
    mp = _round_up(m, LANES)
    tn = max(t for t in range(LANES, MAX_COL_TILE + 1, LANES) if mp % t == 0)
    wb = jnp.pad(w.astype(jnp.bfloat16), ((0, 0), (0, mp - m)))
    return pl.pallas_call(
        _linear_body,
        grid=(n // ROW_TILE, mp // tn),
        in_specs=[pl.BlockSpec((ROW_TILE, k), lambda i, j: (i, 0)),
                  pl.BlockSpec((k, tn), lambda i, j: (0, j))],
        out_specs=pl.BlockSpec((ROW_TILE, tn), lambda i, j: (i, j)),
        out_shape=jax.ShapeDtypeStruct((n, mp), jnp.float32),
        compiler_params=pltpu.CompilerParams(dimension_semantics=("parallel", "parallel"),
                                             vmem_limit_bytes=VMEM_LIMIT_BYTES),
        name="linear",
    )(x, wb)


def _ple_body(x_ref, p_ref, g_ref, wg_ref, wp_ref, o_ref):
    x = x_ref[...]
    h = _rms(x, g_ref[...]).astype(jnp.bfloat16)
    gate = 1.0 / (1.0 + jnp.exp(-jnp.dot(h, wg_ref[...], preferred_element_type=jnp.float32)))
    proj = jnp.dot(p_ref[...].astype(jnp.bfloat16), wp_ref[...], preferred_element_type=jnp.float32)
    o_ref[...] = x + gate * proj


def ple_pallas(x, p, g, w_proj, w_gate):
    n, d = x.shape
    dp = p.shape[1]
    return pl.pallas_call(
        _ple_body,
        grid=(n // ROW_TILE,),
        in_specs=[pl.BlockSpec((ROW_TILE, d), lambda i: (i, 0)),
                  pl.BlockSpec((ROW_TILE, dp), lambda i: (i, 0)),
                  pl.BlockSpec((1, d), lambda i: (0, 0)),
                  pl.BlockSpec((d, d), lambda i: (0, 0)),
                  pl.BlockSpec((dp, d), lambda i: (0, 0))],
        out_specs=pl.BlockSpec((ROW_TILE, d), lambda i: (i, 0)),
        out_shape=jax.ShapeDtypeStruct((n, d), jnp.float32),
        compiler_params=pltpu.CompilerParams(dimension_semantics=("parallel",),
                                             vmem_limit_bytes=VMEM_LIMIT_BYTES),
        name="ple",
    )(x, p, g.reshape(1, d), w_gate.astype(jnp.bfloat16), w_proj.astype(jnp.bfloat16))


PEER_TOKEN_TILE = 512
PEER_EXPERT_TILE = 1024
SQRT_HALF = 0.7071067811865476


def _gelu_exact(x):
    return 0.5 * x * (1.0 + lax.erf(x * SQRT_HALF))


def _top_rows(x, k):
    vals = []
    for _ in range(k):
        m = jnp.max(x, axis=0, keepdims=True)
        vals.append(m)
        x = jnp.where(x == m, NEG_INF, x)
    return vals


def _peer_route_body(h_ref, wqT_ref, k1_ref, k2_ref, n1_ref, c1_ref, r2_ref, e2_ref):
    h = h_ref[...]
    half = PEER_DKEY // 2
    nt = (((1,), (1,)), ((), ()))
    for hd in range(PEER_HEADS):
        qv = lax.dot_general(wqT_ref[hd * PEER_DKEY:(hd + 1) * PEER_DKEY, :], h, nt,
                             preferred_element_type=jnp.float32)
        s1 = jnp.dot(k1_ref[hd], qv[:half].astype(jnp.bfloat16), preferred_element_type=jnp.float32)
        s2 = jnp.dot(k2_ref[hd], qv[half:].astype(jnp.bfloat16), preferred_element_type=jnp.float32)
        v1 = _top_rows(s1, PEER_TOPK)
        v2 = _top_rows(s2, PEER_TOPK)
        v2_stack = jnp.concatenate(v2, axis=0)
        blocks = []
        for p in range(PEER_TOPK):
            n_p = PEER_TOPK // (p + 1)
            rows = -(-n_p // 8) * 8
            blk = v1[p] + v2_stack[:rows]
            if n_p < rows:
                r = lax.broadcasted_iota(jnp.int32, blk.shape, 0)
                blk = jnp.where(r < n_p, blk, NEG_INF)
            blocks.append(blk)
        c = _top_rows(jnp.concatenate(blocks, axis=0), PEER_TOPK)
        z = jnp.ones_like(c[0])
        for kk in range(1, PEER_TOPK):
            z = z + jnp.exp(c[kk] - c[0])
        tau = c[PEER_TOPK - 1]
        n1 = jnp.zeros_like(s1)
        r2 = jnp.zeros_like(s2)
        for q in range(PEER_TOPK):
            n1 = n1 + jnp.where(s1 + v2[q] >= tau, 1.0, 0.0)
            r2 = r2 + jnp.where(v2[q] > s2, 1.0, 0.0)
        n1_ref[hd] = n1
        c1_ref[hd] = jnp.exp(s1 - v1[0]) / z
        r2_ref[hd] = r2.astype(jnp.bfloat16)
        e2_ref[hd] = jnp.exp(s2 - v2[0]).astype(jnp.bfloat16)


def _peer_dense_body(h_ref, u_ref, v_ref, n1_ref, c1_ref, r2_ref, e2_ref, y_ref):
    j = pl.program_id(1)

    @pl.when(j == 0)
    def _():
        y_ref[...] = jnp.zeros_like(y_ref)

    h = h_ref[...]
    act = lax.dot_general(u_ref[...], h, (((1,), (1,)), ((), ())),
                          preferred_element_type=jnp.float32)
    groups = PEER_EXPERT_TILE // N_KEYS
    parts = []
    for aa in range(groups):
        w = None
        for hd in range(PEER_HEADS):
            picked = r2_ref[hd] < n1_ref[hd, aa:aa + 1, :].astype(jnp.bfloat16)
            term = jnp.where(picked, e2_ref[hd], 0.0) * c1_ref[hd, aa:aa + 1, :].astype(jnp.bfloat16)
            w = term if w is None else w + term
        g = _gelu_exact(act[aa * N_KEYS:(aa + 1) * N_KEYS]).astype(jnp.bfloat16)
        parts.append(w * g)
    p = jnp.concatenate(parts, axis=0)
    y_ref[...] += lax.dot_general(p, v_ref[...], (((0,), (0,)), ((), ())),
                                  preferred_element_type=jnp.float32)


def peer_pallas(h, wqT, k1, k2, u, v):
    n, d = h.shape
    t = PEER_TOKEN_TILE
    nt = n // t
    hk = (PEER_HEADS, N_KEYS)
    route = lambda dt: jax.ShapeDtypeStruct(hk + (n,), dt)
    route_spec = pl.BlockSpec(hk + (t,), lambda i: (0, 0, i))
    n1, c1, r2, e2 = pl.pallas_call(
        _peer_route_body,
        grid=(nt,),
        in_specs=[pl.BlockSpec((t, d), lambda i: (i, 0)),
                  pl.BlockSpec(wqT.shape, lambda i: (0, 0)),
                  pl.BlockSpec(k1.shape, lambda i: (0, 0, 0)),
                  pl.BlockSpec(k2.shape, lambda i: (0, 0, 0))],
        out_specs=[route_spec] * 4,
        out_shape=[route(jnp.float32), route(jnp.float32), route(jnp.bfloat16), route(jnp.bfloat16)],
        compiler_params=pltpu.CompilerParams(dimension_semantics=("parallel",),
                                             vmem_limit_bytes=VMEM_LIMIT_BYTES),
        name="peer_route",
    )(h, wqT, k1, k2)

    e = PEER_EXPERT_TILE
    groups = e // N_KEYS
    row_spec = pl.BlockSpec((PEER_HEADS, groups, t), lambda i, j: (0, j, i))
    full_spec = pl.BlockSpec(hk + (t,), lambda i, j: (0, 0, i))
    return pl.pallas_call(
        _peer_dense_body,
        grid=(nt, N_EXPERTS // e),
        in_specs=[pl.BlockSpec((t, d), lambda i, j: (i, 0)),
                  pl.BlockSpec((e, d), lambda i, j: (j, 0)),
                  pl.BlockSpec((e, d), lambda i, j: (j, 0)),
                  row_spec, row_spec, full_spec, full_spec],
        out_specs=pl.BlockSpec((t, d), lambda i, j: (i, 0)),
        out_shape=jax.ShapeDtypeStruct((n, d), jnp.float32),
        compiler_params=pltpu.CompilerParams(dimension_semantics=("parallel", "arbitrary"),
                                             vmem_limit_bytes=VMEM_LIMIT_BYTES),
        name="peer_dense",
    )(h, u, v, n1, c1, r2, e2)


ATT_TILE = 128
MASKED_BELOW = -0.5e30
KEY_PAIR = 4


def t5_bucket(dist):
    n = jnp.maximum(dist, 0)
    max_exact = N_BUCKETS // 2
    nf = jnp.maximum(n, max_exact).astype(jnp.float32)
    large = max_exact + (jnp.log(nf / max_exact) / math.log(MAX_DISTANCE / max_exact)
                         * (N_BUCKETS - max_exact)).astype(jnp.int32)
    large = jnp.minimum(large, N_BUCKETS - 1)
    return jnp.where(n < max_exact, n, large)


def _softmax_cols(s):
    m = jnp.max(s, axis=0, keepdims=True)
    e = jnp.where(s > MASKED_BELOW, jnp.exp(s - m), 0.0)
    l = jnp.sum(e, axis=0, keepdims=True)
    return e / jnp.maximum(l, TINY)


def _dot_f32_by_01(mat01, x):
    return sum(jnp.dot(mat01, part, preferred_element_type=jnp.float32) for part in _split3(x))


def _x_dot_01(x, mat01):
    return sum(jnp.dot(part, mat01, preferred_element_type=jnp.float32) for part in _split3(x))


def _online_tiles(k_ref, v_ref, qT, lo, hi, bias_of, mask_of, tile):
    width = qT.shape[1]
    tn = (((0,), (0,)), ((), ()))

    def step(j, carry):
        m, l, acc = carry
        rows = pl.ds(pl.multiple_of(j * tile, tile), tile)
        s = jnp.dot(k_ref[rows, :], qT, preferred_element_type=jnp.float32) + bias_of(j)
        s = mask_of(j, s)
        m_new = jnp.maximum(m, jnp.max(s, axis=0, keepdims=True))
        alpha = jnp.exp(m - m_new)
        e = jnp.where(s > MASKED_BELOW, jnp.exp(s - m_new), 0.0)
        l = alpha * l + jnp.sum(e, axis=0, keepdims=True)
        pv = lax.dot_general(v_ref[rows, :], e.astype(jnp.bfloat16), tn,
                             preferred_element_type=jnp.float32)
        return m_new, l, alpha * acc + pv

    init = (jnp.full((1, width), NEG_INF, jnp.float32), jnp.zeros((1, width), jnp.float32),
            jnp.zeros((HEAD_DIM, width), jnp.float32))
    m, l, acc = lax.fori_loop(lo, hi, step, init)
    return acc / jnp.maximum(l, TINY)


def _nsa_core(qT, gate_logits, q_pos, kc_ref, vc_ref, ks_ref, vs_ref, kw_ref, vw_ref, cmp_bias,
              sel_range, sel_bias, win_range, win_bias, *, tq, n_sel, n_blocks, sel_tile, win_tile):
    tn = (((0,), (0,)), ((), ()))
    p_cmp = _softmax_cols(jnp.dot(kc_ref[...], qT, preferred_element_type=jnp.float32) + cmp_bias)
    o_cmp = lax.dot_general(vc_ref[...], p_cmp.astype(jnp.bfloat16), tn, preferred_element_type=jnp.float32)
    ncp = p_cmp.shape[0]
    nb = lax.broadcasted_iota(jnp.int32, (n_blocks, ncp), 0) * SEL_BLOCK
    cs = lax.broadcasted_iota(jnp.int32, (n_blocks, ncp), 1) * CMP_STRIDE
    overlap = jnp.where((cs < nb + SEL_BLOCK) & (cs + CMP_LEN > nb), 1.0, 0.0).astype(jnp.bfloat16)
    per_head = _dot_f32_by_01(overlap, p_cmp)
    imp = sum(per_head[:, r * tq:(r + 1) * tq] for r in range(NSA_GROUP))
    blk = lax.broadcasted_iota(jnp.int32, (n_blocks, tq), 0)
    cur = q_pos // SEL_BLOCK
    forced = (blk == 0) | (blk == cur) | (blk == cur - 1)
    score = jnp.where(blk <= cur, jnp.where(forced, FORCE_SCORE, imp), -1.0)
    rank = jnp.zeros((n_blocks, tq), jnp.float32)
    for mrow in range(n_blocks):
        row = score[mrow:mrow + 1, :]
        ahead = (row > score) | ((row == score) & (blk > mrow))
        rank = rank + jnp.where(ahead, 1.0, 0.0)
    sel = jnp.where(rank < n_sel, 1.0, 0.0).astype(jnp.bfloat16)
    sel = jnp.concatenate([sel] * NSA_GROUP, axis=1)

    def sel_mask(j, s):
        kb = (j * sel_tile + lax.broadcasted_iota(jnp.int32, (sel_tile, n_blocks), 0)) // SEL_BLOCK
        nn = lax.broadcasted_iota(jnp.int32, (sel_tile, n_blocks), 1)
        expand = jnp.where(kb == nn, 1.0, 0.0).astype(jnp.bfloat16)
        mk = jnp.dot(expand, sel, preferred_element_type=jnp.float32)
        return jnp.where(mk > 0.5, s, NEG_INF)

    o_sel = _online_tiles(ks_ref, vs_ref, qT, sel_range[0], sel_range[1], sel_bias, sel_mask, sel_tile)
    o_win = _online_tiles(kw_ref, vw_ref, qT, win_range[0], win_range[1], win_bias, lambda j, s: s, win_tile)
    g = 1.0 / (1.0 + jnp.exp(-gate_logits))
    return g[0:1] * o_cmp + g[1:2] * o_sel + g[2:3] * o_win


def _nsa_prompt_body(qT_ref, gate_ref, kc_ref, vc_ref, ks_ref, vs_ref, kw_ref, vw_ref, cb_ref, tz_ref,
                     o_ref, *, n_sel):
    i = pl.program_id(2)
    tq = ATT_TILE
    q_pos = i * tq + lax.broadcasted_iota(jnp.int32, (1, tq), 1)
    masked_tile = tz_ref.shape[0] - 1

    def pair_bias(max_delta):
        def bias(jj):
            tiles = []
            for j in (KEY_PAIR * jj + t for t in range(KEY_PAIR)):
                idx = jnp.where(j > i, masked_tile, jnp.minimum(i - j, max_delta))
                tiles.append(tz_ref[idx])
            return jnp.concatenate(tiles, axis=0)
        return bias

    first_win = jnp.maximum(i - WINDOW // ATT_TILE, 0)
    o_ref[...] = _nsa_core(
        qT_ref[...], gate_ref[...], q_pos, kc_ref, vc_ref, ks_ref, vs_ref, kw_ref, vw_ref, cb_ref[...],
        (0, i // KEY_PAIR + 1), pair_bias(2),
        (first_win // KEY_PAIR, i // KEY_PAIR + 1), pair_bias(masked_tile),
        tq=tq, n_sel=n_sel, n_blocks=ks_ref.shape[0] // SEL_BLOCK,
        sel_tile=KEY_PAIR * ATT_TILE, win_tile=KEY_PAIR * ATT_TILE)


def _bias_lanes(tbl, dist, ok):
    onehot = (t5_bucket(dist)[..., None] == jnp.arange(N_BUCKETS)).astype(jnp.float32)
    vals = jnp.dot(onehot.reshape(-1, N_BUCKETS), tbl.reshape(N_BUCKETS, -1),
                   precision=lax.Precision.HIGHEST).reshape(dist.shape + tbl.shape[1:])
    b = jnp.where(ok[..., None, None], vals, NEG_INF)
    nd = b.ndim
    b = jnp.moveaxis(b, (nd - 2, nd - 1), (0, nd - 2))
    return b.reshape(b.shape[:-2] + (b.shape[-2] * b.shape[-1],))


def nsa_prompt_attention(q, gate_logits, kcmp, vcmp, ks, vs, kw, vw, rel_bias):
    B, S, G, R, dh = q.shape
    t = ATT_TILE
    n_qt = S // t
    bf = jnp.bfloat16
    tbl = rel_bias.reshape(N_BUCKETS, G, R).astype(jnp.float32)
    qT = (q * (HEAD_DIM ** -0.5)).astype(bf).reshape(B, n_qt, t, G, R, dh)
    qT = qT.transpose(0, 3, 1, 5, 4, 2).reshape(B, G, n_qt, dh, R * t)
    gT = gate_logits.reshape(B, n_qt, t, G, R, 3).transpose(0, 3, 1, 5, 4, 2).reshape(B, G, n_qt, 3, R * t)
    n_cmp = kcmp.shape[1]
    ncp = _round_up(n_cmp, 8)
    padc = lambda x: jnp.pad(x.astype(bf).transpose(0, 2, 1, 3), ((0, 0), (0, 0), (0, ncp - n_cmp), (0, 0)))
    tr = lambda x: x.astype(bf).transpose(0, 2, 1, 3)
    qp = jnp.arange(S).reshape(n_qt, 1, t)
    c_idx = jnp.arange(ncp).reshape(1, ncp, 1)
    dist = qp - (c_idx * CMP_STRIDE + CMP_LEN - 1)
    cb = _bias_lanes(tbl, dist, (dist >= 0) & (c_idx < n_cmp))
    n_delta = WINDOW // t + 2
    d = (jnp.arange(n_delta).reshape(-1, 1, 1) * t + jnp.arange(t).reshape(1, 1, t)
         - jnp.arange(t).reshape(1, t, 1))
    tz = _bias_lanes(tbl, d, (d >= 0) & (d < WINDOW))
    body = functools.partial(_nsa_prompt_body, n_sel=min(SEL_TOPK, S // SEL_BLOCK))
    kv_spec = pl.BlockSpec((None, None, S, dh), lambda b, g, i: (b, g, 0, 0))
    cmp_spec = pl.BlockSpec((None, None, ncp, dh), lambda b, g, i: (b, g, 0, 0))
    oT = pl.pallas_call(
        body,
        grid=(B, G, n_qt),
        in_specs=[pl.BlockSpec((None, None, None, dh, R * t), lambda b, g, i: (b, g, i, 0, 0)),
                  pl.BlockSpec((None, None, None, 3, R * t), lambda b, g, i: (b, g, i, 0, 0)),
                  cmp_spec, cmp_spec, kv_spec, kv_spec, kv_spec, kv_spec,
                  pl.BlockSpec((None, None, ncp, R * t), lambda b, g, i: (g, i, 0, 0)),
                  pl.BlockSpec((None,) + tz.shape[1:], lambda b, g, i: (g, 0, 0, 0))],
        out_specs=pl.BlockSpec((None, None, None, dh, R * t), lambda b, g, i: (b, g, i, 0, 0)),
        out_shape=jax.ShapeDtypeStruct((B, G, n_qt, dh, R * t), jnp.float32),
        compiler_params=pltpu.CompilerParams(dimension_semantics=("parallel", "parallel", "arbitrary"),
                                             vmem_limit_bytes=VMEM_LIMIT_BYTES),
        name="nsa_prompt_attention",
    )(qT, gT, padc(kcmp), padc(vcmp), tr(ks), tr(vs), tr(kw), tr(vw), cb, tz)
    o = oT.reshape(B, G, n_qt, dh, R, t).transpose(0, 2, 5, 1, 4, 3)
    return o.reshape(B, S, G * R * dh)


SAMPLE_PAGES_PER_STEP = 8


def _softmax_lanes(s):
    m = jnp.max(s, axis=1, keepdims=True)
    e = jnp.where(s > MASKED_BELOW, jnp.exp(s - m), 0.0)
    return e, jnp.sum(e, axis=1, keepdims=True)


def _attend_lanes(qbd, k2, v2, bias, mask=None):
    s = jnp.dot(qbd, k2, preferred_element_type=jnp.float32) + bias
    if mask is not None:
        s = jnp.where(mask > 0.5, s, NEG_INF)
    e, l = _softmax_lanes(s)
    o = lax.dot_general(e.astype(jnp.bfloat16), v2, (((1,), (1,)), ((), ())),
                        preferred_element_type=jnp.float32)
    return o / jnp.maximum(l, TINY)


def _nsa_sample_body(pt_ref, qbd_ref, gate_ref, kc_ref, vc_ref, new_ref, wk_ref, wv_ref, cb_ref, sb_ref, wb_ref,
                     *rest, n_sel, tq, past, n_blocks):
    pps = SAMPLE_PAGES_PER_STEP
    k_pages, v_pages = rest[:pps], rest[pps:2 * pps]
    o_ref, ks_ref, vs_ref = rest[2 * pps:]
    step = pl.program_id(1)
    rows = ks_ref.shape[0]
    width = qbd_ref.shape[0]
    for i in range(pps):
        cols = pl.ds(pl.multiple_of((step * pps + i) * PAGE_SIZE, PAGE_SIZE), PAGE_SIZE)
        ks_ref[:, cols] = k_pages[i][...].reshape(rows, PAGE_SIZE).astype(jnp.bfloat16)
        vs_ref[:, cols] = v_pages[i][...].reshape(rows, PAGE_SIZE).astype(jnp.bfloat16)

    @pl.when(step == pl.num_programs(1) - 1)
    def _():
        qbd = qbd_ref[...]
        n_keys = ks_ref.shape[1]
        ks_ref[:, past:n_keys] = new_ref[0]
        vs_ref[:, past:n_keys] = new_ref[1]
        s_c = lax.dot_general(qbd, kc_ref[...], (((1,), (1,)), ((), ())),
                              preferred_element_type=jnp.float32) + cb_ref[...]
        e_c, l_c = _softmax_lanes(s_c)
        p_cmp = e_c / jnp.maximum(l_c, TINY)
        o_cmp = jnp.dot(p_cmp.astype(jnp.bfloat16), vc_ref[...], preferred_element_type=jnp.float32)
        ncp = p_cmp.shape[1]
        cs = lax.broadcasted_iota(jnp.int32, (ncp, LANES), 0) * CMP_STRIDE
        nb = lax.broadcasted_iota(jnp.int32, (ncp, LANES), 1) * SEL_BLOCK
        overlap_t = jnp.where((cs < nb + SEL_BLOCK) & (cs + CMP_LEN > nb), 1.0, 0.0).astype(jnp.bfloat16)
        ra = lax.broadcasted_iota(jnp.int32, (width, width), 0)
        rb = lax.broadcasted_iota(jnp.int32, (width, width), 1)
        per_group = NSA_GROUP * tq
        same_query = jnp.where((ra // per_group == rb // per_group) & (ra % tq == rb % tq), 1.0, 0.0)
        imp = _dot_f32_by_01(same_query.astype(jnp.bfloat16), _x_dot_01(p_cmp, overlap_t))
        blk = lax.broadcasted_iota(jnp.int32, (width, LANES), 1)
        cur = (past + lax.broadcasted_iota(jnp.int32, (width, LANES), 0) % tq) // SEL_BLOCK
        forced = (blk == 0) | (blk == cur) | (blk == cur - 1)
        score = jnp.where(blk <= cur, jnp.where(forced, FORCE_SCORE, imp), -1.0)
        score = jnp.where(blk < n_blocks, score, -2.0)
        score_t = score.T
        bt = lax.broadcasted_iota(jnp.int32, (LANES, width), 0)
        rank = jnp.zeros((LANES, width), jnp.float32)
        for mrow in range(n_blocks):
            row = score_t[mrow:mrow + 1, :]
            ahead = (row > score_t) | ((row == score_t) & (bt > mrow))
            rank = rank + jnp.where(ahead, 1.0, 0.0)
        sel = jnp.where(rank < n_sel, 1.0, 0.0).T.astype(jnp.bfloat16)
        kb = lax.broadcasted_iota(jnp.int32, (LANES, n_keys), 1) // SEL_BLOCK
        nn = lax.broadcasted_iota(jnp.int32, (LANES, n_keys), 0)
        expand = jnp.where(kb == nn, 1.0, 0.0).astype(jnp.bfloat16)
        mask = jnp.dot(sel, expand, preferred_element_type=jnp.float32)
        o_sel = _attend_lanes(qbd, ks_ref[...], vs_ref[...], sb_ref[...], mask)
        window = lambda ref, new: jnp.concatenate(
            [ref[...].reshape(rows, ref.shape[-1]).astype(jnp.bfloat16), new], axis=1)
        o_win = _attend_lanes(qbd, window(wk_ref, new_ref[2]), window(wv_ref, new_ref[3]), wb_ref[...])
        g = 1.0 / (1.0 + jnp.exp(-gate_ref[...]))
        o = g[:, 0:1] * o_cmp + g[:, 1:2] * o_sel + g[:, 2:3] * o_win
        r = lax.broadcasted_iota(jnp.int32, (width, rows), 0) // per_group
        c = lax.broadcasted_iota(jnp.int32, (width, rows), 1) // HEAD_DIM
        o = jnp.where(r == c, o, 0.0)
        fa = lax.broadcasted_iota(jnp.int32, (rows, LANES), 0) % HEAD_DIM
        fb = lax.broadcasted_iota(jnp.int32, (rows, LANES), 1)
        fold = jnp.where(fa == fb, 1.0, 0.0).astype(jnp.bfloat16)
        o_ref[...] = _x_dot_01(o, fold)[:, :HEAD_DIM]


def _rows_table(table):
    g, k, w = table.shape
    return table.transpose(0, 2, 1).reshape(g * w, k)


def nsa_sample_attention(q, gate_logits, kcmp, vcmp, new_kv, sk_pool, sv_pool, wk_buf, wv_buf, page_table, rel_bias):
    DB, T, G, R, dh = q.shape
    bf = jnp.bfloat16
    width = G * R * T
    rows = G * dh
    n_pages = page_table.shape[1]
    past = n_pages * PAGE_SIZE
    n_keys = past + PAGE_SIZE
    n_blocks = _round_up(past + T, SEL_BLOCK) // SEL_BLOCK
    tbl = rel_bias.reshape(N_BUCKETS, G, R).astype(jnp.float32)
    q_pos = past + jnp.arange(T)
    qg = (q * (HEAD_DIM ** -0.5)).transpose(0, 2, 3, 1, 4)
    same_group = jnp.eye(G, dtype=qg.dtype).reshape(1, G, 1, 1, G, 1)
    qbd = (qg[:, :, :, :, None, :] * same_group).astype(bf).reshape(DB, width, rows)
    gates = jnp.pad(gate_logits.transpose(0, 2, 3, 1, 4).reshape(DB, width, 3), ((0, 0), (0, 0), (0, 5)))
    n_cmp = n_blocks * SEL_BLOCK // CMP_STRIDE - CMP_LEN // CMP_STRIDE + 1
    ncp = kcmp.shape[1]
    c_idx = jnp.arange(ncp).reshape(ncp, 1)
    dist = q_pos.reshape(1, T) - (c_idx * CMP_STRIDE + CMP_LEN - 1)
    cb = _rows_table(_bias_lanes(tbl, dist, (dist >= 0) & (c_idx < n_cmp)))
    k_idx = jnp.arange(n_keys).reshape(-1, 1)
    dist = q_pos.reshape(1, T) - k_idx
    sb = _rows_table(_bias_lanes(tbl, dist, dist >= 0))
    wbuf = wk_buf.shape[1]
    n_win = wbuf + PAGE_SIZE
    w_idx = jnp.arange(n_win).reshape(-1, 1)
    w_pos = past - wbuf + w_idx
    dist = q_pos.reshape(1, T) - w_pos
    wb = _rows_table(_bias_lanes(tbl, dist, (dist >= 0) & (dist < WINDOW) & (w_pos >= 0) & (w_idx < wbuf + T)))
    new_t = lambda x: jnp.pad(x.astype(bf).transpose(0, 2, 3, 1),
                              ((0, 0), (0, 0), (0, 0), (0, PAGE_SIZE - T))).reshape(DB, rows, PAGE_SIZE)
    new = jnp.stack([new_t(x) for x in new_kv], axis=1)
    native = lambda x: x.transpose(0, 2, 3, 1)
    pps = SAMPLE_PAGES_PER_STEP
    body = functools.partial(_nsa_sample_body, n_sel=min(SEL_TOPK, n_blocks), tq=T, past=past, n_blocks=n_blocks)
    per_b = lambda shape: pl.BlockSpec((None,) + shape, lambda b, s, pt: (b,) + (0,) * len(shape))
    const = lambda x: pl.BlockSpec(x.shape, lambda b, s, pt: (0,) * x.ndim)
    page = lambda i: pl.BlockSpec((None, G, dh, PAGE_SIZE), lambda b, s, pt: (pt[b, s * pps + i], 0, 0, 0))
    sk_t, sv_t = native(sk_pool), native(sv_pool)
    o = pl.pallas_call(
        body,
        grid_spec=pltpu.PrefetchScalarGridSpec(
            num_scalar_prefetch=1,
            grid=(DB, n_pages // pps),
            in_specs=[per_b((width, rows)), per_b((width, 8)), per_b((ncp, rows)), per_b((ncp, rows)),
                      per_b((4, rows, PAGE_SIZE)), per_b((G, dh, wbuf)), per_b((G, dh, wbuf)),
                      const(cb), const(sb), const(wb)]
                     + [page(i) for i in range(pps)] * 2,
            out_specs=per_b((width, dh)),
            scratch_shapes=[pltpu.VMEM((rows, n_keys), bf), pltpu.VMEM((rows, n_keys), bf)]),
        out_shape=jax.ShapeDtypeStruct((DB, width, dh), jnp.float32),
        compiler_params=pltpu.CompilerParams(dimension_semantics=("parallel", "arbitrary"),
                                             vmem_limit_bytes=VMEM_LIMIT_BYTES),
        name="nsa_sample_attention",
    )(page_table, qbd, gates, kcmp, vcmp, new, native(wk_buf), native(wv_buf), cb, sb, wb,
      *([sk_t] * pps), *([sv_t] * pps))
    return o.reshape(DB, G, R, T, dh).transpose(0, 3, 1, 2, 4).reshape(DB, T, G * R * dh)


FOX_Q_TILE = 512
FOX_K_TILE = 256


def _softmax_step(s, v, carry, masked):
    m, l, acc = carry
    m_new = jnp.maximum(m, jnp.max(s, axis=0, keepdims=True))
    alpha = jnp.exp(m - m_new)
    e = jnp.exp(s - m_new)
    if masked:
        e = jnp.where(s > MASKED_BELOW, e, 0.0)
    l = alpha * l + jnp.sum(e, axis=0, keepdims=True)
    pv = lax.dot_general(v, e.astype(jnp.bfloat16), (((0,), (0,)), ((), ())),
                         preferred_element_type=jnp.float32)
    return m_new, l, alpha * acc + pv


def _fox_prompt_body(qT_ref, qd_ref, k_ref, kd_ref, v_ref, o_ref):
    i = pl.program_id(2)
    qT = qT_ref[...]
    q_decay = qd_ref[...]
    width = qT.shape[1]

    def tile(j):
        rows = pl.ds(pl.multiple_of(j * FOX_K_TILE, FOX_K_TILE), FOX_K_TILE)
        s = jnp.dot(k_ref[rows, :], qT, preferred_element_type=jnp.float32)
        s = s + lax.dot_general(kd_ref[:, rows], q_decay, (((0,), (0,)), ((), ())),
                                preferred_element_type=jnp.float32)
        return s, v_ref[rows, :]

    def full_step(j, carry):
        s, v = tile(j)
        return _softmax_step(s, v, carry, masked=False)

    carry = (jnp.full((1, width), NEG_INF, jnp.float32), jnp.zeros((1, width), jnp.float32),
             jnp.zeros((HEAD_DIM, width), jnp.float32))
    ratio = FOX_Q_TILE // FOX_K_TILE
    carry = lax.fori_loop(0, i * ratio, full_step, carry)
    q_pos = i * FOX_Q_TILE + lax.broadcasted_iota(jnp.int32, (FOX_K_TILE, width), 1)
    for dj in range(ratio):
        j = i * ratio + dj
        s, v = tile(j)
        k_pos = j * FOX_K_TILE + lax.broadcasted_iota(jnp.int32, (FOX_K_TILE, width), 0)
        s = jnp.where(k_pos <= q_pos, s, NEG_INF)
        carry = _softmax_step(s, v, carry, masked=True)
    m, l, acc = carry
    o_ref[...] = acc / jnp.maximum(l, TINY)


DECAY_COLS = 8


def fox_prompt_attention(q, k, v, logf):
    B, S, H, dh = q.shape
    bf = jnp.bfloat16
    c = jnp.cumsum(logf, axis=1).transpose(0, 2, 1)
    hi, mid, lo = _split3_outside_kernel(c)
    one, zero = jnp.ones_like(hi), jnp.zeros_like(hi)
    k_decay = jnp.stack([-hi, -mid, -lo, one, one, one, zero, zero], axis=2)
    q_decay = jnp.stack([one, one, one, hi, mid, lo, zero, zero], axis=2)
    kb = k.astype(bf).transpose(0, 2, 1, 3)
    qT = (q * (dh ** -0.5)).astype(bf).transpose(0, 2, 3, 1)
    vb = v.astype(bf).transpose(0, 2, 1, 3)
    per_head = lambda rows: pl.BlockSpec((None, None, S, rows), lambda b, h, i: (b, h, 0, 0))
    per_tile = lambda rows: pl.BlockSpec((None, None, rows, FOX_Q_TILE), lambda b, h, i: (b, h, 0, i))
    oT = pl.pallas_call(
        _fox_prompt_body,
        grid=(B, H, S // FOX_Q_TILE),
        in_specs=[per_tile(dh), per_tile(DECAY_COLS), per_head(dh),
                  pl.BlockSpec((None, None, DECAY_COLS, S), lambda b, h, i: (b, h, 0, 0)), per_head(dh)],
        out_specs=per_tile(dh),
        out_shape=jax.ShapeDtypeStruct((B, H, dh, S), jnp.float32),
        compiler_params=pltpu.CompilerParams(dimension_semantics=("parallel", "parallel", "arbitrary"),
                                             vmem_limit_bytes=VMEM_LIMIT_BYTES),
        name="fox_prompt_attention",
    )(qT, q_decay, kb, k_decay, vb)
    return oT.transpose(0, 3, 1, 2).reshape(B, S, H * dh)


FOX_PAGES_PER_STEP = 4


def _fox_sample_body(pt_ref, qbd_ref, bp_ref, kn_ref, vn_ref, bn_ref, *rest, n_q):
    pps = FOX_PAGES_PER_STEP
    k_pages, v_pages = rest[:pps], rest[pps:2 * pps]
    o_ref, m_ref, l_ref, acc_ref = rest[2 * pps:]
    p = pl.program_id(1)
    width = qbd_ref.shape[0]
    rows = kn_ref.shape[0]

    @pl.when(p == 0)
    def _():
        m_ref[...] = jnp.full(m_ref.shape, NEG_INF, jnp.float32)
        l_ref[...] = jnp.zeros(l_ref.shape, jnp.float32)
        acc_ref[...] = jnp.zeros(acc_ref.shape, jnp.float32)

    def attend(k2, v2, bias_t):
        s_t = jnp.dot(qbd_ref[...], k2, preferred_element_type=jnp.float32).T + bias_t
        m = m_ref[...]
        m_new = jnp.maximum(m, jnp.max(s_t, axis=0, keepdims=True))
        alpha = jnp.exp(m - m_new)
        e = jnp.where(s_t > MASKED_BELOW, jnp.exp(s_t - m_new), 0.0)
        m_ref[...] = m_new
        l_ref[...] = alpha * l_ref[...] + jnp.sum(e, axis=0, keepdims=True)
        acc_ref[...] = alpha * acc_ref[...] + jnp.dot(v2, e.astype(jnp.bfloat16),
                                                      preferred_element_type=jnp.float32)

    pages = lambda refs: jnp.concatenate(
        [ref[...].reshape(rows, PAGE_SIZE).astype(jnp.bfloat16) for ref in refs], axis=1)
    attend(pages(k_pages), pages(v_pages), bp_ref[...].reshape(pps * PAGE_SIZE, width))

    @pl.when(p == pl.num_programs(1) - 1)
    def _():
        attend(kn_ref[...], vn_ref[...], bn_ref[...])
        out = acc_ref[...] / jnp.maximum(l_ref[...], TINY)
        r = lax.broadcasted_iota(jnp.int32, (rows, width), 0) // HEAD_DIM
        c = lax.broadcasted_iota(jnp.int32, (rows, width), 1) // n_q
        out = jnp.where(r == c, out, 0.0)
        la = lax.broadcasted_iota(jnp.int32, (width, width), 0) % n_q
        lb = lax.broadcasted_iota(jnp.int32, (width, width), 1)
        gather_q = jnp.where(la == lb, 1.0, 0.0).astype(jnp.bfloat16)
        o_ref[...] = _x_dot_01(out, gather_q)[:, :n_q]


def fox_sample_attention(q, k, v, logf, k_pool, v_pool, f_pool, page_table):
    DB, T, H, dh = q.shape
    n_pages = page_table.shape[1]
    past = n_pages * PAGE_SIZE
    bf = jnp.bfloat16
    width = H * T
    rows = H * dh
    f_all = jnp.concatenate([f_pool[page_table].reshape(DB, past, H), logf], axis=1)
    c = jnp.cumsum(f_all, axis=1)
    c_q = c[:, past:].transpose(0, 2, 1).reshape(DB, 1, 1, width)
    c_k = jnp.repeat(c[:, :past].reshape(DB, n_pages, PAGE_SIZE, H), T, axis=-1)
    bias_past = c_q - c_k
    tok = jnp.arange(PAGE_SIZE).reshape(1, PAGE_SIZE, 1)
    qi = (jnp.arange(width) % T).reshape(1, 1, width)
    c_new = jnp.pad(jnp.repeat(c[:, past:], T, axis=-1), ((0, 0), (0, PAGE_SIZE - T), (0, 0)))
    bias_new = jnp.where((tok <= qi) & (tok < T), c_q[:, 0] - c_new, NEG_INF)
    qh = (q * (dh ** -0.5)).transpose(0, 2, 1, 3)
    same_head = jnp.eye(H, dtype=qh.dtype).reshape(1, H, 1, H, 1)
    qbd = (qh[:, :, :, None, :] * same_head).astype(bf).reshape(DB, width, rows)
    new_t = lambda x: jnp.pad(x.astype(bf).transpose(0, 2, 3, 1),
                              ((0, 0), (0, 0), (0, 0), (0, PAGE_SIZE - T))).reshape(DB, rows, PAGE_SIZE)
    pool_t = lambda x: x.transpose(0, 2, 3, 1)
    body = functools.partial(_fox_sample_body, n_q=T)
    pps = FOX_PAGES_PER_STEP
    page = lambda i: pl.BlockSpec((None, H, dh, PAGE_SIZE), lambda b, p, pt: (pt[b, p * pps + i], 0, 0, 0))
    per_b = lambda shape: pl.BlockSpec((None,) + shape, lambda b, p, pt: (b,) + (0,) * len(shape))
    kt, vt = pool_t(k_pool), pool_t(v_pool)
    o = pl.pallas_call(
        body,
        grid_spec=pltpu.PrefetchScalarGridSpec(
            num_scalar_prefetch=1,
            grid=(DB, n_pages // pps),
            in_specs=[per_b((width, rows)),
                      pl.BlockSpec((None, pps, PAGE_SIZE, width), lambda b, p, pt: (b, p, 0, 0)),
                      per_b((rows, PAGE_SIZE)), per_b((rows, PAGE_SIZE)), per_b((PAGE_SIZE, width))]
                     + [page(i) for i in range(pps)] * 2,
            out_specs=per_b((rows, T)),
            scratch_shapes=[pltpu.VMEM((1, width), jnp.float32), pltpu.VMEM((1, width), jnp.float32),
                            pltpu.VMEM((rows, width), jnp.float32)]),
        out_shape=jax.ShapeDtypeStruct((DB, rows, T), jnp.float32),
        compiler_params=pltpu.CompilerParams(dimension_semantics=("parallel", "arbitrary"),
                                             vmem_limit_bytes=VMEM_LIMIT_BYTES),
        name="fox_sample_attention",
    )(page_table, qbd, bias_past, new_t(k), new_t(v), bias_new, *([kt] * pps), *([vt] * pps))
    return o.reshape(DB, H, dh, T).transpose(0, 3, 1, 2).reshape(DB, T, H * dh)


def _compress_body(pt_ref, newk_ref, newv_ref, pe_ref, w1_ref, w2_ref, *rest, n_tok, n_out):
    pps = SAMPLE_PAGES_PER_STEP
    k_pages, v_pages = rest[:pps], rest[pps:2 * pps]
    ko_ref, vo_ref, xk_ref, xv_ref, flat_ref = rest[2 * pps:]
    step = pl.program_id(1)
    rows = newk_ref.shape[1]
    pairs = rows // LANES
    for i in range(pps):
        dst = pl.ds(pl.multiple_of((step * pps + i) * PAGE_SIZE, PAGE_SIZE), PAGE_SIZE)
        for x_ref, pages in ((xk_ref, k_pages), (xv_ref, v_pages)):
            page_t = pages[i][...].reshape(rows, PAGE_SIZE).T
            for gp in range(pairs):
                x_ref[gp, dst, :] = page_t[:, gp * LANES:(gp + 1) * LANES]

    @pl.when(step == pl.num_programs(1) - 1)
    def _():
        n_new = newk_ref.shape[0]
        tail = xk_ref.shape[1] - n_tok - n_new
        for which, (x_ref, new_ref, o_ref) in enumerate(((xk_ref, newk_ref, ko_ref), (xv_ref, newv_ref, vo_ref))):
            for gp in range(pairs):
                x_ref[gp, n_tok:n_tok + n_new, :] = new_ref[:, gp * LANES:(gp + 1) * LANES]
                x_ref[gp, n_tok + n_new:, :] = jnp.zeros((tail, LANES), jnp.float32)
            for l in range(CMP_LEN):
                for gp in range(pairs):
                    blk = x_ref[gp, pl.ds(l, n_out, stride=CMP_STRIDE), :]
                    blk = blk + pe_ref[which, l:l + 1, :]
                    flat_ref[gp * n_out:(gp + 1) * n_out, l * LANES:(l + 1) * LANES] = blk.astype(jnp.bfloat16)
            hidden = _gelu_exact(jnp.dot(flat_ref[...], w1_ref[which], preferred_element_type=jnp.float32))
            out = jnp.dot(hidden.astype(jnp.bfloat16), w2_ref[which], preferred_element_type=jnp.float32)
            o_ref[...] = jnp.concatenate([out[gp * n_out:(gp + 1) * n_out] for gp in range(pairs)],
                                         axis=1).astype(o_ref.dtype)


def _pair_weights(w1, w2, pe):
    dh = w2.shape[0]
    eye2 = jnp.eye(2, dtype=w1.dtype)
    w1p = jnp.einsum('lde,pq->lpdqe', w1.reshape(CMP_LEN, dh, dh), eye2).reshape(CMP_LEN * 2 * dh, 2 * dh)
    w2p = jnp.einsum('de,pq->pdqe', w2, eye2).reshape(2 * dh, 2 * dh)
    pep = jnp.concatenate([pe, pe], axis=1)
    return w1p, w2p, pep


def compress_paged(k_pool, v_pool, new_k, new_v, page_table, weights_k, weights_v):
    DB, T, G, dh = new_k.shape
    bf = jnp.bfloat16
    rows = G * dh
    n_pages = page_table.shape[1]
    past = n_pages * PAGE_SIZE
    lp = _round_up(past + T, SEL_BLOCK)
    n_cmp = lp // CMP_STRIDE - CMP_LEN // CMP_STRIDE + 1
    n_out = _round_up(n_cmp, 8)
    x_rows = _round_up(CMP_LEN + CMP_STRIDE * (n_out - 1), PAGE_SIZE)
    wk = _pair_weights(weights_k[1], weights_k[2], weights_k[0])
    wv = _pair_weights(weights_v[1], weights_v[2], weights_v[0])
    w1 = jnp.stack([wk[0], wv[0]]).astype(bf)
    w2 = jnp.stack([wk[1], wv[1]]).astype(bf)
    pe = jnp.stack([wk[2], wv[2]])
    native = lambda x: x.transpose(0, 2, 3, 1)
    pps = SAMPLE_PAGES_PER_STEP
    body = functools.partial(_compress_body, n_tok=past, n_out=n_out)
    per_b = lambda shape: pl.BlockSpec((None,) + shape, lambda b, s, pt: (b,) + (0,) * len(shape))
    const = lambda x: pl.BlockSpec(x.shape, lambda b, s, pt: (0,) * x.ndim)
    page = lambda i: pl.BlockSpec((None, G, dh, PAGE_SIZE), lambda b, s, pt: (pt[b, s * pps + i], 0, 0, 0))
    kt, vt = native(k_pool), native(v_pool)
    out_shape = jax.ShapeDtypeStruct((DB, n_out, rows), bf)
    return pl.pallas_call(
        body,
        grid_spec=pltpu.PrefetchScalarGridSpec(
            num_scalar_prefetch=1,
            grid=(DB, n_pages // pps),
            in_specs=[per_b((T, rows)), per_b((T, rows)), const(pe), const(w1), const(w2)]
                     + [page(i) for i in range(pps)] * 2,
            out_specs=[per_b((n_out, rows)), per_b((n_out, rows))],
            scratch_shapes=[pltpu.VMEM((rows // LANES, x_rows, LANES), jnp.float32),
                            pltpu.VMEM((rows // LANES, x_rows, LANES), jnp.float32),
                            pltpu.VMEM((rows // LANES * n_out, CMP_LEN * LANES), bf)]),
        out_shape=[out_shape, out_shape],
        compiler_params=pltpu.CompilerParams(dimension_semantics=("parallel", "arbitrary"),
                                             vmem_limit_bytes=VMEM_LIMIT_BYTES),
        name="nsa_compress_paged",
    )(page_table, new_k.reshape(DB, T, rows), new_v.reshape(DB, T, rows), pe, w1, w2, *([kt] * pps), *([vt] * pps))


def compress(x, pe, w1, w2):
    B, T, G, dh = x.shape
    r = CMP_LEN // CMP_STRIDE
    n_chunks = T // CMP_STRIDE
    nc = n_chunks - r + 1
    ch = x.reshape(B, n_chunks, CMP_STRIDE, G, dh)
    blk = jnp.concatenate([ch[:, m:m + nc] for m in range(r)], axis=2)
    blk = blk + pe[:, None, :]
    flat = blk.transpose(0, 1, 3, 2, 4).reshape(B, nc, G, CMP_LEN * dh)
    return jax.nn.gelu(flat @ w1, approximate=False) @ w2


def nsa_split(proj):
    B, T, _ = proj.shape
    nq = NSA_HEADS * HEAD_DIM
    nkv = NSA_KV_HEADS * HEAD_DIM
    q = proj[..., :nq].reshape(B, T, NSA_KV_HEADS, NSA_GROUP, HEAD_DIM)
    kv = proj[..., nq:nq + 6 * nkv].reshape(B, T, 6, NSA_KV_HEADS, HEAD_DIM)
    gl = proj[..., nq + 6 * nkv:nq + 6 * nkv + 3 * NSA_HEADS].reshape(B, T, NSA_KV_HEADS, NSA_GROUP, 3)
    return q, gl, [kv[:, :, s] for s in range(6)]


def nsa_prompt(proj, cw, rel_bias):
    S = proj.shape[1]
    pe_k, w1_k, w2_k, pe_v, w1_v, w2_v = cw
    q, gl, (kc, vc, ks, vs, kw, vw) = nsa_split(proj)
    kcmp = compress(kc, pe_k, w1_k, w2_k)
    vcmp = compress(vc, pe_v, w1_v, w2_v)
    o = nsa_prompt_attention(q, gl, kcmp, vcmp, ks, vs, kw, vw, rel_bias)
    wb = min(WINDOW, S)
    return o, [kc, vc, ks, vs, kw[:, S - wb:], vw[:, S - wb:]]


def nsa_sample(proj, cw, rel_bias, ck_pool, cv_pool, sk_pool, sv_pool, wk_buf, wv_buf, page_table):
    DB, T, _ = proj.shape
    pe_k, w1_k, w2_k, pe_v, w1_v, w2_v = cw
    q, gl, (kc, vc, ks, vs, kw, vw) = nsa_split(proj)
    kcmp, vcmp = compress_paged(ck_pool, cv_pool, kc, vc, page_table, (pe_k, w1_k, w2_k), (pe_v, w1_v, w2_v))
    kw_all = jnp.concatenate([wk_buf, kw], axis=1)
    vw_all = jnp.concatenate([wv_buf, vw], axis=1)
    o = nsa_sample_attention(q, gl, kcmp, vcmp, (ks, vs, kw, vw), sk_pool, sv_pool, wk_buf, wv_buf,
                             page_table, rel_bias)
    return o, [kc, vc, ks, vs, kw_all[:, T:], vw_all[:, T:]]


def kernel(x_prompt, x_sample, p_prompt, p_sample, cache_fox_k, cache_fox_v, cache_fox_logf,
           cache_nsa_cmp_k, cache_nsa_cmp_v, cache_nsa_sel_k, cache_nsa_sel_v,
           cache_nsa_win_k, cache_nsa_win_v, page_table,
           norm_mix, norm_ffn, norm_ple, norm_final,
           fox_w_in, fox_b_f, fox_w_out, nsa_w_in, nsa_w_out,
           cmp_pe_k, cmp_w1_k, cmp_w2_k, cmp_pe_v, cmp_w1_v, cmp_w2_v, rel_bias,
           peer_wq, peer_sub_k1, peer_sub_k2, peer_u, peer_v, ple_w_proj, ple_w_gate):
    B, S, d = x_prompt.shape
    DB, T, _ = x_sample.shape
    n_p, n_s = B * S, DB * T
    bf = jnp.bfloat16
    rows = lambda a, b: jnp.concatenate([a.reshape(n_p, -1), b.reshape(n_s, -1)], axis=0)
    x = rows(x_prompt, x_sample)
    fox_p, fox_s, nsa_p, nsa_s = [], [], [], []
    for i in range(DEPTH):
        j = i // 2
        if i % 2 == 0:
            h = rmsnorm_pallas(x, norm_mix[i], bf)
            proj = linear_pallas(h, fox_w_in[j])
            nh = FOX_HEADS * HEAD_DIM
            logf = jax.nn.log_sigmoid(proj[:, 3 * nh:3 * nh + FOX_HEADS] + fox_b_f[j])
            heads = lambda a, lead: a.reshape(lead + (FOX_HEADS, -1))
            qp, kp, vp = (heads(proj[:n_p, s * nh:(s + 1) * nh], (B, S)) for s in range(3))
            qs, ks_, vs_ = (heads(proj[n_p:, s * nh:(s + 1) * nh], (DB, T)) for s in range(3))
            fp, fs = logf[:n_p].reshape(B, S, FOX_HEADS), logf[n_p:].reshape(DB, T, FOX_HEADS)
            op = fox_prompt_attention(qp, kp, vp, fp)
            os_ = fox_sample_attention(qs, ks_, vs_, fs, cache_fox_k[j], cache_fox_v[j], cache_fox_logf[j],
                                       page_table)
            fox_p.append([kp, vp, fp])
            fox_s.append([ks_, vs_, fs])
            y = linear_pallas(rows(op, os_), fox_w_out[j])
        else:
            cw = (cmp_pe_k[j], cmp_w1_k[j], cmp_w2_k[j], cmp_pe_v[j], cmp_w1_v[j], cmp_w2_v[j])
            h = rmsnorm_pallas(x, norm_mix[i], bf)
            proj = linear_pallas(h, nsa_w_in[j])
            op, stp = nsa_prompt(proj[:n_p].reshape(B, S, -1), cw, rel_bias)
            os_, sts = nsa_sample(proj[n_p:].reshape(DB, T, -1), cw, rel_bias,
                                  cache_nsa_cmp_k[j], cache_nsa_cmp_v[j], cache_nsa_sel_k[j],
                                  cache_nsa_sel_v[j], cache_nsa_win_k[j], cache_nsa_win_v[j], page_table)
            nsa_p.append(stp)
            nsa_s.append(sts)
            y = linear_pallas(rows(op, os_), nsa_w_out[j])
        x = x + y
        h = rmsnorm_pallas(x, norm_ffn[i], bf)
        x = x + peer_pallas(h, peer_wq[i].T.astype(bf), peer_sub_k1[i].astype(bf), peer_sub_k2[i].astype(bf),
                            peer_u[i].astype(bf), peer_v[i].astype(bf))
        x = ple_pallas(x, rows(p_prompt[i], p_sample[i]), norm_ple[i], ple_w_proj[i], ple_w_gate[i])
    y = rmsnorm_pallas(x, norm_final, jnp.float32)
    st = lambda lst, k: jnp.stack([s[k] for s in lst])
    return (y[:n_p].reshape(B, S, d), y[n_p:].reshape(DB, T, d),
            st(fox_p, 0), st(fox_p, 1), st(fox_p, 2),
            st(fox_s, 0), st(fox_s, 1), st(fox_s, 2),
            st(nsa_p, 0), st(nsa_p, 1), st(nsa_p, 2), st(nsa_p, 3), st(nsa_p, 4), st(nsa_p, 5),
            st(nsa_s, 0), st(nsa_s, 1), st(nsa_s, 2), st(nsa_s, 3), st(nsa_s, 4), st(nsa_s, 5))
```

```python
import functools
import math

import jax
import jax.numpy as jnp
from jax import lax
from jax.experimental import pallas as pl
from jax.experimental.pallas import tpu as pltpu

D_MODEL = 1024
DEPTH = 2
PAGE_SIZE = 128
HEAD_DIM = 64
FOX_HEADS = D_MODEL // HEAD_DIM
NSA_HEADS = D_MODEL // HEAD_DIM
NSA_KV_HEADS = 4
NSA_GROUP = NSA_HEADS // NSA_KV_HEADS
CMP_LEN = 32
CMP_STRIDE = 16
SEL_BLOCK = 64
SEL_TOPK = 16
WINDOW = 512
N_BUCKETS = 32
MAX_DISTANCE = 128
PEER_HEADS = 8
PEER_TOPK = 16
N_KEYS = 128
N_EXPERTS = N_KEYS * N_KEYS
PEER_DKEY = 256
FORCE_SCORE = 1e4
RMS_EPS = 1e-6
NEG_INF = -1e30
TINY = 1e-30

VMEM_LIMIT_BYTES = 56 * 1024 * 1024
LANES = 128
ROW_TILE = 512
MAX_COL_TILE = 1024


def _round_up(x, m):
    return -(-x // m) * m


def _split3(x):
    hi = x.astype(jnp.bfloat16)
    r1 = x - hi.astype(jnp.float32)
    mid = r1.astype(jnp.bfloat16)
    lo = (r1 - mid.astype(jnp.float32)).astype(jnp.bfloat16)
    return hi, mid, lo


def _split3_outside_kernel(x):
    to_bf16 = lambda a: lax.reduce_precision(a, exponent_bits=8, mantissa_bits=7)
    hi = to_bf16(x)
    r1 = x - hi
    mid = to_bf16(r1)
    lo = to_bf16(r1 - mid)
    return hi.astype(jnp.bfloat16), mid.astype(jnp.bfloat16), lo.astype(jnp.bfloat16)


def _rms(x, g):
    return x * lax.rsqrt(jnp.mean(x * x, axis=-1, keepdims=True) + RMS_EPS) * g


def _rmsnorm_body(x_ref, g_ref, o_ref):
    o_ref[...] = _rms(x_ref[...], g_ref[...]).astype(o_ref.dtype)


def rmsnorm_pallas(x, g, out_dtype):
    n, d = x.shape
    return pl.pallas_call(
        _rmsnorm_body,
        grid=(n // ROW_TILE,),
        in_specs=[pl.BlockSpec((ROW_TILE, d), lambda i: (i, 0)),
                  pl.BlockSpec((1, d), lambda i: (0, 0))],
        out_specs=pl.BlockSpec((ROW_TILE, d), lambda i: (i, 0)),
        out_shape=jax.ShapeDtypeStruct((n, d), out_dtype),
        compiler_params=pltpu.CompilerParams(dimension_semantics=("parallel",)),
        name="rmsnorm",
    )(x, g.reshape(1, d))


def _linear_body(x_ref, w_ref, *rest):
    o_ref = rest[-1]
    y = jnp.dot(x_ref[...].astype(jnp.bfloat16), w_ref[...], preferred_element_type=jnp.float32)
    o_ref[...] = y if len(rest) == 1 else rest[0][...] + y


def linear_pallas(x, w, residual=None):
    n, k = x.shape
    m = w.shape[1]
    mp = _round_up(m, LANES)
    tn = max(t for t in range(LANES, MAX_COL_TILE + 1, LANES) if mp % t == 0)
    wb = jnp.pad(w.astype(jnp.bfloat16), ((0, 0), (0, mp - m)))
    extra = () if residual is None else (residual,)
    out_spec = pl.BlockSpec((ROW_TILE, tn), lambda i, j: (i, j))
    return pl.pallas_call(
        _linear_body,
        grid=(n // ROW_TILE, mp // tn),
        in_specs=[pl.BlockSpec((ROW_TILE, k), lambda i, j: (i, 0)),
                  pl.BlockSpec((k, tn), lambda i, j: (0, j))] + [out_spec] * len(extra),
        out_specs=out_spec,
        out_shape=jax.ShapeDtypeStruct((n, mp), jnp.float32),
        compiler_params=pltpu.CompilerParams(dimension_semantics=("parallel", "parallel"),
                                             vmem_limit_bytes=VMEM_LIMIT_BYTES),
        name="linear",
    )(x, wb, *extra)


def _ple_body(x_ref, p_ref, g_ref, wg_ref, wp_ref, o_ref):
    x = x_ref[...]
    h = _rms(x, g_ref[...]).astype(jnp.bfloat16)
    gate = 1.0 / (1.0 + jnp.exp(-jnp.dot(h, wg_ref[...], preferred_element_type=jnp.float32)))
    proj = jnp.dot(p_ref[...].astype(jnp.bfloat16), wp_ref[...], preferred_element_type=jnp.float32)
    o_ref[...] = x + gate * proj


def ple_pallas(x, p, g, w_proj, w_gate):
    n, d = x.shape
    dp = p.shape[1]
    return pl.pallas_call(
        _ple_body,
        grid=(n // ROW_TILE,),
        in_specs=[pl.BlockSpec((ROW_TILE, d), lambda i: (i, 0)),
                  pl.BlockSpec((ROW_TILE, dp), lambda i: (i, 0)),
                  pl.BlockSpec((1, d), lambda i: (0, 0)),
                  pl.BlockSpec((d, d), lambda i: (0, 0)),
                  pl.BlockSpec((dp, d), lambda i: (0, 0))],
        out_specs=pl.BlockSpec((ROW_TILE, d), lambda i: (i, 0)),
        out_shape=jax.ShapeDtypeStruct((n, d), jnp.float32),
        compiler_params=pltpu.CompilerParams(dimension_semantics=("parallel",),
                                             vmem_limit_bytes=VMEM_LIMIT_BYTES),
        name="ple",
    )(x, p, g.reshape(1, d), w_gate.astype(jnp.bfloat16), w_proj.astype(jnp.bfloat16))


PEER_TOKEN_TILE = 512
PEER_EXPERT_TILE = 1024
SQRT_HALF = 0.7071067811865476


def _gelu_exact(x):
    return 0.5 * x * (1.0 + lax.erf(x * SQRT_HALF))


def _top_rows(x, k):
    vals = []
    for _ in range(k):
        m = jnp.max(x, axis=0, keepdims=True)
        vals.append(m)
        x = jnp.where(x == m, NEG_INF, x)
    return vals


def _peer_route_body(h_ref, wqT_ref, k1_ref, k2_ref, n1_ref, c1_ref, r2_ref, e2_ref):
    h = h_ref[...]
    half = PEER_DKEY // 2
    nt = (((1,), (1,)), ((), ()))
    for hd in range(PEER_HEADS):
        qv = lax.dot_general(wqT_ref[hd * PEER_DKEY:(hd + 1) * PEER_DKEY, :], h, nt,
                             preferred_element_type=jnp.float32)
        s1 = jnp.dot(k1_ref[hd], qv[:half].astype(jnp.bfloat16), preferred_element_type=jnp.float32)
        s2 = jnp.dot(k2_ref[hd], qv[half:].astype(jnp.bfloat16), preferred_element_type=jnp.float32)
        v1 = _top_rows(s1, PEER_TOPK)
        v2 = _top_rows(s2, PEER_TOPK)
        v2_stack = jnp.concatenate(v2, axis=0)
        blocks = []
        for p in range(PEER_TOPK):
            n_p = PEER_TOPK // (p + 1)
            rows = -(-n_p // 8) * 8
            blk = v1[p] + v2_stack[:rows]
            if n_p < rows:
                r = lax.broadcasted_iota(jnp.int32, blk.shape, 0)
                blk = jnp.where(r < n_p, blk, NEG_INF)
            blocks.append(blk)
        c = _top_rows(jnp.concatenate(blocks, axis=0), PEER_TOPK)
        z = jnp.ones_like(c[0])
        for kk in range(1, PEER_TOPK):
            z = z + jnp.exp(c[kk] - c[0])
        tau = c[PEER_TOPK - 1]
        n1 = jnp.zeros_like(s1)
        r2 = jnp.zeros_like(s2)
        for q in range(PEER_TOPK):
            n1 = n1 + jnp.where(s1 + v2[q] >= tau, 1.0, 0.0)
            r2 = r2 + jnp.where(v2[q] > s2, 1.0, 0.0)
        n1_ref[hd] = n1
        c1_ref[hd] = jnp.exp(s1 - v1[0]) / z
        r2_ref[hd] = r2.astype(jnp.bfloat16)
        e2_ref[hd] = jnp.exp(s2 - v2[0]).astype(jnp.bfloat16)


def _peer_dense_body(x_ref, h_ref, u_ref, v_ref, n1_ref, c1_ref, r2_ref, e2_ref, y_ref):
    j = pl.program_id(1)

    @pl.when(j == 0)
    def _():
        y_ref[...] = x_ref[...]

    h = h_ref[...]
    act = lax.dot_general(u_ref[...], h, (((1,), (1,)), ((), ())),
                          preferred_element_type=jnp.float32)
    groups = PEER_EXPERT_TILE // N_KEYS
    parts = []
    for aa in range(groups):
        w = None
        for hd in range(PEER_HEADS):
            picked = r2_ref[hd] < n1_ref[hd, aa:aa + 1, :].astype(jnp.bfloat16)
            term = jnp.where(picked, e2_ref[hd], 0.0) * c1_ref[hd, aa:aa + 1, :].astype(jnp.bfloat16)
            w = term if w is None else w + term
        g = _gelu_exact(act[aa * N_KEYS:(aa + 1) * N_KEYS]).astype(jnp.bfloat16)
        parts.append(w * g)
    p = jnp.concatenate(parts, axis=0)
    y_ref[...] += lax.dot_general(p, v_ref[...], (((0,), (0,)), ((), ())),
                                  preferred_element_type=jnp.float32)


def peer_pallas(x, h, wqT, k1, k2, u, v):
    n, d = h.shape
    t = PEER_TOKEN_TILE
    nt = n // t
    hk = (PEER_HEADS, N_KEYS)
    route = lambda dt: jax.ShapeDtypeStruct(hk + (n,), dt)
    route_spec = pl.BlockSpec(hk + (t,), lambda i: (0, 0, i))
    n1, c1, r2, e2 = pl.pallas_call(
        _peer_route_body,
        grid=(nt,),
        in_specs=[pl.BlockSpec((t, d), lambda i: (i, 0)),
                  pl.BlockSpec(wqT.shape, lambda i: (0, 0)),
                  pl.BlockSpec(k1.shape, lambda i: (0, 0, 0)),
                  pl.BlockSpec(k2.shape, lambda i: (0, 0, 0))],
        out_specs=[route_spec] * 4,
        out_shape=[route(jnp.float32), route(jnp.float32), route(jnp.bfloat16), route(jnp.bfloat16)],
        compiler_params=pltpu.CompilerParams(dimension_semantics=("parallel",),
                                             vmem_limit_bytes=VMEM_LIMIT_BYTES),
        name="peer_route",
    )(h, wqT, k1, k2)

    e = PEER_EXPERT_TILE
    groups = e // N_KEYS
    row_spec = pl.BlockSpec((PEER_HEADS, groups, t), lambda i, j: (0, j, i))
    full_spec = pl.BlockSpec(hk + (t,), lambda i, j: (0, 0, i))
    return pl.pallas_call(
        _peer_dense_body,
        grid=(nt, N_EXPERTS // e),
        in_specs=[pl.BlockSpec((t, d), lambda i, j: (i, 0)),
                  pl.BlockSpec((t, d), lambda i, j: (i, 0)),
                  pl.BlockSpec((e, d), lambda i, j: (j, 0)),
                  pl.BlockSpec((e, d), lambda i, j: (j, 0)),
                  row_spec, row_spec, full_spec, full_spec],
        out_specs=pl.BlockSpec((t, d), lambda i, j: (i, 0)),
        out_shape=jax.ShapeDtypeStruct((n, d), jnp.float32),
        compiler_params=pltpu.CompilerParams(dimension_semantics=("parallel", "arbitrary"),
                                             vmem_limit_bytes=VMEM_LIMIT_BYTES),
        name="peer_dense",
    )(x, h, u, v, n1, c1, r2, e2)


ATT_TILE = 128
MASKED_BELOW = -0.5e30
KEY_PAIR = 4


def t5_bucket(dist):
    n = jnp.maximum(dist, 0)
    max_exact = N_BUCKETS // 2
    nf = jnp.maximum(n, max_exact).astype(jnp.float32)
    large = max_exact + (jnp.log(nf / max_exact) / math.log(MAX_DISTANCE / max_exact)
                         * (N_BUCKETS - max_exact)).astype(jnp.int32)
    large = jnp.minimum(large, N_BUCKETS - 1)
    return jnp.where(n < max_exact, n, large)


def _softmax_cols(s):
    m = jnp.max(s, axis=0, keepdims=True)
    e = jnp.where(s > MASKED_BELOW, jnp.exp(s - m), 0.0)
    l = jnp.sum(e, axis=0, keepdims=True)
    return e / jnp.maximum(l, TINY)


def _dot_f32_by_01(mat01, x):
    return sum(jnp.dot(mat01, part, preferred_element_type=jnp.float32) for part in _split3(x))


def _x_dot_01(x, mat01):
    return sum(jnp.dot(part, mat01, preferred_element_type=jnp.float32) for part in _split3(x))


def _online_tiles(k_ref, v_ref, qT, lo, hi, bias_of, mask_of, tile):
    width = qT.shape[1]
    tn = (((0,), (0,)), ((), ()))

    def step(j, carry):
        m, l, acc = carry
        rows = pl.ds(pl.multiple_of(j * tile, tile), tile)
        s = jnp.dot(k_ref[rows, :], qT, preferred_element_type=jnp.float32) + bias_of(j)
        s = mask_of(j, s)
        m_new = jnp.maximum(m, jnp.max(s, axis=0, keepdims=True))
        alpha = jnp.exp(m - m_new)
        e = jnp.where(s > MASKED_BELOW, jnp.exp(s - m_new), 0.0)
        l = alpha * l + jnp.sum(e, axis=0, keepdims=True)
        pv = lax.dot_general(v_ref[rows, :], e.astype(jnp.bfloat16), tn,
                             preferred_element_type=jnp.float32)
        return m_new, l, alpha * acc + pv

    init = (jnp.full((1, width), NEG_INF, jnp.float32), jnp.zeros((1, width), jnp.float32),
            jnp.zeros((HEAD_DIM, width), jnp.float32))
    m, l, acc = lax.fori_loop(lo, hi, step, init)
    return acc / jnp.maximum(l, TINY)


def _nsa_core(qT, gate_logits, q_pos, kc_ref, vc_ref, ks_ref, vs_ref, kw_ref, vw_ref, cmp_bias,
              sel_range, sel_bias, win_range, win_bias, *, tq, n_sel, n_blocks, sel_tile, win_tile):
    tn = (((0,), (0,)), ((), ()))
    p_cmp = _softmax_cols(jnp.dot(kc_ref[...], qT, preferred_element_type=jnp.float32) + cmp_bias)
    o_cmp = lax.dot_general(vc_ref[...], p_cmp.astype(jnp.bfloat16), tn, preferred_element_type=jnp.float32)
    ncp = p_cmp.shape[0]
    nb = lax.broadcasted_iota(jnp.int32, (n_blocks, ncp), 0) * SEL_BLOCK
    cs = lax.broadcasted_iota(jnp.int32, (n_blocks, ncp), 1) * CMP_STRIDE
    overlap = jnp.where((cs < nb + SEL_BLOCK) & (cs + CMP_LEN > nb), 1.0, 0.0).astype(jnp.bfloat16)
    per_head = _dot_f32_by_01(overlap, p_cmp)
    imp = sum(per_head[:, r * tq:(r + 1) * tq] for r in range(NSA_GROUP))
    blk = lax.broadcasted_iota(jnp.int32, (n_blocks, tq), 0)
    cur = q_pos // SEL_BLOCK
    forced = (blk == 0) | (blk == cur) | (blk == cur - 1)
    score = jnp.where(blk <= cur, jnp.where(forced, FORCE_SCORE, imp), -1.0)
    rank = jnp.zeros((n_blocks, tq), jnp.float32)
    for mrow in range(n_blocks):
        row = score[mrow:mrow + 1, :]
        ahead = (row > score) | ((row == score) & (blk > mrow))
        rank = rank + jnp.where(ahead, 1.0, 0.0)
    sel = jnp.where(rank < n_sel, 1.0, 0.0).astype(jnp.bfloat16)
    sel = jnp.concatenate([sel] * NSA_GROUP, axis=1)

    def sel_mask(j, s):
        kb = (j * sel_tile + lax.broadcasted_iota(jnp.int32, (sel_tile, n_blocks), 0)) // SEL_BLOCK
        nn = lax.broadcasted_iota(jnp.int32, (sel_tile, n_blocks), 1)
        expand = jnp.where(kb == nn, 1.0, 0.0).astype(jnp.bfloat16)
        mk = jnp.dot(expand, sel, preferred_element_type=jnp.float32)
        return jnp.where(mk > 0.5, s, NEG_INF)

    o_sel = _online_tiles(ks_ref, vs_ref, qT, sel_range[0], sel_range[1], sel_bias, sel_mask, sel_tile)
    o_win = _online_tiles(kw_ref, vw_ref, qT, win_range[0], win_range[1], win_bias, lambda j, s: s, win_tile)
    g = 1.0 / (1.0 + jnp.exp(-gate_logits))
    return g[0:1] * o_cmp + g[1:2] * o_sel + g[2:3] * o_win


def _nsa_prompt_body(qT_ref, gate_ref, kc_ref, vc_ref, ks_ref, vs_ref, kw_ref, vw_ref, cb_ref, tz_ref,
                     o_ref, *, n_sel):
    i = pl.program_id(2)
    tq = ATT_TILE
    q_pos = i * tq + lax.broadcasted_iota(jnp.int32, (1, tq), 1)
    masked_tile = tz_ref.shape[0] - 1

    def pair_bias(max_delta):
        def bias(jj):
            tiles = []
            for j in (KEY_PAIR * jj + t for t in range(KEY_PAIR)):
                idx = jnp.where(j > i, masked_tile, jnp.minimum(i - j, max_delta))
                tiles.append(tz_ref[idx])
            return jnp.concatenate(tiles, axis=0)
        return bias

    first_win = jnp.maximum(i - WINDOW // ATT_TILE, 0)
    o_ref[...] = _nsa_core(
        qT_ref[...], gate_ref[...], q_pos, kc_ref, vc_ref, ks_ref, vs_ref, kw_ref, vw_ref, cb_ref[...],
        (0, i // KEY_PAIR + 1), pair_bias(2),
        (first_win // KEY_PAIR, i // KEY_PAIR + 1), pair_bias(masked_tile),
        tq=tq, n_sel=n_sel, n_blocks=ks_ref.shape[0] // SEL_BLOCK,
        sel_tile=KEY_PAIR * ATT_TILE, win_tile=KEY_PAIR * ATT_TILE)


def _bias_lanes(tbl, dist, ok):
    onehot = (t5_bucket(dist)[..., None] == jnp.arange(N_BUCKETS)).astype(jnp.float32)
    vals = jnp.dot(onehot.reshape(-1, N_BUCKETS), tbl.reshape(N_BUCKETS, -1),
                   precision=lax.Precision.HIGHEST).reshape(dist.shape + tbl.shape[1:])
    b = jnp.where(ok[..., None, None], vals, NEG_INF)
    nd = b.ndim
    b = jnp.moveaxis(b, (nd - 2, nd - 1), (0, nd - 2))
    return b.reshape(b.shape[:-2] + (b.shape[-2] * b.shape[-1],))


def nsa_prompt_attention(q, gate_logits, kcmp, vcmp, ks, vs, kw, vw, rel_bias):
    B, S, G, R, dh = q.shape
    t = ATT_TILE
    n_qt = S // t
    bf = jnp.bfloat16
    tbl = rel_bias.reshape(N_BUCKETS, G, R).astype(jnp.float32)
    qT = (q * (HEAD_DIM ** -0.5)).astype(bf).reshape(B, n_qt, t, G, R, dh)
    qT = qT.transpose(0, 3, 1, 5, 4, 2).reshape(B, G, n_qt, dh, R * t)
    gT = gate_logits.reshape(B, n_qt, t, G, R, 3).transpose(0, 3, 1, 5, 4, 2).reshape(B, G, n_qt, 3, R * t)
    n_cmp = kcmp.shape[1]
    ncp = _round_up(n_cmp, 8)
    padc = lambda x: jnp.pad(x.astype(bf).transpose(0, 2, 1, 3), ((0, 0), (0, 0), (0, ncp - n_cmp), (0, 0)))
    tr = lambda x: x.astype(bf).transpose(0, 2, 1, 3)
    qp = jnp.arange(S).reshape(n_qt, 1, t)
    c_idx = jnp.arange(ncp).reshape(1, ncp, 1)
    dist = qp - (c_idx * CMP_STRIDE + CMP_LEN - 1)
    cb = _bias_lanes(tbl, dist, (dist >= 0) & (c_idx < n_cmp))
    n_delta = WINDOW // t + 2
    d = (jnp.arange(n_delta).reshape(-1, 1, 1) * t + jnp.arange(t).reshape(1, 1, t)
         - jnp.arange(t).reshape(1, t, 1))
    tz = _bias_lanes(tbl, d, (d >= 0) & (d < WINDOW))
    body = functools.partial(_nsa_prompt_body, n_sel=min(SEL_TOPK, S // SEL_BLOCK))
    kv_spec = pl.BlockSpec((None, None, S, dh), lambda b, g, i: (b, g, 0, 0))
    cmp_spec = pl.BlockSpec((None, None, ncp, dh), lambda b, g, i: (b, g, 0, 0))
    oT = pl.pallas_call(
        body,
        grid=(B, G, n_qt),
        in_specs=[pl.BlockSpec((None, None, None, dh, R * t), lambda b, g, i: (b, g, i, 0, 0)),
                  pl.BlockSpec((None, None, None, 3, R * t), lambda b, g, i: (b, g, i, 0, 0)),
                  cmp_spec, cmp_spec, kv_spec, kv_spec, kv_spec, kv_spec,
                  pl.BlockSpec((None, None, ncp, R * t), lambda b, g, i: (g, i, 0, 0)),
                  pl.BlockSpec((None,) + tz.shape[1:], lambda b, g, i: (g, 0, 0, 0))],
        out_specs=pl.BlockSpec((None, None, None, dh, R * t), lambda b, g, i: (b, g, i, 0, 0)),
        out_shape=jax.ShapeDtypeStruct((B, G, n_qt, dh, R * t), jnp.float32),
        compiler_params=pltpu.CompilerParams(dimension_semantics=("parallel", "parallel", "arbitrary"),
                                             vmem_limit_bytes=VMEM_LIMIT_BYTES),
        name="nsa_prompt_attention",
    )(qT, gT, padc(kcmp), padc(vcmp), tr(ks), tr(vs), tr(kw), tr(vw), cb, tz)
    o = oT.reshape(B, G, n_qt, dh, R, t).transpose(0, 2, 5, 1, 4, 3)
    return o.reshape(B, S, G * R * dh)


SAMPLE_PAGES_PER_STEP = 8


def _softmax_lanes(s):
    m = jnp.max(s, axis=1, keepdims=True)
    e = jnp.where(s > MASKED_BELOW, jnp.exp(s - m), 0.0)
    return e, jnp.sum(e, axis=1, keepdims=True)


def _attend_lanes(qbd, k2, v2, bias, mask=None):
    s = jnp.dot(qbd, k2, preferred_element_type=jnp.float32) + bias
    if mask is not None:
        s = jnp.where(mask > 0.5, s, NEG_INF)
    e, l = _softmax_lanes(s)
    o = lax.dot_general(e.astype(jnp.bfloat16), v2, (((1,), (1,)), ((), ())),
                        preferred_element_type=jnp.float32)
    return o / jnp.maximum(l, TINY)


def _nsa_sample_body(pt_ref, qbd_ref, gate_ref, kc_ref, vc_ref, new_ref, wk_ref, wv_ref, cb_ref, sb_ref, wb_ref,
                     *rest, n_sel, tq, past, n_blocks):
    pps = SAMPLE_PAGES_PER_STEP
    k_pages, v_pages = rest[:pps], rest[pps:2 * pps]
    o_ref, ks_ref, vs_ref = rest[2 * pps:]
    step = pl.program_id(1)
    rows = ks_ref.shape[0]
    width = qbd_ref.shape[0]
    for i in range(pps):
        cols = pl.ds(pl.multiple_of((step * pps + i) * PAGE_SIZE, PAGE_SIZE), PAGE_SIZE)
        ks_ref[:, cols] = k_pages[i][...].reshape(rows, PAGE_SIZE).astype(jnp.bfloat16)
        vs_ref[:, cols] = v_pages[i][...].reshape(rows, PAGE_SIZE).astype(jnp.bfloat16)

    @pl.when(step == pl.num_programs(1) - 1)
    def _():
        qbd = qbd_ref[...]
        n_keys = ks_ref.shape[1]
        ks_ref[:, past:n_keys] = new_ref[0]
        vs_ref[:, past:n_keys] = new_ref[1]
        s_c = lax.dot_general(qbd, kc_ref[...], (((1,), (1,)), ((), ())),
                              preferred_element_type=jnp.float32) + cb_ref[...]
        e_c, l_c = _softmax_lanes(s_c)
        p_cmp = e_c / jnp.maximum(l_c, TINY)
        o_cmp = jnp.dot(p_cmp.astype(jnp.bfloat16), vc_ref[...], preferred_element_type=jnp.float32)
        ncp = p_cmp.shape[1]
        cs = lax.broadcasted_iota(jnp.int32, (ncp, LANES), 0) * CMP_STRIDE
        nb = lax.broadcasted_iota(jnp.int32, (ncp, LANES), 1) * SEL_BLOCK
        overlap_t = jnp.where((cs < nb + SEL_BLOCK) & (cs + CMP_LEN > nb), 1.0, 0.0).astype(jnp.bfloat16)
        ra = lax.broadcasted_iota(jnp.int32, (width, width), 0)
        rb = lax.broadcasted_iota(jnp.int32, (width, width), 1)
        per_group = NSA_GROUP * tq
        same_query = jnp.where((ra // per_group == rb // per_group) & (ra % tq == rb % tq), 1.0, 0.0)
        imp = _dot_f32_by_01(same_query.astype(jnp.bfloat16), _x_dot_01(p_cmp, overlap_t))
        blk = lax.broadcasted_iota(jnp.int32, (width, LANES), 1)
        cur = (past + lax.broadcasted_iota(jnp.int32, (width, LANES), 0) % tq) // SEL_BLOCK
        forced = (blk == 0) | (blk == cur) | (blk == cur - 1)
        score = jnp.where(blk <= cur, jnp.where(forced, FORCE_SCORE, imp), -1.0)
        score = jnp.where(blk < n_blocks, score, -2.0)
        score_t = score.T
        bt = lax.broadcasted_iota(jnp.int32, (LANES, width), 0)
        rank = jnp.zeros((LANES, width), jnp.float32)
        for mrow in range(n_blocks):
            row = score_t[mrow:mrow + 1, :]
            ahead = (row > score_t) | ((row == score_t) & (bt > mrow))
            rank = rank + jnp.where(ahead, 1.0, 0.0)
        sel = jnp.where(rank < n_sel, 1.0, 0.0).T.astype(jnp.bfloat16)
        kb = lax.broadcasted_iota(jnp.int32, (LANES, n_keys), 1) // SEL_BLOCK
        nn = lax.broadcasted_iota(jnp.int32, (LANES, n_keys), 0)
        expand = jnp.where(kb == nn, 1.0, 0.0).astype(jnp.bfloat16)
        mask = jnp.dot(sel, expand, preferred_element_type=jnp.float32)
        o_sel = _attend_lanes(qbd, ks_ref[...], vs_ref[...], sb_ref[...], mask)
        window = lambda ref, new: jnp.concatenate(
            [ref[...].reshape(rows, ref.shape[-1]).astype(jnp.bfloat16), new], axis=1)
        o_win = _attend_lanes(qbd, window(wk_ref, new_ref[2]), window(wv_ref, new_ref[3]), wb_ref[...])
        g = 1.0 / (1.0 + jnp.exp(-gate_ref[...]))
        o = g[:, 0:1] * o_cmp + g[:, 1:2] * o_sel + g[:, 2:3] * o_win
        r = lax.broadcasted_iota(jnp.int32, (width, rows), 0) // per_group
        c = lax.broadcasted_iota(jnp.int32, (width, rows), 1) // HEAD_DIM
        o = jnp.where(r == c, o, 0.0)
        fa = lax.broadcasted_iota(jnp.int32, (rows, LANES), 0) % HEAD_DIM
        fb = lax.broadcasted_iota(jnp.int32, (rows, LANES), 1)
        fold = jnp.where(fa == fb, 1.0, 0.0).astype(jnp.bfloat16)
        o_ref[...] = _x_dot_01(o, fold)[:, :HEAD_DIM]


def _rows_table(table):
    g, k, w = table.shape
    return table.transpose(0, 2, 1).reshape(g * w, k)


def nsa_sample_attention(q, gate_logits, kcmp, vcmp, new_kv, sk_pool, sv_pool, wk_buf, wv_buf, page_table, rel_bias):
    DB, T, G, R, dh = q.shape
    bf = jnp.bfloat16
    width = G * R * T
    rows = G * dh
    n_pages = page_table.shape[1]
    past = n_pages * PAGE_SIZE
    n_keys = past + PAGE_SIZE
    n_blocks = _round_up(past + T, SEL_BLOCK) // SEL_BLOCK
    tbl = rel_bias.reshape(N_BUCKETS, G, R).astype(jnp.float32)
    q_pos = past + jnp.arange(T)
    qg = (q * (HEAD_DIM ** -0.5)).transpose(0, 2, 3, 1, 4)
    same_group = jnp.eye(G, dtype=qg.dtype).reshape(1, G, 1, 1, G, 1)
    qbd = (qg[:, :, :, :, None, :] * same_group).astype(bf).reshape(DB, width, rows)
    gates = jnp.pad(gate_logits.transpose(0, 2, 3, 1, 4).reshape(DB, width, 3), ((0, 0), (0, 0), (0, 5)))
    n_cmp = n_blocks * SEL_BLOCK // CMP_STRIDE - CMP_LEN // CMP_STRIDE + 1
    ncp = kcmp.shape[1]
    c_idx = jnp.arange(ncp).reshape(ncp, 1)
    dist = q_pos.reshape(1, T) - (c_idx * CMP_STRIDE + CMP_LEN - 1)
    cb = _rows_table(_bias_lanes(tbl, dist, (dist >= 0) & (c_idx < n_cmp)))
    k_idx = jnp.arange(n_keys).reshape(-1, 1)
    dist = q_pos.reshape(1, T) - k_idx
    sb = _rows_table(_bias_lanes(tbl, dist, dist >= 0))
    wbuf = wk_buf.shape[1]
    n_win = wbuf + PAGE_SIZE
    w_idx = jnp.arange(n_win).reshape(-1, 1)
    w_pos = past - wbuf + w_idx
    dist = q_pos.reshape(1, T) - w_pos
    wb = _rows_table(_bias_lanes(tbl, dist, (dist >= 0) & (dist < WINDOW) & (w_pos >= 0) & (w_idx < wbuf + T)))
    new_t = lambda x: jnp.pad(x.astype(bf).transpose(0, 2, 3, 1),
                              ((0, 0), (0, 0), (0, 0), (0, PAGE_SIZE - T))).reshape(DB, rows, PAGE_SIZE)
    new = jnp.stack([new_t(x) for x in new_kv], axis=1)
    native = lambda x: x.transpose(0, 2, 3, 1)
    pps = SAMPLE_PAGES_PER_STEP
    body = functools.partial(_nsa_sample_body, n_sel=min(SEL_TOPK, n_blocks), tq=T, past=past, n_blocks=n_blocks)
    per_b = lambda shape: pl.BlockSpec((None,) + shape, lambda b, s, pt: (b,) + (0,) * len(shape))
    const = lambda x: pl.BlockSpec(x.shape, lambda b, s, pt: (0,) * x.ndim)
    page = lambda i: pl.BlockSpec((None, G, dh, PAGE_SIZE), lambda b, s, pt: (pt[b, s * pps + i], 0, 0, 0))
    sk_t, sv_t = native(sk_pool), native(sv_pool)
    o = pl.pallas_call(
        body,
        grid_spec=pltpu.PrefetchScalarGridSpec(
            num_scalar_prefetch=1,
            grid=(DB, n_pages // pps),
            in_specs=[per_b((width, rows)), per_b((width, 8)), per_b((ncp, rows)), per_b((ncp, rows)),
                      per_b((4, rows, PAGE_SIZE)), per_b((G, dh, wbuf)), per_b((G, dh, wbuf)),
                      const(cb), const(sb), const(wb)]
                     + [page(i) for i in range(pps)] * 2,
            out_specs=per_b((width, dh)),
            scratch_shapes=[pltpu.VMEM((rows, n_keys), bf), pltpu.VMEM((rows, n_keys), bf)]),
        out_shape=jax.ShapeDtypeStruct((DB, width, dh), jnp.float32),
        compiler_params=pltpu.CompilerParams(dimension_semantics=("parallel", "arbitrary"),
                                             vmem_limit_bytes=VMEM_LIMIT_BYTES),
        name="nsa_sample_attention",
    )(page_table, qbd, gates, kcmp, vcmp, new, native(wk_buf), native(wv_buf), cb, sb, wb,
      *([sk_t] * pps), *([sv_t] * pps))
    return o.reshape(DB, G, R, T, dh).transpose(0, 3, 1, 2, 4).reshape(DB, T, G * R * dh)


FOX_Q_TILE = 512
FOX_K_TILE = 256


def _softmax_step(s, v, carry, masked):
    m, l, acc = carry
    m_new = jnp.maximum(m, jnp.max(s, axis=0, keepdims=True))
    alpha = jnp.exp(m - m_new)
    e = jnp.exp(s - m_new)
    if masked:
        e = jnp.where(s > MASKED_BELOW, e, 0.0)
    l = alpha * l + jnp.sum(e, axis=0, keepdims=True)
    pv = lax.dot_general(v, e.astype(jnp.bfloat16), (((0,), (0,)), ((), ())),
                         preferred_element_type=jnp.float32)
    return m_new, l, alpha * acc + pv


def _fox_prompt_body(qT_ref, qd_ref, k_ref, kd_ref, v_ref, o_ref):
    i = pl.program_id(2)
    qT = qT_ref[...]
    q_decay = qd_ref[...]
    width = qT.shape[1]

    def tile(j):
        rows = pl.ds(pl.multiple_of(j * FOX_K_TILE, FOX_K_TILE), FOX_K_TILE)
        s = jnp.dot(k_ref[rows, :], qT, preferred_element_type=jnp.float32)
        s = s + lax.dot_general(kd_ref[:, rows], q_decay, (((0,), (0,)), ((), ())),
                                preferred_element_type=jnp.float32)
        return s, v_ref[rows, :]

    def full_step(j, carry):
        s, v = tile(j)
        return _softmax_step(s, v, carry, masked=False)

    carry = (jnp.full((1, width), NEG_INF, jnp.float32), jnp.zeros((1, width), jnp.float32),
             jnp.zeros((HEAD_DIM, width), jnp.float32))
    ratio = FOX_Q_TILE // FOX_K_TILE
    carry = lax.fori_loop(0, i * ratio, full_step, carry)
    q_pos = i * FOX_Q_TILE + lax.broadcasted_iota(jnp.int32, (FOX_K_TILE, width), 1)
    for dj in range(ratio):
        j = i * ratio + dj
        s, v = tile(j)
        k_pos = j * FOX_K_TILE + lax.broadcasted_iota(jnp.int32, (FOX_K_TILE, width), 0)
        s = jnp.where(k_pos <= q_pos, s, NEG_INF)
        carry = _softmax_step(s, v, carry, masked=True)
    m, l, acc = carry
    o_ref[...] = acc / jnp.maximum(l, TINY)


DECAY_COLS = 8


def fox_prompt_attention(q, k, v, logf):
    B, S, H, dh = q.shape
    bf = jnp.bfloat16
    c = jnp.cumsum(logf, axis=1).transpose(0, 2, 1)
    hi, mid, lo = _split3_outside_kernel(c)
    one, zero = jnp.ones_like(hi), jnp.zeros_like(hi)
    k_decay = jnp.stack([-hi, -mid, -lo, one, one, one, zero, zero], axis=2)
    q_decay = jnp.stack([one, one, one, hi, mid, lo, zero, zero], axis=2)
    kb = k.astype(bf).transpose(0, 2, 1, 3)
    qT = (q * (dh ** -0.5)).astype(bf).transpose(0, 2, 3, 1)
    vb = v.astype(bf).transpose(0, 2, 1, 3)
    per_head = lambda rows: pl.BlockSpec((None, None, S, rows), lambda b, h, i: (b, h, 0, 0))
    per_tile = lambda rows: pl.BlockSpec((None, None, rows, FOX_Q_TILE), lambda b, h, i: (b, h, 0, i))
    oT = pl.pallas_call(
        _fox_prompt_body,
        grid=(B, H, S // FOX_Q_TILE),
        in_specs=[per_tile(dh), per_tile(DECAY_COLS), per_head(dh),
                  pl.BlockSpec((None, None, DECAY_COLS, S), lambda b, h, i: (b, h, 0, 0)), per_head(dh)],
        out_specs=per_tile(dh),
        out_shape=jax.ShapeDtypeStruct((B, H, dh, S), jnp.float32),
        compiler_params=pltpu.CompilerParams(dimension_semantics=("parallel", "parallel", "arbitrary"),
                                             vmem_limit_bytes=VMEM_LIMIT_BYTES),
        name="fox_prompt_attention",
    )(qT, q_decay, kb, k_decay, vb)
    return oT.transpose(0, 3, 1, 2).reshape(B, S, H * dh)


FOX_PAGES_PER_STEP = 4


def _fox_sample_body(pt_ref, qbd_ref, bp_ref, kn_ref, vn_ref, bn_ref, *rest, n_q):
    pps = FOX_PAGES_PER_STEP
    k_pages, v_pages = rest[:pps], rest[pps:2 * pps]
    o_ref, m_ref, l_ref, acc_ref = rest[2 * pps:]
    p = pl.program_id(1)
    width = qbd_ref.shape[0]
    rows = kn_ref.shape[0]

    @pl.when(p == 0)
    def _():
        m_ref[...] = jnp.full(m_ref.shape, NEG_INF, jnp.float32)
        l_ref[...] = jnp.zeros(l_ref.shape, jnp.float32)
        acc_ref[...] = jnp.zeros(acc_ref.shape, jnp.float32)

    def attend(k2, v2, bias_t):
        s_t = jnp.dot(qbd_ref[...], k2, preferred_element_type=jnp.float32).T + bias_t
        m = m_ref[...]
        m_new = jnp.maximum(m, jnp.max(s_t, axis=0, keepdims=True))
        alpha = jnp.exp(m - m_new)
        e = jnp.where(s_t > MASKED_BELOW, jnp.exp(s_t - m_new), 0.0)
        m_ref[...] = m_new
        l_ref[...] = alpha * l_ref[...] + jnp.sum(e, axis=0, keepdims=True)
        acc_ref[...] = alpha * acc_ref[...] + jnp.dot(v2, e.astype(jnp.bfloat16),
                                                      preferred_element_type=jnp.float32)

    pages = lambda refs: jnp.concatenate(
        [ref[...].reshape(rows, PAGE_SIZE).astype(jnp.bfloat16) for ref in refs], axis=1)
    attend(pages(k_pages), pages(v_pages), bp_ref[...].reshape(pps * PAGE_SIZE, width))

    @pl.when(p == pl.num_programs(1) - 1)
    def _():
        attend(kn_ref[...], vn_ref[...], bn_ref[...])
        out = acc_ref[...] / jnp.maximum(l_ref[...], TINY)
        r = lax.broadcasted_iota(jnp.int32, (rows, width), 0) // HEAD_DIM
        c = lax.broadcasted_iota(jnp.int32, (rows, width), 1) // n_q
        out = jnp.where(r == c, out, 0.0)
        la = lax.broadcasted_iota(jnp.int32, (width, width), 0) % n_q
        lb = lax.broadcasted_iota(jnp.int32, (width, width), 1)
        gather_q = jnp.where(la == lb, 1.0, 0.0).astype(jnp.bfloat16)
        o_ref[...] = _x_dot_01(out, gather_q)[:, :n_q]


def fox_sample_attention(q, k, v, logf, k_pool, v_pool, f_pool, page_table):
    DB, T, H, dh = q.shape
    n_pages = page_table.shape[1]
    past = n_pages * PAGE_SIZE
    bf = jnp.bfloat16
    width = H * T
    rows = H * dh
    f_all = jnp.concatenate([f_pool[page_table].reshape(DB, past, H), logf], axis=1)
    c = jnp.cumsum(f_all, axis=1)
    c_q = c[:, past:].transpose(0, 2, 1).reshape(DB, 1, 1, width)
    c_k = jnp.repeat(c[:, :past].reshape(DB, n_pages, PAGE_SIZE, H), T, axis=-1)
    bias_past = c_q - c_k
    tok = jnp.arange(PAGE_SIZE).reshape(1, PAGE_SIZE, 1)
    qi = (jnp.arange(width) % T).reshape(1, 1, width)
    c_new = jnp.pad(jnp.repeat(c[:, past:], T, axis=-1), ((0, 0), (0, PAGE_SIZE - T), (0, 0)))
    bias_new = jnp.where((tok <= qi) & (tok < T), c_q[:, 0] - c_new, NEG_INF)
    qh = (q * (dh ** -0.5)).transpose(0, 2, 1, 3)
    same_head = jnp.eye(H, dtype=qh.dtype).reshape(1, H, 1, H, 1)
    qbd = (qh[:, :, :, None, :] * same_head).astype(bf).reshape(DB, width, rows)
    new_t = lambda x: jnp.pad(x.astype(bf).transpose(0, 2, 3, 1),
                              ((0, 0), (0, 0), (0, 0), (0, PAGE_SIZE - T))).reshape(DB, rows, PAGE_SIZE)
    pool_t = lambda x: x.transpose(0, 2, 3, 1)
    body = functools.partial(_fox_sample_body, n_q=T)
    pps = FOX_PAGES_PER_STEP
    page = lambda i: pl.BlockSpec((None, H, dh, PAGE_SIZE), lambda b, p, pt: (pt[b, p * pps + i], 0, 0, 0))
    per_b = lambda shape: pl.BlockSpec((None,) + shape, lambda b, p, pt: (b,) + (0,) * len(shape))
    kt, vt = pool_t(k_pool), pool_t(v_pool)
    o = pl.pallas_call(
        body,
        grid_spec=pltpu.PrefetchScalarGridSpec(
            num_scalar_prefetch=1,
            grid=(DB, n_pages // pps),
            in_specs=[per_b((width, rows)),
                      pl.BlockSpec((None, pps, PAGE_SIZE, width), lambda b, p, pt: (b, p, 0, 0)),
                      per_b((rows, PAGE_SIZE)), per_b((rows, PAGE_SIZE)), per_b((PAGE_SIZE, width))]
                     + [page(i) for i in range(pps)] * 2,
            out_specs=per_b((rows, T)),
            scratch_shapes=[pltpu.VMEM((1, width), jnp.float32), pltpu.VMEM((1, width), jnp.float32),
                            pltpu.VMEM((rows, width), jnp.float32)]),
        out_shape=jax.ShapeDtypeStruct((DB, rows, T), jnp.float32),
        compiler_params=pltpu.CompilerParams(dimension_semantics=("parallel", "arbitrary"),
                                             vmem_limit_bytes=VMEM_LIMIT_BYTES),
        name="fox_sample_attention",
    )(page_table, qbd, bias_past, new_t(k), new_t(v), bias_new, *([kt] * pps), *([vt] * pps))
    return o.reshape(DB, H, dh, T).transpose(0, 3, 1, 2).reshape(DB, T, H * dh)


def _compress_body(pt_ref, newk_ref, newv_ref, pe_ref, w1_ref, w2_ref, *rest, n_tok, n_out):
    pps = SAMPLE_PAGES_PER_STEP
    k_pages, v_pages = rest[:pps], rest[pps:2 * pps]
    ko_ref, vo_ref, xk_ref, xv_ref, flat_ref = rest[2 * pps:]
    step = pl.program_id(1)
    rows = newk_ref.shape[1]
    pairs = rows // LANES
    for i in range(pps):
        dst = pl.ds(pl.multiple_of((step * pps + i) * PAGE_SIZE, PAGE_SIZE), PAGE_SIZE)
        for x_ref, pages in ((xk_ref, k_pages), (xv_ref, v_pages)):
            page_t = pages[i][...].reshape(rows, PAGE_SIZE).T
            for gp in range(pairs):
                x_ref[gp, dst, :] = page_t[:, gp * LANES:(gp + 1) * LANES]

    @pl.when(step == pl.num_programs(1) - 1)
    def _():
        n_new = newk_ref.shape[0]
        tail = xk_ref.shape[1] - n_tok - n_new
        for which, (x_ref, new_ref, o_ref) in enumerate(((xk_ref, newk_ref, ko_ref), (xv_ref, newv_ref, vo_ref))):
            for gp in range(pairs):
                x_ref[gp, n_tok:n_tok + n_new, :] = new_ref[:, gp * LANES:(gp + 1) * LANES]
                x_ref[gp, n_tok + n_new:, :] = jnp.zeros((tail, LANES), jnp.float32)
            for l in range(CMP_LEN):
                for gp in range(pairs):
                    blk = x_ref[gp, pl.ds(l, n_out, stride=CMP_STRIDE), :]
                    blk = blk + pe_ref[which, l:l + 1, :]
                    flat_ref[gp * n_out:(gp + 1) * n_out, l * LANES:(l + 1) * LANES] = blk.astype(jnp.bfloat16)
            hidden = _gelu_exact(jnp.dot(flat_ref[...], w1_ref[which], preferred_element_type=jnp.float32))
            out = jnp.dot(hidden.astype(jnp.bfloat16), w2_ref[which], preferred_element_type=jnp.float32)
            o_ref[...] = jnp.concatenate([out[gp * n_out:(gp + 1) * n_out] for gp in range(pairs)],
                                         axis=1).astype(o_ref.dtype)


def _pair_weights(w1, w2, pe):
    dh = w2.shape[0]
    eye2 = jnp.eye(2, dtype=w1.dtype)
    w1p = jnp.einsum('lde,pq->lpdqe', w1.reshape(CMP_LEN, dh, dh), eye2).reshape(CMP_LEN * 2 * dh, 2 * dh)
    w2p = jnp.einsum('de,pq->pdqe', w2, eye2).reshape(2 * dh, 2 * dh)
    pep = jnp.concatenate([pe, pe], axis=1)
    return w1p, w2p, pep


def compress_paged(k_pool, v_pool, new_k, new_v, page_table, weights_k, weights_v):
    DB, T, G, dh = new_k.shape
    bf = jnp.bfloat16
    rows = G * dh
    n_pages = page_table.shape[1]
    past = n_pages * PAGE_SIZE
    lp = _round_up(past + T, SEL_BLOCK)
    n_cmp = lp // CMP_STRIDE - CMP_LEN // CMP_STRIDE + 1
    n_out = _round_up(n_cmp, 8)
    x_rows = _round_up(CMP_LEN + CMP_STRIDE * (n_out - 1), PAGE_SIZE)
    wk = _pair_weights(weights_k[1], weights_k[2], weights_k[0])
    wv = _pair_weights(weights_v[1], weights_v[2], weights_v[0])
    w1 = jnp.stack([wk[0], wv[0]]).astype(bf)
    w2 = jnp.stack([wk[1], wv[1]]).astype(bf)
    pe = jnp.stack([wk[2], wv[2]])
    native = lambda x: x.transpose(0, 2, 3, 1)
    pps = SAMPLE_PAGES_PER_STEP
    body = functools.partial(_compress_body, n_tok=past, n_out=n_out)
    per_b = lambda shape: pl.BlockSpec((None,) + shape, lambda b, s, pt: (b,) + (0,) * len(shape))
    const = lambda x: pl.BlockSpec(x.shape, lambda b, s, pt: (0,) * x.ndim)
    page = lambda i: pl.BlockSpec((None, G, dh, PAGE_SIZE), lambda b, s, pt: (pt[b, s * pps + i], 0, 0, 0))
    kt, vt = native(k_pool), native(v_pool)
    out_shape = jax.ShapeDtypeStruct((DB, n_out, rows), bf)
    return pl.pallas_call(
        body,
        grid_spec=pltpu.PrefetchScalarGridSpec(
            num_scalar_prefetch=1,
            grid=(DB, n_pages // pps),
            in_specs=[per_b((T, rows)), per_b((T, rows)), const(pe), const(w1), const(w2)]
                     + [page(i) for i in range(pps)] * 2,
            out_specs=[per_b((n_out, rows)), per_b((n_out, rows))],
            scratch_shapes=[pltpu.VMEM((rows // LANES, x_rows, LANES), jnp.float32),
                            pltpu.VMEM((rows // LANES, x_rows, LANES), jnp.float32),
                            pltpu.VMEM((rows // LANES * n_out, CMP_LEN * LANES), bf)]),
        out_shape=[out_shape, out_shape],
        compiler_params=pltpu.CompilerParams(dimension_semantics=("parallel", "arbitrary"),
                                             vmem_limit_bytes=VMEM_LIMIT_BYTES),
        name="nsa_compress_paged",
    )(page_table, new_k.reshape(DB, T, rows), new_v.reshape(DB, T, rows), pe, w1, w2, *([kt] * pps), *([vt] * pps))


def compress(x, pe, w1, w2):
    B, T, G, dh = x.shape
    r = CMP_LEN // CMP_STRIDE
    n_chunks = T // CMP_STRIDE
    nc = n_chunks - r + 1
    ch = x.reshape(B, n_chunks, CMP_STRIDE, G, dh)
    blk = jnp.concatenate([ch[:, m:m + nc] for m in range(r)], axis=2)
    blk = blk + pe[:, None, :]
    flat = blk.transpose(0, 1, 3, 2, 4).reshape(B, nc, G, CMP_LEN * dh)
    return jax.nn.gelu(flat @ w1, approximate=False) @ w2


def nsa_split(proj):
    B, T, _ = proj.shape
    nq = NSA_HEADS * HEAD_DIM
    nkv = NSA_KV_HEADS * HEAD_DIM
    q = proj[..., :nq].reshape(B, T, NSA_KV_HEADS, NSA_GROUP, HEAD_DIM)
    kv = proj[..., nq:nq + 6 * nkv].reshape(B, T, 6, NSA_KV_HEADS, HEAD_DIM)
    gl = proj[..., nq + 6 * nkv:nq + 6 * nkv + 3 * NSA_HEADS].reshape(B, T, NSA_KV_HEADS, NSA_GROUP, 3)
    return q, gl, [kv[:, :, s] for s in range(6)]


def nsa_prompt(proj, cw, rel_bias):
    S = proj.shape[1]
    pe_k, w1_k, w2_k, pe_v, w1_v, w2_v = cw
    q, gl, (kc, vc, ks, vs, kw, vw) = nsa_split(proj)
    kcmp = compress(kc, pe_k, w1_k, w2_k)
    vcmp = compress(vc, pe_v, w1_v, w2_v)
    o = nsa_prompt_attention(q, gl, kcmp, vcmp, ks, vs, kw, vw, rel_bias)
    wb = min(WINDOW, S)
    return o, [kc, vc, ks, vs, kw[:, S - wb:], vw[:, S - wb:]]


def nsa_sample(proj, cw, rel_bias, ck_pool, cv_pool, sk_pool, sv_pool, wk_buf, wv_buf, page_table):
    DB, T, _ = proj.shape
    pe_k, w1_k, w2_k, pe_v, w1_v, w2_v = cw
    q, gl, (kc, vc, ks, vs, kw, vw) = nsa_split(proj)
    kcmp, vcmp = compress_paged(ck_pool, cv_pool, kc, vc, page_table, (pe_k, w1_k, w2_k), (pe_v, w1_v, w2_v))
    kw_all = jnp.concatenate([wk_buf, kw], axis=1)
    vw_all = jnp.concatenate([wv_buf, vw], axis=1)
    o = nsa_sample_attention(q, gl, kcmp, vcmp, (ks, vs, kw, vw), sk_pool, sv_pool, wk_buf, wv_buf,
                             page_table, rel_bias)
    return o, [kc, vc, ks, vs, kw_all[:, T:], vw_all[:, T:]]


def kernel(x_prompt, x_sample, p_prompt, p_sample, cache_fox_k, cache_fox_v, cache_fox_logf,
           cache_nsa_cmp_k, cache_nsa_cmp_v, cache_nsa_sel_k, cache_nsa_sel_v,
           cache_nsa_win_k, cache_nsa_win_v, page_table,
           norm_mix, norm_ffn, norm_ple, norm_final,
           fox_w_in, fox_b_f, fox_w_out, nsa_w_in, nsa_w_out,
           cmp_pe_k, cmp_w1_k, cmp_w2_k, cmp_pe_v, cmp_w1_v, cmp_w2_v, rel_bias,
           peer_wq, peer_sub_k1, peer_sub_k2, peer_u, peer_v, ple_w_proj, ple_w_gate):
    B, S, d = x_prompt.shape
    DB, T, _ = x_sample.shape
    n_p, n_s = B * S, DB * T
    bf = jnp.bfloat16
    rows = lambda a, b: jnp.concatenate([a.reshape(n_p, -1), b.reshape(n_s, -1)], axis=0)
    x = rows(x_prompt, x_sample)
    fox_p, fox_s, nsa_p, nsa_s = [], [], [], []
    for i in range(DEPTH):
        j = i // 2
        if i % 2 == 0:
            h = rmsnorm_pallas(x, norm_mix[i], bf)
            proj = linear_pallas(h, fox_w_in[j])
            nh = FOX_HEADS * HEAD_DIM
            logf = jax.nn.log_sigmoid(proj[:, 3 * nh:3 * nh + FOX_HEADS] + fox_b_f[j])
            heads = lambda a, lead: a.reshape(lead + (FOX_HEADS, -1))
            qp, kp, vp = (heads(proj[:n_p, s * nh:(s + 1) * nh], (B, S)) for s in range(3))
            qs, ks_, vs_ = (heads(proj[n_p:, s * nh:(s + 1) * nh], (DB, T)) for s in range(3))
            fp, fs = logf[:n_p].reshape(B, S, FOX_HEADS), logf[n_p:].reshape(DB, T, FOX_HEADS)
            op = fox_prompt_attention(qp, kp, vp, fp)
            os_ = fox_sample_attention(qs, ks_, vs_, fs, cache_fox_k[j], cache_fox_v[j], cache_fox_logf[j],
                                       page_table)
            fox_p.append([kp, vp, fp])
            fox_s.append([ks_, vs_, fs])
            x = linear_pallas(rows(op, os_), fox_w_out[j], residual=x)
        else:
            cw = (cmp_pe_k[j], cmp_w1_k[j], cmp_w2_k[j], cmp_pe_v[j], cmp_w1_v[j], cmp_w2_v[j])
            h = rmsnorm_pallas(x, norm_mix[i], bf)
            proj = linear_pallas(h, nsa_w_in[j])
            op, stp = nsa_prompt(proj[:n_p].reshape(B, S, -1), cw, rel_bias)
            os_, sts = nsa_sample(proj[n_p:].reshape(DB, T, -1), cw, rel_bias,
                                  cache_nsa_cmp_k[j], cache_nsa_cmp_v[j], cache_nsa_sel_k[j],
                                  cache_nsa_sel_v[j], cache_nsa_win_k[j], cache_nsa_win_v[j], page_table)
            nsa_p.append(stp)
            nsa_s.append(sts)
            x = linear_pallas(rows(op, os_), nsa_w_out[j], residual=x)
        h = rmsnorm_pallas(x, norm_ffn[i], bf)
        x = peer_pallas(x, h, peer_wq[i].T.astype(bf), peer_sub_k1[i].astype(bf), peer_sub_k2[i].astype(bf),
                        peer_u[i].astype(bf), peer_v[i].astype(bf))
        x = ple_pallas(x, rows(p_prompt[i], p_sample[i]), norm_ple[i], ple_w_proj[i], ple_w_gate[i])
    y = rmsnorm_pallas(x, norm_final, jnp.float32)
    st = lambda lst, k: jnp.stack([s[k] for s in lst])
    return (y[:n_p].reshape(B, S, d), y[n_p:].reshape(DB, T, d),
            st(fox_p, 0), st(fox_p, 1), st(fox_p, 2),
            st(fox_s, 0), st(fox_s, 1), st(fox_s, 2),
            st(nsa_p, 0), st(nsa_p, 1), st(nsa_p, 2), st(nsa_p, 3), st(nsa_p, 4), st(nsa_p, 5),
            st(nsa_s, 0), st(nsa_s, 1), st(nsa_s, 2), st(nsa_s, 3), st(nsa_s, 4), st(nsa_s, 5))
```
